```python
import math, functools
import jax, jax.numpy as jnp
from jax import lax
import numpy as np

D_MODEL = 1024
BATCH = 2
SEQ = 8192
DEPTH = 4
DEC_BATCH = 32
DEC_SEQ = 1
PAST_LEN = 8192
PAGE_SIZE = 128

N_A_LAYERS = DEPTH // 2
N_B_LAYERS = DEPTH - N_A_LAYERS
SSM_WIDTH = D_MODEL
SSM_GROUP = 16
SSM_GROUPS = SSM_WIDTH // SSM_GROUP
SSM_STATE = 64
HEAD_DIM = 128
N_HEADS = D_MODEL // HEAD_DIM
N_KV_HEADS = N_HEADS
ATT_WIDTH = N_HEADS * HEAD_DIM
MOBA_BLOCK = 256
MOBA_TOPK = 3
Q_CHUNK = 64
ROPE_THETA = 10000.0
RMS_EPS = 1e-6
NEG_INF = -1e30
DT_MIN = 1e-3
DT_MAX = 1e-1

kernel_name = 'yoco_s5_moba_step'


def rmsnorm(x, g):
    xf = x.astype(jnp.float32)
    r = lax.rsqrt(jnp.mean(xf * xf, axis=-1, keepdims=True) + RMS_EPS)
    return (xf * r * g.astype(jnp.float32)).astype(x.dtype)


def rope(x, pos):
    half = HEAD_DIM // 2
    inv = ROPE_THETA ** (-jnp.arange(half, dtype=jnp.float32) / half)
    ang = pos.astype(jnp.float32)[:, None] * inv[None, :]
    cos = jnp.cos(ang)[None, :, None, :]
    sin = jnp.sin(ang)[None, :, None, :]
    xf = x.astype(jnp.float32)
    x1, x2 = xf[..., :half], xf[..., half:]
    return jnp.concatenate([x1 * cos - x2 * sin, x2 * cos + x1 * sin], axis=-1).astype(x.dtype)


def _complex_affine_combine(e1, e2):
    a1r, a1i, b1r, b1i = e1
    a2r, a2i, b2r, b2i = e2
    return (a2r * a1r - a2i * a1i, a2r * a1i + a2i * a1r,
            a2r * b1r - a2i * b1i + b2r, a2r * b1i + a2i * b1r + b2i)


def s5_mixer(u, h0_re, h0_im, lam_re, lam_im, log_dt, b_re, b_im, c_re, c_im, d_skip):
    n, L, _ = u.shape
    f32 = jnp.float32
    uf = u.astype(f32).reshape(n, L, SSM_GROUPS, SSM_GROUP)
    dt = jnp.exp(log_dt.astype(f32))[:, None]
    lr = lam_re.astype(f32)
    li = lam_im.astype(f32)
    decay = jnp.exp(lr * dt)
    abar_re = decay * jnp.cos(li * dt)
    abar_im = decay * jnp.sin(li * dt)
    den = lr * lr + li * li
    f_re = ((abar_re - 1.0) * lr + abar_im * li) / den
    f_im = (abar_im * lr - (abar_re - 1.0) * li) / den
    br = b_re.astype(f32)
    bi = b_im.astype(f32)
    bb_re = f_re[..., None] * br - f_im[..., None] * bi
    bb_im = f_re[..., None] * bi + f_im[..., None] * br
    bu_re = jnp.einsum('blgc,gpc->lbgp', uf, bb_re)
    bu_im = jnp.einsum('blgc,gpc->lbgp', uf, bb_im)
    a_re = jnp.broadcast_to(abar_re[None, None], (L, 1, SSM_GROUPS, SSM_STATE))
    a_im = jnp.broadcast_to(abar_im[None, None], (L, 1, SSM_GROUPS, SSM_STATE))
    acum_re, acum_im, s_re, s_im = lax.associative_scan(
        _complex_affine_combine, (a_re, a_im, bu_re, bu_im), axis=0)
    h0r = h0_re.astype(f32)[None]
    h0i = h0_im.astype(f32)[None]
    x_re = s_re + acum_re * h0r - acum_im * h0i
    x_im = s_im + acum_re * h0i + acum_im * h0r
    y = (jnp.einsum('lbgp,gcp->blgc', x_re, c_re.astype(f32))
         - jnp.einsum('lbgp,gcp->blgc', x_im, c_im.astype(f32)))
    y = y.reshape(n, L, SSM_WIDTH) + d_skip.astype(f32) * u.astype(f32)
    return y.astype(u.dtype), x_re[-1], x_im[-1]


def s5_layer(x, h0_re, h0_im, norm_g, w_in, lam_re, lam_im, log_dt, b_re, b_im, c_re, c_im, d_skip, w_glu, w_out):
    h = rmsnorm(x, norm_g)
    uz = h @ w_in
    u, z = uz[..., :SSM_WIDTH], uz[..., SSM_WIDTH:]
    y, h_re, h_im = s5_mixer(u, h0_re, h0_im, lam_re, lam_im, log_dt, b_re, b_im, c_re, c_im, d_skip)
    g = jax.nn.gelu(y)
    y = g * jax.nn.sigmoid(g @ w_glu)
    return x + (y * jax.nn.silu(z)) @ w_out, h_re, h_im


def shared_kv(x, pos, kv_norm, w_kv):
    n, L, _ = x.shape
    kv = rmsnorm(x, kv_norm) @ w_kv
    k = kv[..., :ATT_WIDTH].reshape(n, L, N_KV_HEADS, HEAD_DIM)
    v = kv[..., ATT_WIDTH:].reshape(n, L, N_KV_HEADS, HEAD_DIM)
    return rope(k, pos), v


def block_means_contig(k):
    n, L, H, hd = k.shape
    nb = -(-L // MOBA_BLOCK)
    kp = jnp.pad(k.astype(jnp.float32), ((0, 0), (0, nb * MOBA_BLOCK - L), (0, 0), (0, 0)))
    return kp.reshape(n, nb, MOBA_BLOCK, H, hd).mean(axis=2)


def block_means_paged(cache_k, page_table, k_new):
    n_pages = page_table.shape[1]
    past_len = n_pages * PAGE_SIZE
    L = k_new.shape[1]
    nb = -(-(past_len + L) // MOBA_BLOCK)
    page_sums = cache_k[page_table].astype(jnp.float32).sum(axis=2)
    rows = jnp.concatenate([page_sums, k_new.astype(jnp.float32)], axis=1)
    seg = jnp.concatenate([(jnp.arange(n_pages, dtype=jnp.int32) * PAGE_SIZE) // MOBA_BLOCK,
                           (past_len + jnp.arange(L, dtype=jnp.int32)) // MOBA_BLOCK])
    sums = jax.ops.segment_sum(rows.transpose(1, 0, 2, 3), seg, num_segments=nb)
    return sums.transpose(1, 0, 2, 3) / MOBA_BLOCK


def fetch_contig(k, v, pos):
    n, L, H, _ = k.shape
    p = jnp.clip(pos, 0, L - 1)
    bi = jnp.arange(n)[:, None, None, None, None]
    hi = jnp.arange(H)[None, None, :, None, None]
    return k[bi, p, hi], v[bi, p, hi]


def fetch_paged(cache_k, cache_v, page_table, k_new, v_new, pos):
    n, n_pages = page_table.shape
    past_len = n_pages * PAGE_SIZE
    L = k_new.shape[1]
    bi = jnp.arange(n)[:, None, None, None, None]
    hi = jnp.arange(N_KV_HEADS)[None, None, :, None, None]
    pp = jnp.clip(pos, 0, past_len - 1)
    phys = page_table[bi, pp // PAGE_SIZE]
    off = pp % PAGE_SIZE
    pn = jnp.clip(pos - past_len, 0, L - 1)
    is_past = (pos < past_len)[..., None]
    kk = jnp.where(is_past, cache_k[phys, off, hi], k_new[bi, pn, hi])
    vv = jnp.where(is_past, cache_v[phys, off, hi], v_new[bi, pn, hi])
    return kk, vv


def moba_chunk(q, q_pos, kbar, fetch):
    n, C, H, _ = q.shape
    nb = kbar.shape[1]
    k_sel = min(MOBA_TOPK, nb)
    own = q_pos // MOBA_BLOCK
    s = jnp.einsum('bchd,bnhd->bchn', q.astype(jnp.float32), kbar)
    past = jnp.arange(nb, dtype=jnp.int32)[None, :] < own[:, None]
    s = jnp.where(past[None, :, None, :], s, NEG_INF)
    _, top = lax.top_k(s, k_sel)
    top_ok = top < own[None, :, None, None]
    own_b = jnp.broadcast_to(own[None, :, None, None], (n, C, H, 1)).astype(top.dtype)
    blocks = jnp.concatenate([top, own_b], axis=-1)
    blk_ok = jnp.concatenate([top_ok, jnp.ones((n, C, H, 1), dtype=bool)], axis=-1)
    pos = blocks[..., None] * MOBA_BLOCK + jnp.arange(MOBA_BLOCK, dtype=jnp.int32)
    kk, vv = fetch(pos)
    logits = jnp.einsum('bchd,bchksd->bchks', q, kk, preferred_element_type=jnp.float32) * (HEAD_DIM ** -0.5)
    ok = blk_ok[..., None] & (pos <= q_pos[None, :, None, None, None])
    logits = jnp.where(ok, logits, NEG_INF).reshape(n, C, H, -1)
    p = jax.nn.softmax(logits, axis=-1).astype(vv.dtype)
    return jnp.einsum('bchm,bchmd->bchd', p, vv.reshape(n, C, H, -1, HEAD_DIM))


def moba_attend(q, q_pos, kbar, fetch):
    n, L, H, hd = q.shape
    if L > Q_CHUNK and L % Q_CHUNK == 0:
        nc = L // Q_CHUNK
        qc = q.reshape(n, nc, Q_CHUNK, H, hd).transpose(1, 0, 2, 3, 4)
        pc = q_pos.reshape(nc, Q_CHUNK)
        out = lax.map(lambda a: moba_chunk(a[0], a[1], kbar, fetch), (qc, pc))
        return out.transpose(1, 0, 2, 3, 4).reshape(n, L, H, hd)
    return moba_chunk(q, q_pos, kbar, fetch)


def moba_layer(x, pos, kbar, fetch, norm_g, w_in, w_out):
    n, L, _ = x.shape
    qz = rmsnorm(x, norm_g) @ w_in
    q = rope(qz[..., :ATT_WIDTH].reshape(n, L, N_HEADS, HEAD_DIM), pos)
    z = qz[..., ATT_WIDTH:]
    o = moba_attend(q, pos, kbar, fetch).reshape(n, L, ATT_WIDTH).astype(x.dtype)
    return x + (o * jax.nn.silu(z)) @ w_out


def setup_inputs(seed: int = 0) -> dict:
    key = jax.random.key(seed)
    ks = jax.random.split(key, 32)
    f32 = jnp.float32
    n_pages = PAST_LEN // PAGE_SIZE
    n_used = DEC_BATCH * n_pages
    n_pool = n_used + -(-n_used // 4)

    def nrm(k, shape, s):
        return jax.random.normal(k, shape, f32) * s

    G, P, GS, E = SSM_GROUPS, SSM_STATE, SSM_GROUP, SSM_WIDTH
    return {
        'x_prompt': nrm(ks[0], (BATCH, SEQ, D_MODEL), 1.0),
        'x_sample': nrm(ks[1], (DEC_BATCH, DEC_SEQ, D_MODEL), 1.0),
        'state_ssm_re': nrm(ks[2], (N_A_LAYERS, DEC_BATCH, G, P), 0.5),
        'state_ssm_im': nrm(ks[3], (N_A_LAYERS, DEC_BATCH, G, P), 0.5),
        'cache_k': nrm(ks[4], (n_pool, PAGE_SIZE, N_KV_HEADS, HEAD_DIM), 1.0),
        'cache_v': nrm(ks[5], (n_pool, PAGE_SIZE, N_KV_HEADS, HEAD_DIM), 1.0),
        'page_table': jax.random.permutation(ks[6], n_pool)[:n_used].reshape(DEC_BATCH, n_pages).astype(jnp.int32),
        'a_norm': 1.0 + nrm(ks[7], (N_A_LAYERS, D_MODEL), 0.01),
        'a_w_in': nrm(ks[8], (N_A_LAYERS, D_MODEL, 2 * E), D_MODEL ** -0.5),
        'a_lam_re': -0.5 + nrm(ks[9], (N_A_LAYERS, G, P), 0.01),
        'a_lam_im': jnp.pi * jnp.arange(P, dtype=f32) + nrm(ks[10], (N_A_LAYERS, G, P), 0.01),
        'a_log_dt': jax.random.uniform(ks[11], (N_A_LAYERS, G), f32, math.log(DT_MIN), math.log(DT_MAX)),
        'a_b_re': nrm(ks[12], (N_A_LAYERS, G, P, GS), (2 * GS) ** -0.5),
        'a_b_im': nrm(ks[13], (N_A_LAYERS, G, P, GS), (2 * GS) ** -0.5),
        'a_c_re': nrm(ks[14], (N_A_LAYERS, G, GS, P), P ** -0.5),
        'a_c_im': nrm(ks[15], (N_A_LAYERS, G, GS, P), P ** -0.5),
        'a_d': nrm(ks[16], (N_A_LAYERS, E), 0.5),
        'a_w_glu': nrm(ks[17], (N_A_LAYERS, E, E), E ** -0.5),
        'a_w_out': nrm(ks[18], (N_A_LAYERS, E, D_MODEL), E ** -0.5),
        'kv_norm': 1.0 + nrm(ks[19], (D_MODEL,), 0.01),
        'w_kv': nrm(ks[20], (D_MODEL, 2 * ATT_WIDTH), D_MODEL ** -0.5),
        'b_norm': 1.0 + nrm(ks[21], (N_B_LAYERS, D_MODEL), 0.01),
        'b_w_in': nrm(ks[22], (N_B_LAYERS, D_MODEL, 2 * ATT_WIDTH), D_MODEL ** -0.5),
        'b_w_out': nrm(ks[23], (N_B_LAYERS, ATT_WIDTH, D_MODEL), ATT_WIDTH ** -0.5),
        'final_norm': 1.0 + nrm(ks[24], (D_MODEL,), 0.01),
    }


def reference(x_prompt, x_sample, state_ssm_re, state_ssm_im, cache_k, cache_v, page_table,
              a_norm, a_w_in, a_lam_re, a_lam_im, a_log_dt, a_b_re, a_b_im, a_c_re, a_c_im, a_d,
              a_w_glu, a_w_out, kv_norm, w_kv, b_norm, b_w_in, b_w_out, final_norm):
    L_p = x_prompt.shape[1]
    L_s = x_sample.shape[1]
    past_len = page_table.shape[1] * PAGE_SIZE
    pos_p = jnp.arange(L_p, dtype=jnp.int32)
    pos_s = past_len + jnp.arange(L_s, dtype=jnp.int32)
    xp, xs = x_prompt, x_sample
    ssm_p_re, ssm_p_im, ssm_s_re, ssm_s_im = [], [], [], []
    for l in range(DEPTH):
        if l < N_A_LAYERS:
            prm = (a_norm[l], a_w_in[l], a_lam_re[l], a_lam_im[l], a_log_dt[l], a_b_re[l], a_b_im[l],
                   a_c_re[l], a_c_im[l], a_d[l], a_w_glu[l], a_w_out[l])
            zeros = jnp.zeros((xp.shape[0], SSM_GROUPS, SSM_STATE), jnp.float32)
            xp, hr, hi = s5_layer(xp, zeros, zeros, *prm)
            ssm_p_re.append(hr)
            ssm_p_im.append(hi)
            xs, hr, hi = s5_layer(xs, state_ssm_re[l], state_ssm_im[l], *prm)
            ssm_s_re.append(hr)
            ssm_s_im.append(hi)
            if l == N_A_LAYERS - 1:
                k_p, v_p = shared_kv(xp, pos_p, kv_norm, w_kv)
                k_s, v_s = shared_kv(xs, pos_s, kv_norm, w_kv)
                kbar_p = block_means_contig(k_p)
                kbar_s = block_means_paged(cache_k, page_table, k_s)
                fetch_p = functools.partial(fetch_contig, k_p, v_p)
                fetch_s = functools.partial(fetch_paged, cache_k, cache_v, page_table, k_s, v_s)
        else:
            j = l - N_A_LAYERS
            xp = moba_layer(xp, pos_p, kbar_p, fetch_p, b_norm[j], b_w_in[j], b_w_out[j])
            xs = moba_layer(xs, pos_s, kbar_s, fetch_s, b_norm[j], b_w_in[j], b_w_out[j])
    y_prompt = rmsnorm(xp, final_norm)
    y_sample = rmsnorm(xs, final_norm)
    return (y_prompt, y_sample, jnp.stack(ssm_p_re), jnp.stack(ssm_p_im), jnp.stack(ssm_s_re),
            jnp.stack(ssm_s_im), k_p, v_p, k_s, v_s)
```

```python
import functools
import math

import jax
import jax.numpy as jnp
from jax import lax
from jax.experimental import pallas as pl
from jax.experimental.pallas import tpu as pltpu

D_MODEL = 1024
SSM_GROUP = 16
SSM_GROUPS = D_MODEL // SSM_GROUP
SSM_STATE = 64
SSM_TBLK = 16
SSM_BLKW = SSM_TBLK * SSM_GROUP
HEAD_DIM = 128
N_HEADS = D_MODEL // HEAD_DIM
MOBA_BLOCK = 256
MOBA_TOPK = 3
PAGE_SIZE = 128
ROPE_THETA = 10000.0
RMS_EPS = 1e-6
NEG_INF = -1e30
LANES = 128
VMEM_LIMIT = 48 * 1024 * 1024

F32 = jnp.float32
BF16 = jnp.bfloat16


def _params(*sem):
    return pltpu.CompilerParams(dimension_semantics=sem, vmem_limit_bytes=VMEM_LIMIT)


def _rmsnorm(x, g):
    r = lax.rsqrt(jnp.mean(x * x, axis=-1, keepdims=True) + RMS_EPS)
    return x * r * g


def _dot(a, b):
    return jnp.dot(a, b, preferred_element_type=F32)


def _dot_nt(a, b, precision=None):
    return lax.dot_general(a, b, (((1,), (1,)), ((), ())), precision=precision,
                           preferred_element_type=F32)


def _rope(x, cos2, sin2):
    return x * cos2 + pltpu.roll(x, HEAD_DIM // 2, 1) * sin2


def _silu(z):
    return z * jax.nn.sigmoid(z)


def _top3(s, valid):
    n_lanes = s.shape[1]
    lane = lax.broadcasted_iota(jnp.int32, s.shape, 1)
    sm = jnp.where(valid, s, NEG_INF)
    sel = jnp.zeros(s.shape, F32)
    picks = []
    for _ in range(MOBA_TOPK):
        m = jnp.max(sm, axis=1, keepdims=True)
        idx = jnp.min(jnp.where(sm == m, lane, n_lanes), axis=1, keepdims=True)
        hit = lane == idx
        sel = jnp.where(hit, jnp.where(valid, 1.0, sel), sel)
        sm = jnp.where(hit, NEG_INF, sm)
        picks.append(idx)
    return sel, picks


def _a_in_kernel(x_ref, g_ref, w_ref, u_ref, z_ref):
    h = _rmsnorm(x_ref[...], g_ref[...]).astype(BF16)
    uz = _dot(h, w_ref[...])
    u_ref[...] = uz[:, :D_MODEL]
    z_ref[...] = uz[:, D_MODEL:]


def _a_in(x, g, w, tm):
    t = x.shape[0]
    return pl.pallas_call(
        _a_in_kernel,
        grid=(t // tm,),
        in_specs=[pl.BlockSpec((tm, D_MODEL), lambda i: (i, 0)),
                  pl.BlockSpec((1, D_MODEL), lambda i: (0, 0)),
                  pl.BlockSpec((D_MODEL, 2 * D_MODEL), lambda i: (0, 0))],
        out_specs=[pl.BlockSpec((tm, D_MODEL), lambda i: (i, 0)),
                   pl.BlockSpec((tm, D_MODEL), lambda i: (i, 0))],
        out_shape=[jax.ShapeDtypeStruct((t, D_MODEL), F32)] * 2,
        compiler_params=_params("parallel"),
        name="a_in",
    )(x, g, w)


def _ssm_scan_kernel(n_seq, ug_ref, m_ref, f_ref, e_ref, ap_ref, y_ref, st_ref):
    rows = ug_ref.shape[0]
    nblk = rows // n_seq
    ug = ug_ref[...]
    s = _dot(ug, f_ref[...])
    row = lax.broadcasted_iota(jnp.int32, (nblk, LANES), 0)
    hp = []
    for q in range(n_seq):
        re = s[q * nblk:(q + 1) * nblk, :LANES]
        im = s[q * nblk:(q + 1) * nblk, LANES:]
        step, k = 1, 0
        while step < nblk:
            ar = ap_ref[2 * k:2 * k + 1, :]
            ai = ap_ref[2 * k + 1:2 * k + 2, :]
            sr = jnp.where(row >= step, pltpu.roll(re, step, 0), 0.0)
            si = jnp.where(row >= step, pltpu.roll(im, step, 0), 0.0)
            re, im = re + ar * sr - ai * si, im + ar * si + ai * sr
            step, k = step * 2, k + 1
        st_ref[2 * q:2 * q + 1, :] = re[nblk - 1:nblk, :]
        st_ref[2 * q + 1:2 * q + 2, :] = im[nblk - 1:nblk, :]
        pr = jnp.where(row >= 1, pltpu.roll(re, 1, 0), 0.0)
        pi = jnp.where(row >= 1, pltpu.roll(im, 1, 0), 0.0)
        hp.append(jnp.concatenate([pr, pi], axis=1))
    hprev = jnp.concatenate(hp, axis=0).astype(BF16)
    carry = _dot(hprev, e_ref[...])
    y_ref[:, :SSM_BLKW] = _dot(ug[:, :SSM_BLKW], m_ref[0]) + carry[:, :SSM_BLKW]
    y_ref[:, SSM_BLKW:] = _dot(ug[:, SSM_BLKW:], m_ref[1]) + carry[:, SSM_BLKW:]


def _ssm_scan(ug, prep, n_seq):
    rows = ug.shape[0]
    n_pairs = SSM_GROUPS // 2
    n_pow = prep["apow"].shape[1]
    return pl.pallas_call(
        functools.partial(_ssm_scan_kernel, n_seq),
        grid=(n_pairs,),
        in_specs=[pl.BlockSpec((rows, 2 * SSM_BLKW), lambda p: (0, p)),
                  pl.BlockSpec((2, SSM_BLKW, SSM_BLKW), lambda p: (p, 0, 0)),
                  pl.BlockSpec((None, 2 * SSM_BLKW, 2 * LANES), lambda p: (p, 0, 0)),
                  pl.BlockSpec((None, 2 * LANES, 2 * SSM_BLKW), lambda p: (p, 0, 0)),
                  pl.BlockSpec((None, n_pow, LANES), lambda p: (p, 0, 0))],
        out_specs=[pl.BlockSpec((rows, 2 * SSM_BLKW), lambda p: (0, p)),
                   pl.BlockSpec((None, 2 * n_seq, LANES), lambda p: (p, 0, 0))],
        out_shape=[jax.ShapeDtypeStruct((rows, SSM_GROUPS * SSM_BLKW), F32),
                   jax.ShapeDtypeStruct((n_pairs, 2 * n_seq, LANES), F32)],
        compiler_params=_params("parallel"),
        name="ssm_scan",
    )(ug, prep["m"], prep["f"], prep["e"], prep["apow"])


def _ssm_step_kernel(u_ref, hr_ref, hi_ref, bb_ref, ar_ref, ai_ref, cs_ref, y_ref, or_ref, oi_ref):
    bu = lax.dot_general(u_ref[...], bb_ref[...], (((2,), (1,)), ((0,), (0,))),
                         preferred_element_type=F32)
    h0r, h0i = hr_ref[...], hi_ref[...]
    ar, ai = ar_ref[...], ai_ref[...]
    hr = bu[:, :, :SSM_STATE] + ar * h0r - ai * h0i
    hi = bu[:, :, SSM_STATE:] + ar * h0i + ai * h0r
    or_ref[...] = hr
    oi_ref[...] = hi
    y_ref[...] = lax.dot_general(jnp.concatenate([hr, hi], axis=2), cs_ref[...],
                                 (((2,), (1,)), ((0,), (0,))), preferred_element_type=F32)


def _ssm_step(u_g, h0r, h0i, prep):
    g, n, _ = u_g.shape
    return pl.pallas_call(
        _ssm_step_kernel,
        out_shape=[jax.ShapeDtypeStruct((g, n, SSM_GROUP), F32),
                   jax.ShapeDtypeStruct((g, n, SSM_STATE), F32),
                   jax.ShapeDtypeStruct((g, n, SSM_STATE), F32)],
        compiler_params=pltpu.CompilerParams(vmem_limit_bytes=VMEM_LIMIT),
        name="ssm_step",
    )(u_g, h0r, h0i, prep["bb"], prep["a_re"], prep["a_im"], prep["cs"])


def _a_out_kernel(ys_ref, u_ref, z_ref, x_ref, d_ref, wg_ref, wo_ref, o_ref):
    y = ys_ref[...] + d_ref[...] * u_ref[...]
    g = jax.nn.gelu(y)
    y2 = g * jax.nn.sigmoid(_dot(g.astype(BF16), wg_ref[...]))
    v = (y2 * _silu(z_ref[...])).astype(BF16)
    o_ref[...] = x_ref[...] + _dot(v, wo_ref[...])


def _a_out(ys, u, z, x, d, wg, wo, tm):
    t = x.shape[0]
    tile = pl.BlockSpec((tm, D_MODEL), lambda i: (i, 0))
    full = pl.BlockSpec((D_MODEL, D_MODEL), lambda i: (0, 0))
    return pl.pallas_call(
        _a_out_kernel,
        grid=(t // tm,),
        in_specs=[tile, tile, tile, tile, pl.BlockSpec((1, D_MODEL), lambda i: (0, 0)), full, full],
        out_specs=tile,
        out_shape=jax.ShapeDtypeStruct((t, D_MODEL), F32),
        compiler_params=_params("parallel"),
        name="a_out",
    )(ys, u, z, x, d, wg, wo)


def _ssm_prep(lam_re, lam_im, log_dt, b_re, b_im, c_re, c_im, nblk):
    hi = lax.Precision.HIGHEST
    g, p = lam_re.shape
    dt = jnp.exp(log_dt)[:, None]

    def power(n):
        n = jnp.asarray(n, F32).reshape((-1, 1, 1))
        mag = jnp.exp(n * (lam_re * dt)[None])
        ang = n * (lam_im * dt)[None]
        return mag * jnp.cos(ang), mag * jnp.sin(ang)

    pw_re, pw_im = power(jnp.arange(SSM_TBLK + 1))
    abar_re, abar_im = pw_re[1], pw_im[1]
    den = lam_re * lam_re + lam_im * lam_im
    f_re = ((abar_re - 1.0) * lam_re + abar_im * lam_im) / den
    f_im = (abar_im * lam_re - (abar_re - 1.0) * lam_im) / den
    bb_re = f_re[..., None] * b_re - f_im[..., None] * b_im
    bb_im = f_re[..., None] * b_im + f_im[..., None] * b_re
    tr, ti = pw_re[:SSM_TBLK, :, :, None], pw_im[:SSM_TBLK, :, :, None]
    w_re = tr * bb_re[None] - ti * bb_im[None]
    w_im = tr * bb_im[None] + ti * bb_re[None]
    f_re_m = w_re[::-1].transpose(1, 0, 3, 2).reshape(g, SSM_BLKW, p)
    f_im_m = w_im[::-1].transpose(1, 0, 3, 2).reshape(g, SSM_BLKW, p)
    kern = (jnp.einsum('tgpi,gop->gtio', w_re, c_re, precision=hi)
            - jnp.einsum('tgpi,gop->gtio', w_im, c_im, precision=hi))
    lag = jnp.arange(SSM_TBLK)[None, :] - jnp.arange(SSM_TBLK)[:, None]
    m = jnp.where((lag >= 0)[None, :, :, None, None], kern[:, jnp.clip(lag, 0, SSM_TBLK - 1)], 0.0)
    m = m.transpose(0, 1, 3, 2, 4).reshape(g, SSM_BLKW, SSM_BLKW)
    er, ei = pw_re[1:, :, None, :], pw_im[1:, :, None, :]
    ca_re = c_re[None] * er - c_im[None] * ei
    ca_im = c_re[None] * ei + c_im[None] * er
    e_re = ca_re.transpose(1, 3, 0, 2).reshape(g, p, SSM_BLKW)
    e_im = -ca_im.transpose(1, 3, 0, 2).reshape(g, p, SSM_BLKW)

    n_pairs = g // 2
    zf = jnp.zeros((n_pairs, SSM_BLKW, p), F32)
    fp = jnp.concatenate([
        jnp.concatenate([f_re_m[0::2], zf, f_im_m[0::2], zf], axis=2),
        jnp.concatenate([zf, f_re_m[1::2], zf, f_im_m[1::2]], axis=2)], axis=1)
    ze = jnp.zeros((n_pairs, p, SSM_BLKW), F32)
    ep = jnp.concatenate([
        jnp.concatenate([e_re[0::2], ze], axis=2),
        jnp.concatenate([ze, e_re[1::2]], axis=2),
        jnp.concatenate([e_im[0::2], ze], axis=2),
        jnp.concatenate([ze, e_im[1::2]], axis=2)], axis=1)
    n_steps = max(1, math.ceil(math.log2(nblk)))
    ap_re, ap_im = power(SSM_TBLK * 2.0 ** jnp.arange(n_steps))
    ap = jnp.stack([ap_re, ap_im], axis=1)
    ap = ap.reshape(n_steps * 2, n_pairs, 2 * p).transpose(1, 0, 2)
    return {
        "m": m.astype(BF16), "f": fp.astype(BF16), "e": ep.astype(BF16), "apow": ap,
        "bb": jnp.concatenate([bb_re.transpose(0, 2, 1), bb_im.transpose(0, 2, 1)], axis=2),
        "a_re": abar_re[:, None, :], "a_im": abar_im[:, None, :],
        "cs": jnp.concatenate([c_re.transpose(0, 2, 1), -c_im.transpose(0, 2, 1)], axis=1),
    }


def _s5_layer_prompt(x, n_seq, norm, w_in, prep, d, wg, wo):
    t = x.shape[0]
    nb = t // SSM_TBLK
    u, z = _a_in(x, norm, w_in, 512)
    ug = (u.reshape(nb, SSM_TBLK, SSM_GROUPS, SSM_GROUP).transpose(0, 2, 1, 3)
          .reshape(nb, SSM_GROUPS * SSM_BLKW).astype(BF16))
    yg, st = _ssm_scan(ug, prep, n_seq)
    ys = (yg.reshape(nb, SSM_GROUPS, SSM_TBLK, SSM_GROUP).transpose(0, 2, 1, 3)
          .reshape(t, D_MODEL))
    st = st.reshape(SSM_GROUPS // 2, n_seq, 2, 2, SSM_STATE)
    st = st.transpose(2, 1, 0, 3, 4).reshape(2, n_seq, SSM_GROUPS, SSM_STATE)
    return _a_out(ys, u, z, x, d, wg, wo, 512), st[0], st[1]


def _s5_layer_sample(x, h0r, h0i, norm, w_in, prep, d, wg, wo):
    n = x.shape[0]
    u, z = _a_in(x, norm, w_in, n)
    u_g = u.reshape(n, SSM_GROUPS, SSM_GROUP).transpose(1, 0, 2)
    y_g, hr, hi = _ssm_step(u_g, h0r.transpose(1, 0, 2), h0i.transpose(1, 0, 2), prep)
    ys = y_g.transpose(1, 0, 2).reshape(n, D_MODEL)
    return _a_out(ys, u, z, x, d, wg, wo, n), hr.transpose(1, 0, 2), hi.transpose(1, 0, 2)


def _kv_kernel(with_means, x_ref, g_ref, w_ref, cos_ref, sin_ref, k_ref, v_ref, kb_ref, vb_ref, *mean_ref):
    h = _rmsnorm(x_ref[...], g_ref[...]).astype(BF16)
    kv = _dot(h, w_ref[...])
    cos2, sin2 = cos_ref[...], sin_ref[...]
    tm = kv.shape[0]
    for hd in range(N_HEADS):
        sl = slice(hd * HEAD_DIM, (hd + 1) * HEAD_DIM)
        k = _rope(kv[:, sl], cos2, sin2)
        v = kv[:, D_MODEL + hd * HEAD_DIM:D_MODEL + (hd + 1) * HEAD_DIM]
        k_ref[:, sl] = k
        v_ref[:, sl] = v
        kb_ref[hd] = k.astype(BF16)
        vb_ref[hd] = v.astype(BF16)
        if with_means:
            for b in range(tm // MOBA_BLOCK):
                blk = k[b * MOBA_BLOCK:(b + 1) * MOBA_BLOCK]
                mean_ref[0][b, :, sl] = jnp.sum(blk, axis=0, keepdims=True) * (1.0 / MOBA_BLOCK)


def _kv_proj(x, g, w, cos2, sin2, n_seq, tm, with_means):
    t = x.shape[0]
    per_seq = t // n_seq // tm
    tile = pl.BlockSpec((tm, D_MODEL), lambda i: (i, 0))
    rope_tile = pl.BlockSpec((tm, HEAD_DIM), lambda i: (i % per_seq, 0))
    head_tile = pl.BlockSpec((None, N_HEADS, tm, HEAD_DIM), lambda i: (i // per_seq, 0, i % per_seq, 0))
    out_specs = [tile, tile, head_tile, head_tile]
    out_shape = [jax.ShapeDtypeStruct((t, D_MODEL), F32)] * 2 + \
                [jax.ShapeDtypeStruct((n_seq, N_HEADS, t // n_seq, HEAD_DIM), BF16)] * 2
    if with_means:
        nb = tm // MOBA_BLOCK
        out_specs.append(pl.BlockSpec((nb, 1, D_MODEL), lambda i: (i, 0, 0)))
        out_shape.append(jax.ShapeDtypeStruct((t // MOBA_BLOCK, 1, D_MODEL), F32))
    return pl.pallas_call(
        functools.partial(_kv_kernel, with_means),
        grid=(t // tm,),
        in_specs=[tile, pl.BlockSpec((1, D_MODEL), lambda i: (0, 0)),
                  pl.BlockSpec((D_MODEL, 2 * D_MODEL), lambda i: (0, 0)), rope_tile, rope_tile],
        out_specs=out_specs,
        out_shape=out_shape,
        compiler_params=_params("parallel"),
        name="kv_proj",
    )(x, g, w, cos2, sin2)


def _q_kernel(blocks_per_seq, x_ref, g_ref, w_ref, cos_ref, sin_ref, kbar_ref, qa_ref, z_ref):
    own = pl.program_id(0) % blocks_per_seq
    h = _rmsnorm(x_ref[...], g_ref[...]).astype(BF16)
    qz = _dot(h, w_ref[...])
    z_ref[...] = qz[:, D_MODEL:]
    cos2, sin2 = cos_ref[...], sin_ref[...]
    lane = lax.broadcasted_iota(jnp.int32, (MOBA_BLOCK, LANES), 1)
    valid = lane < own
    for hd in range(N_HEADS):
        q = _rope(qz[:, hd * HEAD_DIM:(hd + 1) * HEAD_DIM], cos2, sin2)
        s = _dot_nt(q, kbar_ref[hd], precision=lax.Precision.HIGHEST)
        sel, _ = _top3(s, valid)
        qa_ref[hd, :, :HEAD_DIM] = (q * (HEAD_DIM ** -0.5)).astype(BF16)
        qa_ref[hd, :, HEAD_DIM:] = jnp.where(sel > 0.5, 0.0, NEG_INF).astype(BF16)


def _q_proj(x, g, w, cos2, sin2, kbar, n_seq):
    t = x.shape[0]
    bps = t // n_seq // MOBA_BLOCK
    tile = pl.BlockSpec((MOBA_BLOCK, D_MODEL), lambda i: (i, 0))
    rope_tile = pl.BlockSpec((MOBA_BLOCK, HEAD_DIM), lambda i: (i % bps, 0))
    return pl.pallas_call(
        functools.partial(_q_kernel, bps),
        grid=(t // MOBA_BLOCK,),
        in_specs=[tile, pl.BlockSpec((1, D_MODEL), lambda i: (0, 0)),
                  pl.BlockSpec((D_MODEL, 2 * D_MODEL), lambda i: (0, 0)), rope_tile, rope_tile,
                  pl.BlockSpec((None, N_HEADS, LANES, HEAD_DIM), lambda i: (i // bps, 0, 0, 0))],
        out_specs=[pl.BlockSpec((None, N_HEADS, MOBA_BLOCK, 2 * HEAD_DIM),
                                lambda i: (i // bps, 0, i % bps, 0)), tile],
        out_shape=[jax.ShapeDtypeStruct((n_seq, N_HEADS, t // n_seq, 2 * HEAD_DIM), BF16),
                   jax.ShapeDtypeStruct((t, D_MODEL), F32)],
        compiler_params=_params("parallel"),
        name="q_proj",
    )(x, g, w, cos2, sin2, kbar)


def _moba_kernel(qa_ref, k_ref, v_ref, o_ref):
    i = pl.program_id(2)
    qa = qa_ref[...]
    q = qa_ref[:, :HEAD_DIM]
    row = lax.broadcasted_iota(jnp.int32, (MOBA_BLOCK, MOBA_BLOCK), 0)
    col = lax.broadcasted_iota(jnp.int32, (MOBA_BLOCK, MOBA_BLOCK), 1)
    lane = lax.broadcasted_iota(jnp.int32, (MOBA_BLOCK, LANES), 1)
    start = pl.multiple_of(i * MOBA_BLOCK, MOBA_BLOCK)
    s = jnp.where(col <= row, _dot_nt(q, k_ref[pl.ds(start, MOBA_BLOCK), :]), NEG_INF)
    m = jnp.max(s, axis=1, keepdims=True)
    p = jnp.exp(s - m)
    l = jnp.sum(p, axis=1, keepdims=True)
    acc = _dot(p.astype(BF16), v_ref[pl.ds(start, MOBA_BLOCK), :])

    def body(j, carry):
        m, l, acc = carry
        off = pl.multiple_of(j * MOBA_BLOCK, MOBA_BLOCK)
        ka = jnp.concatenate([k_ref[pl.ds(off, MOBA_BLOCK), :],
                              jnp.where(lane == j, 1.0, 0.0).astype(BF16)], axis=1)
        s = _dot_nt(qa, ka)
        m_new = jnp.maximum(m, jnp.max(s, axis=1, keepdims=True))
        alpha = jnp.exp(m - m_new)
        p = jnp.exp(s - m_new)
        l = alpha * l + jnp.sum(p, axis=1, keepdims=True)
        acc = alpha * acc + _dot(p.astype(BF16), v_ref[pl.ds(off, MOBA_BLOCK), :])
        return m_new, l, acc

    m, l, acc = lax.fori_loop(0, i, body, (m, l, acc))
    o_ref[...] = (acc / l).astype(o_ref.dtype)


def _moba_prompt(qa, kb, vb):
    n_seq, _, seq, _ = qa.shape
    kv_spec = pl.BlockSpec((None, None, seq, HEAD_DIM), lambda n, h, i: (n, h, 0, 0))
    return pl.pallas_call(
        _moba_kernel,
        grid=(n_seq, N_HEADS, seq // MOBA_BLOCK),
        in_specs=[pl.BlockSpec((None, None, MOBA_BLOCK, 2 * HEAD_DIM), lambda n, h, i: (n, h, i, 0)),
                  kv_spec, kv_spec],
        out_specs=pl.BlockSpec((None, MOBA_BLOCK, HEAD_DIM), lambda n, h, i: (n, i, h)),
        out_shape=jax.ShapeDtypeStruct((n_seq, seq, D_MODEL), BF16),
        compiler_params=_params("parallel", "parallel", "arbitrary"),
        name="moba_prompt",
    )(qa, kb, vb)


def _b_out_kernel(final, o_ref, z_ref, x_ref, w_ref, g_ref, y_ref):
    v = (o_ref[...].astype(F32) * _silu(z_ref[...])).astype(BF16)
    x = x_ref[...] + _dot(v, w_ref[...])
    y_ref[...] = _rmsnorm(x, g_ref[...]) if final else x


def _b_out(o, z, x, w, g, tm, final):
    t = x.shape[0]
    tile = pl.BlockSpec((tm, D_MODEL), lambda i: (i, 0))
    return pl.pallas_call(
        functools.partial(_b_out_kernel, final),
        grid=(t // tm,),
        in_specs=[tile, tile, tile, pl.BlockSpec((D_MODEL, D_MODEL), lambda i: (0, 0)),
                  pl.BlockSpec((1, D_MODEL), lambda i: (0, 0))],
        out_specs=tile,
        out_shape=jax.ShapeDtypeStruct((t, D_MODEL), F32),
        compiler_params=_params("parallel"),
        name="b_out",
    )(o, z, x, w, g)


PAGES_PER_STEP = 16


def _page_mean_kernel(*refs):
    pages, out_ref = refs[1:1 + PAGES_PER_STEP], refs[-1]
    per_block = MOBA_BLOCK // PAGE_SIZE
    for b in range(PAGES_PER_STEP // per_block):
        tot = jnp.sum(pages[per_block * b][...], axis=0, keepdims=True)
        for r in range(1, per_block):
            tot = tot + jnp.sum(pages[per_block * b + r][...], axis=0, keepdims=True)
        out_ref[b:b + 1, :] = tot * (1.0 / MOBA_BLOCK)


def _page_means(cache_k3, page_table):
    n, n_pages = page_table.shape
    steps = n_pages // PAGES_PER_STEP
    nb = PAGES_PER_STEP * PAGE_SIZE // MOBA_BLOCK

    def page_spec(r):
        return pl.BlockSpec((None, PAGE_SIZE, D_MODEL),
                            lambda s, j, pt, r=r: (pt[s * n_pages + j * PAGES_PER_STEP + r], 0, 0))

    return pl.pallas_call(
        _page_mean_kernel,
        grid_spec=pltpu.PrefetchScalarGridSpec(
            num_scalar_prefetch=1,
            grid=(n, steps),
            in_specs=[page_spec(r) for r in range(PAGES_PER_STEP)],
            out_specs=pl.BlockSpec((None, nb, D_MODEL), lambda s, j, pt: (s, j, 0))),
        out_shape=jax.ShapeDtypeStruct((n, n_pages * PAGE_SIZE // MOBA_BLOCK, D_MODEL), F32),
        compiler_params=_params("parallel", "arbitrary"),
        name="page_means",
    )(page_table.reshape(-1), *([cache_k3] * PAGES_PER_STEP))


def _q_sample_kernel(x_ref, g_ref, w_ref, cos_ref, sin_ref, q_ref, z_ref):
    h = _rmsnorm(x_ref[...], g_ref[...]).astype(BF16)
    qz = _dot(h, w_ref[...])
    z_ref[...] = qz[:, D_MODEL:]
    cos2, sin2 = cos_ref[...], sin_ref[...]
    for hd in range(N_HEADS):
        sl = slice(hd * HEAD_DIM, (hd + 1) * HEAD_DIM)
        q_ref[:, sl] = _rope(qz[:, sl], cos2, sin2)


def _q_sample(x, g, w, cos2, sin2):
    n = x.shape[0]
    return pl.pallas_call(
        _q_sample_kernel,
        out_shape=[jax.ShapeDtypeStruct((n, D_MODEL), F32)] * 2,
        compiler_params=pltpu.CompilerParams(vmem_limit_bytes=VMEM_LIMIT),
        name="q_sample",
    )(x, g, w, cos2, sin2)


def _pick_kernel(q_ref, kbar_ref, knew_ref, idx_ref):
    n_past = kbar_ref.shape[0]
    lane_o = lax.broadcasted_iota(jnp.int32, (8, LANES), 1)
    row_o = lax.broadcasted_iota(jnp.int32, (8, LANES), 0)
    out = jnp.zeros((8, LANES), jnp.int32)
    for hd in range(N_HEADS):
        sl = slice(hd * HEAD_DIM, (hd + 1) * HEAD_DIM)
        q8 = jnp.broadcast_to(q_ref[:, sl], (8, HEAD_DIM))
        own_mean = jnp.broadcast_to(knew_ref[:, sl] * (1.0 / MOBA_BLOCK), (8, HEAD_DIM))
        means = jnp.concatenate([kbar_ref[:, sl], own_mean], axis=0)
        s = _dot_nt(q8, means, precision=lax.Precision.HIGHEST)
        lane = lax.broadcasted_iota(jnp.int32, s.shape, 1)
        _, picks = _top3(s, lane < n_past)
        for r, idx in enumerate(picks):
            out = jnp.where((row_o == hd) & (lane_o == r), idx[0:1, :], out)
    idx_ref[...] = out


def _pick_blocks(q, kbar, knew):
    n, n_past, _ = kbar.shape
    return pl.pallas_call(
        _pick_kernel,
        grid=(n,),
        in_specs=[pl.BlockSpec((None, 1, D_MODEL), lambda s: (s, 0, 0)),
                  pl.BlockSpec((None, n_past, D_MODEL), lambda s: (s, 0, 0)),
                  pl.BlockSpec((None, 1, D_MODEL), lambda s: (s, 0, 0))],
        out_specs=pl.BlockSpec((None, 8, LANES), lambda s: (s, 0, 0)),
        out_shape=jax.ShapeDtypeStruct((n, 8, LANES), jnp.int32),
        compiler_params=_params("parallel"),
        name="pick_blocks",
    )(q.reshape(n, 1, D_MODEL), kbar, knew.reshape(n, 1, D_MODEL))


N_FETCH = MOBA_TOPK * (MOBA_BLOCK // PAGE_SIZE)


def _decode_attn_kernel(*refs):
    q_ref, kn_ref, vn_ref = refs[2:5]
    k_pages = refs[5:5 + N_FETCH]
    v_pages = refs[5 + N_FETCH:5 + 2 * N_FETCH]
    o_ref = refs[-1]
    scale = HEAD_DIM ** -0.5
    q = q_ref[...]
    q8 = jnp.broadcast_to(q, (8, HEAD_DIM))
    kc = jnp.concatenate([r[...] for r in k_pages], axis=0)
    vc = jnp.concatenate([r[...] for r in v_pages], axis=0)
    s = _dot_nt(q8, kc)[0:1, :] * scale
    s_new = jnp.sum(q * kn_ref[...], axis=1, keepdims=True) * scale
    m = jnp.maximum(jnp.max(s, axis=1, keepdims=True), s_new)
    p = jnp.exp(s - m)
    p_new = jnp.exp(s_new - m)
    l = jnp.sum(p, axis=1, keepdims=True) + p_new
    pv = _dot(jnp.broadcast_to(p, (8, p.shape[1])), vc)[0:1, :]
    o_ref[...] = (pv + p_new * vn_ref[...]) / l


def _decode_attn(q, k_new, v_new, cache_k3, cache_v3, page_table, picks):
    n, n_pages = page_table.shape
    per_block = MOBA_BLOCK // PAGE_SIZE

    def page_spec(f):
        def index(s, h, pk, pt):
            blk = pk[(s * N_HEADS + h) * MOBA_TOPK + f // per_block]
            return (pt[s * n_pages + blk * per_block + f % per_block], 0, h)
        return pl.BlockSpec((None, PAGE_SIZE, HEAD_DIM), index)

    row = pl.BlockSpec((None, None, 1, HEAD_DIM), lambda s, h, pk, pt: (s, h, 0, 0))
    return pl.pallas_call(
        _decode_attn_kernel,
        grid_spec=pltpu.PrefetchScalarGridSpec(
            num_scalar_prefetch=2,
            grid=(n, N_HEADS),
            in_specs=[row, row, row] + [page_spec(f) for f in range(N_FETCH)] * 2,
            out_specs=row),
        out_shape=jax.ShapeDtypeStruct((n, N_HEADS, 1, HEAD_DIM), F32),
        compiler_params=_params("parallel", "parallel"),
        name="decode_attn",
    )(picks, page_table.reshape(-1),
      q.reshape(n, N_HEADS, 1, HEAD_DIM), k_new.reshape(n, N_HEADS, 1, HEAD_DIM),
      v_new.reshape(n, N_HEADS, 1, HEAD_DIM),
      *([cache_k3] * N_FETCH), *([cache_v3] * N_FETCH))


def _rope_tables(pos):
    half = HEAD_DIM // 2
    inv = ROPE_THETA ** (-jnp.arange(half, dtype=F32) / half)
    ang = pos.astype(F32)[:, None] * inv[None, :]
    cos, sin = jnp.cos(ang), jnp.sin(ang)
    return jnp.concatenate([cos, cos], axis=1), jnp.concatenate([-sin, sin], axis=1)


def kernel(x_prompt, x_sample, state_ssm_re, state_ssm_im, cache_k, cache_v, page_table, a_norm, a_w_in, a_lam_re, a_lam_im, a_log_dt, a_b_re, a_b_im, a_c_re, a_c_im, a_d, a_w_glu, a_w_out, kv_norm, w_kv, b_norm, b_w_in, b_w_out, final_norm):
    n_seq, seq, _ = x_prompt.shape
    n_dec = x_sample.shape[0]
    n_pool = cache_k.shape[0]
    past_len = page_table.shape[1] * PAGE_SIZE
    assert x_sample.shape[1] == 1 and seq % (2 * MOBA_BLOCK) == 0 and past_len % (PAGES_PER_STEP * PAGE_SIZE) == 0
    assert past_len // MOBA_BLOCK >= MOBA_TOPK and seq // MOBA_BLOCK <= LANES
    n_a, n_b = a_norm.shape[0], b_norm.shape[0]

    xp = x_prompt.reshape(n_seq * seq, D_MODEL)
    xs = x_sample.reshape(n_dec, D_MODEL)
    cos_p, sin_p = _rope_tables(jnp.arange(seq, dtype=jnp.int32))
    cos_s, sin_s = _rope_tables(jnp.full((n_dec,), past_len, jnp.int32))

    st_p_re, st_p_im, st_s_re, st_s_im = [], [], [], []
    for l in range(n_a):
        prep = _ssm_prep(a_lam_re[l], a_lam_im[l], a_log_dt[l], a_b_re[l], a_b_im[l], a_c_re[l], a_c_im[l],
                         seq // SSM_TBLK)
        norm, d = a_norm[l][None], a_d[l][None]
        w_in, wg, wo = a_w_in[l].astype(BF16), a_w_glu[l].astype(BF16), a_w_out[l].astype(BF16)
        xp, hr, hi = _s5_layer_prompt(xp, n_seq, norm, w_in, prep, d, wg, wo)
        st_p_re.append(hr)
        st_p_im.append(hi)
        xs, hr, hi = _s5_layer_sample(xs, state_ssm_re[l], state_ssm_im[l], norm, w_in, prep, d, wg, wo)
        st_s_re.append(hr)
        st_s_im.append(hi)

    w_kv_b = w_kv.astype(BF16)
    k_p, v_p, kb, vb, kbar_p = _kv_proj(xp, kv_norm[None], w_kv_b, cos_p, sin_p, n_seq, 2 * MOBA_BLOCK, True)
    k_s, v_s, _, _ = _kv_proj(xs, kv_norm[None], w_kv_b, cos_s, sin_s, 1, n_dec, False)
    nblk = seq // MOBA_BLOCK
    kbar_p = kbar_p.reshape(n_seq, nblk, N_HEADS, HEAD_DIM).transpose(0, 2, 1, 3)
    kbar_p = jnp.pad(kbar_p, ((0, 0), (0, 0), (0, LANES - nblk), (0, 0)))
    cache_k3 = cache_k.reshape(n_pool, PAGE_SIZE, D_MODEL)
    cache_v3 = cache_v.reshape(n_pool, PAGE_SIZE, D_MODEL)
    kbar_s = _page_means(cache_k3, page_table)

    for j in range(n_b):
        w_in, wo = b_w_in[j].astype(BF16), b_w_out[j].astype(BF16)
        final = j == n_b - 1
        qa, z = _q_proj(xp, b_norm[j][None], w_in, cos_p, sin_p, kbar_p, n_seq)
        o = _moba_prompt(qa, kb, vb).reshape(n_seq * seq, D_MODEL)
        xp = _b_out(o, z, xp, wo, final_norm[None], 512, final)

        q, z = _q_sample(xs, b_norm[j][None], w_in, cos_s, sin_s)
        picks = _pick_blocks(q, kbar_s, k_s)[:, :, :MOBA_TOPK].reshape(-1)
        o = _decode_attn(q, k_s, v_s, cache_k3, cache_v3, page_table, picks).reshape(n_dec, D_MODEL)
        xs = _b_out(o, z, xs, wo, final_norm[None], n_dec, final)

    return (xp.reshape(n_seq, seq, D_MODEL), xs.reshape(n_dec, 1, D_MODEL),
            jnp.stack(st_p_re), jnp.stack(st_p_im), jnp.stack(st_s_re), jnp.stack(st_s_im),
            k_p.reshape(n_seq, seq, N_HEADS, HEAD_DIM), v_p.reshape(n_seq, seq, N_HEADS, HEAD_DIM),
            k_s.reshape(n_dec, 1, N_HEADS, HEAD_DIM), v_s.reshape(n_dec, 1, N_HEADS, HEAD_DIM))
```

```python
import functools
import math

import jax
import jax.numpy as jnp
from jax import lax
from jax.experimental import pallas as pl
from jax.experimental.pallas import tpu as pltpu

D_MODEL = 1024
SSM_GROUP = 16
SSM_GROUPS = D_MODEL // SSM_GROUP
SSM_STATE = 64
SSM_TBLK = 16
SSM_BLKW = SSM_TBLK * SSM_GROUP
HEAD_DIM = 128
N_HEADS = D_MODEL // HEAD_DIM
MOBA_BLOCK = 256
MOBA_TOPK = 3
PAGE_SIZE = 128
ROPE_THETA = 10000.0
RMS_EPS = 1e-6
NEG_INF = -1e30
LANES = 128
VMEM_LIMIT = 48 * 1024 * 1024
ROW_TILE = 512
MOBA_HEADS = 4
V_ROWS = HEAD_DIM + 16

F32 = jnp.float32
BF16 = jnp.bfloat16


def _params(*sem):
    return pltpu.CompilerParams(dimension_semantics=sem, vmem_limit_bytes=VMEM_LIMIT)


def _rmsnorm(x, g):
    r = lax.rsqrt(jnp.mean(x * x, axis=-1, keepdims=True) + RMS_EPS)
    return x * r * g


def _dot(a, b):
    return jnp.dot(a, b, preferred_element_type=F32)


def _dot_nt(a, b, precision=None):
    return lax.dot_general(a, b, (((1,), (1,)), ((), ())), precision=precision,
                           preferred_element_type=F32)


def _rope(x, cos2, sin2):
    return x * cos2 + pltpu.roll(x, HEAD_DIM // 2, 1) * sin2


def _silu(z):
    return z * jax.nn.sigmoid(z)


def _top3(s, valid):
    n_lanes = s.shape[1]
    lane = lax.broadcasted_iota(jnp.int32, s.shape, 1)
    sm = jnp.where(valid, s, NEG_INF)
    sel = jnp.zeros(s.shape, F32)
    picks = []
    for _ in range(MOBA_TOPK):
        m = jnp.max(sm, axis=1, keepdims=True)
        idx = jnp.min(jnp.where(sm == m, lane, n_lanes), axis=1, keepdims=True)
        hit = lane == idx
        sel = jnp.where(hit, jnp.where(valid, 1.0, sel), sel)
        sm = jnp.where(hit, NEG_INF, sm)
        picks.append(idx)
    return sel, picks


GROUPS_PER_TILE = LANES // SSM_GROUP
STEP_TILES = SSM_BLKW // LANES


def _slot_move(pieces, shifts, slot):
    out = None
    for s, (v, sh) in enumerate(zip(pieces, shifts)):
        v = pltpu.roll(v, sh, 1) if sh else v
        out = v if out is None else jnp.where(slot == s, v, out)
    return out


def _to_blocked(u, tiles_ref, ug_ref):
    nb = ug_ref.shape[0]
    slot = lax.broadcasted_iota(jnp.int32, (nb, LANES), 1) // SSM_GROUP
    for tile in range(D_MODEL // LANES):
        tiles_ref[tile] = u[:, tile * LANES:(tile + 1) * LANES]
    for tile in range(D_MODEL // LANES):
        steps = [tiles_ref[tile, pl.ds(r, nb, stride=SSM_TBLK), :] for r in range(SSM_TBLK)]
        for gp in range(GROUPS_PER_TILE):
            for hh in range(STEP_TILES):
                shifts = [((rp - gp) % GROUPS_PER_TILE) * SSM_GROUP for rp in range(GROUPS_PER_TILE)]
                piece = _slot_move(steps[hh * GROUPS_PER_TILE:(hh + 1) * GROUPS_PER_TILE], shifts, slot)
                col = (tile * GROUPS_PER_TILE + gp) * SSM_BLKW + hh * LANES
                ug_ref[:, col:col + LANES] = piece.astype(ug_ref.dtype)


def _from_blocked(yg_ref, tiles_ref):
    nb = yg_ref.shape[0]
    slot = lax.broadcasted_iota(jnp.int32, (nb, LANES), 1) // SSM_GROUP
    for tile in range(D_MODEL // LANES):
        for r in range(SSM_TBLK):
            hh, rm = divmod(r, GROUPS_PER_TILE)
            cols = [(tile * GROUPS_PER_TILE + gp) * SSM_BLKW + hh * LANES for gp in range(GROUPS_PER_TILE)]
            shifts = [((gp - rm) % GROUPS_PER_TILE) * SSM_GROUP for gp in range(GROUPS_PER_TILE)]
            tiles_ref[tile, pl.ds(r, nb, stride=SSM_TBLK), :] = _slot_move(
                [yg_ref[:, c:c + LANES] for c in cols], shifts, slot)
    return jnp.concatenate([tiles_ref[tile] for tile in range(D_MODEL // LANES)], axis=1)


def _a_in_kernel(blocked, x_ref, g_ref, w_ref, u_ref, z_ref, *rest):
    h = _rmsnorm(x_ref[...], g_ref[...]).astype(BF16)
    uz = _dot(h, w_ref[...])
    u_ref[...] = uz[:, :D_MODEL]
    z_ref[...] = uz[:, D_MODEL:]
    if blocked:
        ug_ref, tiles_ref = rest
        _to_blocked(uz[:, :D_MODEL], tiles_ref, ug_ref)


def _a_in(x, g, w, tm, blocked):
    t = x.shape[0]
    tile = pl.BlockSpec((tm, D_MODEL), lambda i: (i, 0))
    out_specs, out_shape = [tile, tile], [jax.ShapeDtypeStruct((t, D_MODEL), F32)] * 2
    if blocked:
        out_specs.append(pl.BlockSpec((tm // SSM_TBLK, SSM_GROUPS * SSM_BLKW), lambda i: (i, 0)))
        out_shape.append(jax.ShapeDtypeStruct((t // SSM_TBLK, SSM_GROUPS * SSM_BLKW), BF16))
    return pl.pallas_call(
        functools.partial(_a_in_kernel, blocked),
        grid=(t // tm,),
        in_specs=[tile, pl.BlockSpec((1, D_MODEL), lambda i: (0, 0)),
                  pl.BlockSpec((D_MODEL, 2 * D_MODEL), lambda i: (0, 0))],
        out_specs=out_specs,
        out_shape=out_shape,
        scratch_shapes=[pltpu.VMEM((D_MODEL // LANES, tm, LANES), F32)] if blocked else [],
        compiler_params=_params("parallel"),
        name="a_in",
    )(x, g, w)


def _ssm_scan_kernel(n_seq, ug_ref, m_ref, f_ref, e_ref, ap_ref, y_ref, st_ref):
    rows = ug_ref.shape[0]
    nblk = rows // n_seq
    ug = ug_ref[...]
    s = _dot(ug, f_ref[...])
    row = lax.broadcasted_iota(jnp.int32, (nblk, LANES), 0)
    hp = []
    for q in range(n_seq):
        re = s[q * nblk:(q + 1) * nblk, :LANES]
        im = s[q * nblk:(q + 1) * nblk, LANES:]
        step, k = 1, 0
        while step < nblk:
            ar = ap_ref[2 * k:2 * k + 1, :]
            ai = ap_ref[2 * k + 1:2 * k + 2, :]
            sr = jnp.where(row >= step, pltpu.roll(re, step, 0), 0.0)
            si = jnp.where(row >= step, pltpu.roll(im, step, 0), 0.0)
            re, im = re + ar * sr - ai * si, im + ar * si + ai * sr
            step, k = step * 2, k + 1
        st_ref[2 * q:2 * q + 1, :] = re[nblk - 1:nblk, :]
        st_ref[2 * q + 1:2 * q + 2, :] = im[nblk - 1:nblk, :]
        pr = jnp.where(row >= 1, pltpu.roll(re, 1, 0), 0.0)
        pi = jnp.where(row >= 1, pltpu.roll(im, 1, 0), 0.0)
        hp.append(jnp.concatenate([pr, pi], axis=1))
    hprev = jnp.concatenate(hp, axis=0).astype(BF16)
    carry = _dot(hprev, e_ref[...])
    y_ref[:, :SSM_BLKW] = _dot(ug[:, :SSM_BLKW], m_ref[0]) + carry[:, :SSM_BLKW]
    y_ref[:, SSM_BLKW:] = _dot(ug[:, SSM_BLKW:], m_ref[1]) + carry[:, SSM_BLKW:]


def _ssm_scan(ug, prep, n_seq):
    rows = ug.shape[0]
    n_pairs = SSM_GROUPS // 2
    n_pow = prep["apow"].shape[1]
    return pl.pallas_call(
        functools.partial(_ssm_scan_kernel, n_seq),
        grid=(n_pairs,),
        in_specs=[pl.BlockSpec((rows, 2 * SSM_BLKW), lambda p: (0, p)),
                  pl.BlockSpec((2, SSM_BLKW, SSM_BLKW), lambda p: (p, 0, 0)),
                  pl.BlockSpec((None, 2 * SSM_BLKW, 2 * LANES), lambda p: (p, 0, 0)),
                  pl.BlockSpec((None, 2 * LANES, 2 * SSM_BLKW), lambda p: (p, 0, 0)),
                  pl.BlockSpec((None, n_pow, LANES), lambda p: (p, 0, 0))],
        out_specs=[pl.BlockSpec((rows, 2 * SSM_BLKW), lambda p: (0, p)),
                   pl.BlockSpec((None, 2 * n_seq, LANES), lambda p: (p, 0, 0))],
        out_shape=[jax.ShapeDtypeStruct((rows, SSM_GROUPS * SSM_BLKW), F32),
                   jax.ShapeDtypeStruct((n_pairs, 2 * n_seq, LANES), F32)],
        compiler_params=_params("parallel"),
        name="ssm_scan",
    )(ug, prep["m"], prep["f"], prep["e"], prep["apow"])


def _ssm_step_kernel(u_ref, hr_ref, hi_ref, bb_ref, ar_ref, ai_ref, cs_ref, y_ref, or_ref, oi_ref):
    bu = lax.dot_general(u_ref[...], bb_ref[...], (((2,), (1,)), ((0,), (0,))),
                         preferred_element_type=F32)
    h0r, h0i = hr_ref[...], hi_ref[...]
    ar, ai = ar_ref[...], ai_ref[...]
    hr = bu[:, :, :SSM_STATE] + ar * h0r - ai * h0i
    hi = bu[:, :, SSM_STATE:] + ar * h0i + ai * h0r
    or_ref[...] = hr
    oi_ref[...] = hi
    y_ref[...] = lax.dot_general(jnp.concatenate([hr, hi], axis=2), cs_ref[...],
                                 (((2,), (1,)), ((0,), (0,))), preferred_element_type=F32)


def _ssm_step(u_g, h0r, h0i, prep):
    g, n, _ = u_g.shape
    return pl.pallas_call(
        _ssm_step_kernel,
        out_shape=[jax.ShapeDtypeStruct((g, n, SSM_GROUP), F32),
                   jax.ShapeDtypeStruct((g, n, SSM_STATE), F32),
                   jax.ShapeDtypeStruct((g, n, SSM_STATE), F32)],
        compiler_params=pltpu.CompilerParams(vmem_limit_bytes=VMEM_LIMIT),
        name="ssm_step",
    )(u_g, h0r, h0i, prep["bb"], prep["a_re"], prep["a_im"], prep["cs"])


def _a_out_kernel(blocked, ys_ref, u_ref, z_ref, x_ref, d_ref, wg_ref, wo_ref, o_ref, *scratch):
    ys = _from_blocked(ys_ref, scratch[0]) if blocked else ys_ref[...]
    y = ys + d_ref[...] * u_ref[...]
    g = jax.nn.gelu(y)
    y2 = g * jax.nn.sigmoid(_dot(g.astype(BF16), wg_ref[...]))
    v = (y2 * _silu(z_ref[...])).astype(BF16)
    o_ref[...] = x_ref[...] + _dot(v, wo_ref[...])


def _a_out(ys, u, z, x, d, wg, wo, tm, blocked):
    t = x.shape[0]
    tile = pl.BlockSpec((tm, D_MODEL), lambda i: (i, 0))
    full = pl.BlockSpec((D_MODEL, D_MODEL), lambda i: (0, 0))
    ys_spec = pl.BlockSpec((tm // SSM_TBLK, SSM_GROUPS * SSM_BLKW), lambda i: (i, 0)) if blocked else tile
    return pl.pallas_call(
        functools.partial(_a_out_kernel, blocked),
        grid=(t // tm,),
        in_specs=[ys_spec, tile, tile, tile, pl.BlockSpec((1, D_MODEL), lambda i: (0, 0)), full, full],
        out_specs=tile,
        out_shape=jax.ShapeDtypeStruct((t, D_MODEL), F32),
        scratch_shapes=[pltpu.VMEM((D_MODEL // LANES, tm, LANES), F32)] if blocked else [],
        compiler_params=_params("parallel"),
        name="a_out",
    )(ys, u, z, x, d, wg, wo)


def _ssm_prep(lam_re, lam_im, log_dt, b_re, b_im, c_re, c_im, nblk):
    hi = lax.Precision.HIGHEST
    g, p = lam_re.shape
    dt = jnp.exp(log_dt)[:, None]

    def power(n):
        n = jnp.asarray(n, F32).reshape((-1, 1, 1))
        mag = jnp.exp(n * (lam_re * dt)[None])
        ang = n * (lam_im * dt)[None]
        return mag * jnp.cos(ang), mag * jnp.sin(ang)

    pw_re, pw_im = power(jnp.arange(SSM_TBLK + 1))
    abar_re, abar_im = pw_re[1], pw_im[1]
    den = lam_re * lam_re + lam_im * lam_im
    f_re = ((abar_re - 1.0) * lam_re + abar_im * lam_im) / den
    f_im = (abar_im * lam_re - (abar_re - 1.0) * lam_im) / den
    bb_re = f_re[..., None] * b_re - f_im[..., None] * b_im
    bb_im = f_re[..., None] * b_im + f_im[..., None] * b_re
    tr, ti = pw_re[:SSM_TBLK, :, :, None], pw_im[:SSM_TBLK, :, :, None]
    w_re = tr * bb_re[None] - ti * bb_im[None]
    w_im = tr * bb_im[None] + ti * bb_re[None]
    f_re_m = w_re[::-1].transpose(1, 0, 3, 2).reshape(g, SSM_BLKW, p)
    f_im_m = w_im[::-1].transpose(1, 0, 3, 2).reshape(g, SSM_BLKW, p)
    kern = (jnp.einsum('tgpi,gop->gtio', w_re, c_re, precision=hi)
            - jnp.einsum('tgpi,gop->gtio', w_im, c_im, precision=hi))
    lag = jnp.arange(SSM_TBLK)[None, :] - jnp.arange(SSM_TBLK)[:, None]
    m = jnp.where((lag >= 0)[None, :, :, None, None], kern[:, jnp.clip(lag, 0, SSM_TBLK - 1)], 0.0)
    m = m.transpose(0, 1, 3, 2, 4).reshape(g, SSM_BLKW, SSM_BLKW)
    er, ei = pw_re[1:, :, None, :], pw_im[1:, :, None, :]
    ca_re = c_re[None] * er - c_im[None] * ei
    ca_im = c_re[None] * ei + c_im[None] * er
    e_re = ca_re.transpose(1, 3, 0, 2).reshape(g, p, SSM_BLKW)
    e_im = -ca_im.transpose(1, 3, 0, 2).reshape(g, p, SSM_BLKW)

    n_pairs = g // 2
    zf = jnp.zeros((n_pairs, SSM_BLKW, p), F32)
    fp = jnp.concatenate([
        jnp.concatenate([f_re_m[0::2], zf, f_im_m[0::2], zf], axis=2),
        jnp.concatenate([zf, f_re_m[1::2], zf, f_im_m[1::2]], axis=2)], axis=1)
    ze = jnp.zeros((n_pairs, p, SSM_BLKW), F32)
    ep = jnp.concatenate([
        jnp.concatenate([e_re[0::2], ze], axis=2),
        jnp.concatenate([ze, e_re[1::2]], axis=2),
        jnp.concatenate([e_im[0::2], ze], axis=2),
        jnp.concatenate([ze, e_im[1::2]], axis=2)], axis=1)
    n_steps = max(1, math.ceil(math.log2(nblk)))
    ap_re, ap_im = power(SSM_TBLK * 2.0 ** jnp.arange(n_steps))
    ap = jnp.stack([ap_re, ap_im], axis=1)
    ap = ap.reshape(n_steps * 2, n_pairs, 2 * p).transpose(1, 0, 2)
    return {
        "m": m.astype(BF16), "f": fp.astype(BF16), "e": ep.astype(BF16), "apow": ap,
        "bb": jnp.concatenate([bb_re.transpose(0, 2, 1), bb_im.transpose(0, 2, 1)], axis=2),
        "a_re": abar_re[:, None, :], "a_im": abar_im[:, None, :],
        "cs": jnp.concatenate([c_re.transpose(0, 2, 1), -c_im.transpose(0, 2, 1)], axis=1),
    }


def _s5_layer_prompt(x, n_seq, norm, w_in, prep, d, wg, wo):
    u, z, ug = _a_in(x, norm, w_in, ROW_TILE, True)
    yg, st = _ssm_scan(ug, prep, n_seq)
    st = st.reshape(SSM_GROUPS // 2, n_seq, 2, 2, SSM_STATE)
    st = st.transpose(2, 1, 0, 3, 4).reshape(2, n_seq, SSM_GROUPS, SSM_STATE)
    return _a_out(yg, u, z, x, d, wg, wo, ROW_TILE, True), st[0], st[1]


def _s5_layer_sample(x, h0r, h0i, norm, w_in, prep, d, wg, wo):
    n = x.shape[0]
    u, z = _a_in(x, norm, w_in, n, False)
    u_g = u.reshape(n, SSM_GROUPS, SSM_GROUP).transpose(1, 0, 2)
    y_g, hr, hi = _ssm_step(u_g, h0r.transpose(1, 0, 2), h0i.transpose(1, 0, 2), prep)
    ys = y_g.transpose(1, 0, 2).reshape(n, D_MODEL)
    return _a_out(ys, u, z, x, d, wg, wo, n, False), hr.transpose(1, 0, 2), hi.transpose(1, 0, 2)


def _kv_kernel(prompt, x_ref, g_ref, w_ref, cos_ref, sin_ref, k_ref, v_ref, *attn_refs):
    h = _rmsnorm(x_ref[...], g_ref[...]).astype(BF16)
    kv = _dot(h, w_ref[...])
    cos2, sin2 = cos_ref[...], sin_ref[...]
    tm = kv.shape[0]
    for hd in range(N_HEADS):
        sl = slice(hd * HEAD_DIM, (hd + 1) * HEAD_DIM)
        k = _rope(kv[:, sl], cos2, sin2)
        v = kv[:, D_MODEL + hd * HEAD_DIM:D_MODEL + (hd + 1) * HEAD_DIM]
        k_ref[:, sl] = k
        v_ref[:, sl] = v
        if prompt:
            kb_ref, vt_ref, mean_ref = attn_refs
            kb_ref[hd] = k.astype(BF16)
            for b in range(tm // MOBA_BLOCK):
                rows = slice(b * MOBA_BLOCK, (b + 1) * MOBA_BLOCK)
                mean_ref[b, :, sl] = jnp.sum(k[rows], axis=0, keepdims=True) * (1.0 / MOBA_BLOCK)
                vt_ref[hd, b, :HEAD_DIM, :] = v[rows].T.astype(BF16)
                vt_ref[hd, b, HEAD_DIM:, :] = jnp.ones((V_ROWS - HEAD_DIM, MOBA_BLOCK), BF16)


def _kv_proj(x, g, w, cos2, sin2, n_seq, tm, prompt):
    t = x.shape[0]
    per_seq = t // n_seq // tm
    tile = pl.BlockSpec((tm, D_MODEL), lambda i: (i, 0))
    rope_tile = pl.BlockSpec((tm, HEAD_DIM), lambda i: (i % per_seq, 0))
    out_specs = [tile, tile]
    out_shape = [jax.ShapeDtypeStruct((t, D_MODEL), F32)] * 2
    if prompt:
        nb = tm // MOBA_BLOCK
        out_specs += [
            pl.BlockSpec((None, N_HEADS, tm, HEAD_DIM), lambda i: (i // per_seq, 0, i % per_seq, 0)),
            pl.BlockSpec((None, N_HEADS, nb, V_ROWS, MOBA_BLOCK), lambda i: (i // per_seq, 0, i % per_seq, 0, 0)),
            pl.BlockSpec((nb, 1, D_MODEL), lambda i: (i, 0, 0))]
        out_shape += [
            jax.ShapeDtypeStruct((n_seq, N_HEADS, t // n_seq, HEAD_DIM), BF16),
            jax.ShapeDtypeStruct((n_seq, N_HEADS, t // n_seq // MOBA_BLOCK, V_ROWS, MOBA_BLOCK), BF16),
            jax.ShapeDtypeStruct((t // MOBA_BLOCK, 1, D_MODEL), F32)]
    return pl.pallas_call(
        functools.partial(_kv_kernel, prompt),
        grid=(t // tm,),
        in_specs=[tile, pl.BlockSpec((1, D_MODEL), lambda i: (0, 0)),
                  pl.BlockSpec((D_MODEL, 2 * D_MODEL), lambda i: (0, 0)), rope_tile, rope_tile],
        out_specs=out_specs,
        out_shape=out_shape,
        compiler_params=_params("parallel"),
        name="kv_proj",
    )(x, g, w, cos2, sin2)


def _q_kernel(blocks_per_seq, x_ref, g_ref, w_ref, cos_ref, sin_ref, kbar_ref, q_ref, bias_ref, z_ref):
    own = pl.program_id(0) % blocks_per_seq
    h = _rmsnorm(x_ref[...], g_ref[...]).astype(BF16)
    qz = _dot(h, w_ref[...])
    z_ref[...] = qz[:, D_MODEL:]
    cos2, sin2 = cos_ref[...], sin_ref[...]
    lane = lax.broadcasted_iota(jnp.int32, (MOBA_BLOCK, LANES), 1)
    valid = lane < own
    for hd in range(N_HEADS):
        q = _rope(qz[:, hd * HEAD_DIM:(hd + 1) * HEAD_DIM], cos2, sin2)
        s = _dot_nt(q, kbar_ref[hd], precision=lax.Precision.HIGHEST)
        sel, _ = _top3(s, valid)
        q_ref[hd] = (q * (HEAD_DIM ** -0.5)).astype(BF16)
        bias = jnp.where(sel > 0.5, 0.0, NEG_INF).T
        bias_ref[hd] = bias[:blocks_per_seq, :]


def _q_proj(x, g, w, cos2, sin2, kbar, n_seq):
    t = x.shape[0]
    bps = t // n_seq // MOBA_BLOCK
    tile = pl.BlockSpec((MOBA_BLOCK, D_MODEL), lambda i: (i, 0))
    rope_tile = pl.BlockSpec((MOBA_BLOCK, HEAD_DIM), lambda i: (i % bps, 0))
    return pl.pallas_call(
        functools.partial(_q_kernel, bps),
        grid=(t // MOBA_BLOCK,),
        in_specs=[tile, pl.BlockSpec((1, D_MODEL), lambda i: (0, 0)),
                  pl.BlockSpec((D_MODEL, 2 * D_MODEL), lambda i: (0, 0)), rope_tile, rope_tile,
                  pl.BlockSpec((None, N_HEADS, LANES, HEAD_DIM), lambda i: (i // bps, 0, 0, 0))],
        out_specs=[pl.BlockSpec((None, N_HEADS, MOBA_BLOCK, HEAD_DIM), lambda i: (i // bps, 0, i % bps, 0)),
                   pl.BlockSpec((None, N_HEADS, None, bps, MOBA_BLOCK), lambda i: (i // bps, 0, i % bps, 0, 0)),
                   tile],
        out_shape=[jax.ShapeDtypeStruct((n_seq, N_HEADS, t // n_seq, HEAD_DIM), BF16),
                   jax.ShapeDtypeStruct((n_seq, N_HEADS, bps, bps, MOBA_BLOCK), F32),
                   jax.ShapeDtypeStruct((t, D_MODEL), F32)],
        compiler_params=_params("parallel"),
        name="q_proj",
    )(x, g, w, cos2, sin2, kbar)


def _moba_kernel(q_ref, bias_ref, k_ref, vt_ref, o_ref, s_scr, acc_scr):
    i = pl.program_id(2)
    row = lax.broadcasted_iota(jnp.int32, (MOBA_BLOCK, LANES), 0)
    col = lax.broadcasted_iota(jnp.int32, (MOBA_BLOCK, LANES), 1)
    chains = [(h, c) for h in range(MOBA_HEADS) for c in range(MOBA_BLOCK // LANES)]
    nc = len(chains)

    def scores(h, c, blk):
        off = pl.multiple_of(blk * MOBA_BLOCK, MOBA_BLOCK)
        return _dot_nt(k_ref[h, pl.ds(off, MOBA_BLOCK), :], q_ref[h, c * LANES:(c + 1) * LANES, :])

    ms = []
    for n, (h, c) in enumerate(chains):
        s = jnp.where(row <= col + c * LANES, scores(h, c, i), NEG_INF)
        m = jnp.max(s, axis=0, keepdims=True)
        acc_scr[n] = _dot(vt_ref[h, i], jnp.exp(s - m).astype(BF16))
        ms.append(m)

    def body(t, ms):
        blks = (2 * t, 2 * t + 1)
        mbs = []
        for n, (h, c) in enumerate(chains):
            for u, blk in enumerate(blks):
                s = scores(h, c, blk)
                s_scr[u * nc + n] = s
                mbs.append(jnp.max(s, axis=0, keepdims=True))
        out = []
        for n, (h, c) in enumerate(chains):
            b = [bias_ref[h, pl.ds(blk, 1), :][:, c * LANES:(c + 1) * LANES] for blk in blks]
            m_new = jnp.maximum(ms[n], jnp.maximum(mbs[2 * n] + b[0], mbs[2 * n + 1] + b[1]))
            alpha = jnp.exp(ms[n] - m_new)
            p = [jnp.exp(s_scr[u * nc + n] - (m_new - b[u])).astype(BF16) for u in range(2)]
            acc_scr[n] = alpha * acc_scr[n] + (_dot(vt_ref[h, blks[0]], p[0]) + _dot(vt_ref[h, blks[1]], p[1]))
            out.append(m_new)
        return tuple(out)

    lax.fori_loop(0, (i + 1) // 2, body, tuple(ms))
    for n, (h, c) in enumerate(chains):
        acc = acc_scr[n]
        o = acc[:HEAD_DIM, :] / acc[HEAD_DIM:HEAD_DIM + 1, :]
        o_ref[c * LANES:(c + 1) * LANES, h * HEAD_DIM:(h + 1) * HEAD_DIM] = o.T.astype(o_ref.dtype)


def _moba_prompt(q, bias, kb, vt):
    n_seq, _, seq, _ = q.shape
    nblk = seq // MOBA_BLOCK
    hg = MOBA_HEADS
    n_chains = hg * (MOBA_BLOCK // LANES)
    return pl.pallas_call(
        _moba_kernel,
        grid=(n_seq, N_HEADS // hg, nblk),
        in_specs=[pl.BlockSpec((None, hg, MOBA_BLOCK, HEAD_DIM), lambda n, h, i: (n, h, i, 0)),
                  pl.BlockSpec((None, hg, None, nblk, MOBA_BLOCK), lambda n, h, i: (n, h, i, 0, 0)),
                  pl.BlockSpec((None, hg, seq, HEAD_DIM), lambda n, h, i: (n, h, 0, 0)),
                  pl.BlockSpec((None, hg, nblk, V_ROWS, MOBA_BLOCK), lambda n, h, i: (n, h, 0, 0, 0))],
        out_specs=pl.BlockSpec((None, MOBA_BLOCK, hg * HEAD_DIM), lambda n, h, i: (n, i, h)),
        out_shape=jax.ShapeDtypeStruct((n_seq, seq, D_MODEL), BF16),
        scratch_shapes=[pltpu.VMEM((2 * n_chains, MOBA_BLOCK, LANES), F32),
                        pltpu.VMEM((n_chains, V_ROWS, LANES), F32)],
        compiler_params=_params("parallel", "parallel", "arbitrary"),
        name="moba_prompt",
    )(q, bias, kb, vt)


def _b_out_kernel(final, o_ref, z_ref, x_ref, w_ref, g_ref, y_ref):
    v = (o_ref[...].astype(F32) * _silu(z_ref[...])).astype(BF16)
    x = x_ref[...] + _dot(v, w_ref[...])
    y_ref[...] = _rmsnorm(x, g_ref[...]) if final else x


def _b_out(o, z, x, w, g, tm, final):
    t = x.shape[0]
    tile = pl.BlockSpec((tm, D_MODEL), lambda i: (i, 0))
    return pl.pallas_call(
        functools.partial(_b_out_kernel, final),
        grid=(t // tm,),
        in_specs=[tile, tile, tile, pl.BlockSpec((D_MODEL, D_MODEL), lambda i: (0, 0)),
                  pl.BlockSpec((1, D_MODEL), lambda i: (0, 0))],
        out_specs=tile,
        out_shape=jax.ShapeDtypeStruct((t, D_MODEL), F32),
        compiler_params=_params("parallel"),
        name="b_out",
    )(o, z, x, w, g)


PAGES_PER_STEP = 16


def _page_mean_kernel(*refs):
    pages, out_ref = refs[1:1 + PAGES_PER_STEP], refs[-1]
    per_block = MOBA_BLOCK // PAGE_SIZE
    for b in range(PAGES_PER_STEP // per_block):
        tot = jnp.sum(pages[per_block * b][...], axis=0, keepdims=True)
        for r in range(1, per_block):
            tot = tot + jnp.sum(pages[per_block * b + r][...], axis=0, keepdims=True)
        out_ref[b:b + 1, :] = tot * (1.0 / MOBA_BLOCK)


def _page_means(cache_k3, page_table):
    n, n_pages = page_table.shape
    steps = n_pages // PAGES_PER_STEP
    nb = PAGES_PER_STEP * PAGE_SIZE // MOBA_BLOCK

    def page_spec(r):
        return pl.BlockSpec((None, PAGE_SIZE, D_MODEL),
                            lambda s, j, pt, r=r: (pt[s * n_pages + j * PAGES_PER_STEP + r], 0, 0))

    return pl.pallas_call(
        _page_mean_kernel,
        grid_spec=pltpu.PrefetchScalarGridSpec(
            num_scalar_prefetch=1,
            grid=(n, steps),
            in_specs=[page_spec(r) for r in range(PAGES_PER_STEP)],
            out_specs=pl.BlockSpec((None, nb, D_MODEL), lambda s, j, pt: (s, j, 0))),
        out_shape=jax.ShapeDtypeStruct((n, n_pages * PAGE_SIZE // MOBA_BLOCK, D_MODEL), F32),
        compiler_params=_params("parallel", "arbitrary"),
        name="page_means",
    )(page_table.reshape(-1), *([cache_k3] * PAGES_PER_STEP))


def _q_sample_kernel(x_ref, g_ref, w_ref, cos_ref, sin_ref, q_ref, z_ref):
    h = _rmsnorm(x_ref[...], g_ref[...]).astype(BF16)
    qz = _dot(h, w_ref[...])
    z_ref[...] = qz[:, D_MODEL:]
    cos2, sin2 = cos_ref[...], sin_ref[...]
    for hd in range(N_HEADS):
        sl = slice(hd * HEAD_DIM, (hd + 1) * HEAD_DIM)
        q_ref[:, sl] = _rope(qz[:, sl], cos2, sin2)


def _q_sample(x, g, w, cos2, sin2):
    n = x.shape[0]
    return pl.pallas_call(
        _q_sample_kernel,
        out_shape=[jax.ShapeDtypeStruct((n, D_MODEL), F32)] * 2,
        compiler_params=pltpu.CompilerParams(vmem_limit_bytes=VMEM_LIMIT),
        name="q_sample",
    )(x, g, w, cos2, sin2)


def _pick_kernel(q_ref, kbar_ref, knew_ref, idx_ref):
    n_past = kbar_ref.shape[0]
    lane_o = lax.broadcasted_iota(jnp.int32, (8, LANES), 1)
    row_o = lax.broadcasted_iota(jnp.int32, (8, LANES), 0)
    out = jnp.zeros((8, LANES), jnp.int32)
    for hd in range(N_HEADS):
        sl = slice(hd * HEAD_DIM, (hd + 1) * HEAD_DIM)
        q8 = jnp.broadcast_to(q_ref[:, sl], (8, HEAD_DIM))
        own_mean = jnp.broadcast_to(knew_ref[:, sl] * (1.0 / MOBA_BLOCK), (8, HEAD_DIM))
        means = jnp.concatenate([kbar_ref[:, sl], own_mean], axis=0)
        s = _dot_nt(q8, means, precision=lax.Precision.HIGHEST)
        lane = lax.broadcasted_iota(jnp.int32, s.shape, 1)
        _, picks = _top3(s, lane < n_past)
        for r, idx in enumerate(picks):
            out = jnp.where((row_o == hd) & (lane_o == r), idx[0:1, :], out)
    idx_ref[...] = out


def _pick_blocks(q, kbar, knew):
    n, n_past, _ = kbar.shape
    return pl.pallas_call(
        _pick_kernel,
        grid=(n,),
        in_specs=[pl.BlockSpec((None, 1, D_MODEL), lambda s: (s, 0, 0)),
                  pl.BlockSpec((None, n_past, D_MODEL), lambda s: (s, 0, 0)),
                  pl.BlockSpec((None, 1, D_MODEL), lambda s: (s, 0, 0))],
        out_specs=pl.BlockSpec((None, 8, LANES), lambda s: (s, 0, 0)),
        out_shape=jax.ShapeDtypeStruct((n, 8, LANES), jnp.int32),
        compiler_params=_params("parallel"),
        name="pick_blocks",
    )(q.reshape(n, 1, D_MODEL), kbar, knew.reshape(n, 1, D_MODEL))


N_FETCH = MOBA_TOPK * (MOBA_BLOCK // PAGE_SIZE)


def _decode_attn_kernel(*refs):
    q_ref, kn_ref, vn_ref = refs[2:5]
    k_pages = refs[5:5 + N_FETCH]
    v_pages = refs[5 + N_FETCH:5 + 2 * N_FETCH]
    o_ref = refs[-1]
    scale = HEAD_DIM ** -0.5
    q = q_ref[...]
    q8 = jnp.broadcast_to(q, (8, HEAD_DIM))
    kc = jnp.concatenate([r[...] for r in k_pages], axis=0)
    vc = jnp.concatenate([r[...] for r in v_pages], axis=0)
    s = _dot_nt(q8, kc)[0:1, :] * scale
    s_new = jnp.sum(q * kn_ref[...], axis=1, keepdims=True) * scale
    m = jnp.maximum(jnp.max(s, axis=1, keepdims=True), s_new)
    p = jnp.exp(s - m)
    p_new = jnp.exp(s_new - m)
    l = jnp.sum(p, axis=1, keepdims=True) + p_new
    pv = _dot(jnp.broadcast_to(p, (8, p.shape[1])), vc)[0:1, :]
    o_ref[...] = (pv + p_new * vn_ref[...]) / l


def _decode_attn(q, k_new, v_new, cache_k3, cache_v3, page_table, picks):
    n, n_pages = page_table.shape
    per_block = MOBA_BLOCK // PAGE_SIZE

    def page_spec(f):
        def index(s, h, pk, pt):
            blk = pk[(s * N_HEADS + h) * MOBA_TOPK + f // per_block]
            return (pt[s * n_pages + blk * per_block + f % per_block], 0, h)
        return pl.BlockSpec((None, PAGE_SIZE, HEAD_DIM), index)

    row = pl.BlockSpec((None, None, 1, HEAD_DIM), lambda s, h, pk, pt: (s, h, 0, 0))
    return pl.pallas_call(
        _decode_attn_kernel,
        grid_spec=pltpu.PrefetchScalarGridSpec(
            num_scalar_prefetch=2,
            grid=(n, N_HEADS),
            in_specs=[row, row, row] + [page_spec(f) for f in range(N_FETCH)] * 2,
            out_specs=row),
        out_shape=jax.ShapeDtypeStruct((n, N_HEADS, 1, HEAD_DIM), F32),
        compiler_params=_params("parallel", "parallel"),
        name="decode_attn",
    )(picks, page_table.reshape(-1),
      q.reshape(n, N_HEADS, 1, HEAD_DIM), k_new.reshape(n, N_HEADS, 1, HEAD_DIM),
      v_new.reshape(n, N_HEADS, 1, HEAD_DIM),
      *([cache_k3] * N_FETCH), *([cache_v3] * N_FETCH))


def _rope_tables(pos):
    half = HEAD_DIM // 2
    inv = ROPE_THETA ** (-jnp.arange(half, dtype=F32) / half)
    ang = pos.astype(F32)[:, None] * inv[None, :]
    cos, sin = jnp.cos(ang), jnp.sin(ang)
    return jnp.concatenate([cos, cos], axis=1), jnp.concatenate([-sin, sin], axis=1)


def kernel(x_prompt, x_sample, state_ssm_re, state_ssm_im, cache_k, cache_v, page_table, a_norm, a_w_in, a_lam_re, a_lam_im, a_log_dt, a_b_re, a_b_im, a_c_re, a_c_im, a_d, a_w_glu, a_w_out, kv_norm, w_kv, b_norm, b_w_in, b_w_out, final_norm):
    n_seq, seq, _ = x_prompt.shape
    n_dec = x_sample.shape[0]
    n_pool = cache_k.shape[0]
    past_len = page_table.shape[1] * PAGE_SIZE
    assert x_sample.shape[1] == 1 and seq % ROW_TILE == 0 and ROW_TILE % MOBA_BLOCK == 0
    assert past_len % (PAGES_PER_STEP * PAGE_SIZE) == 0 and past_len // MOBA_BLOCK >= MOBA_TOPK
    assert seq // MOBA_BLOCK <= LANES and (seq // MOBA_BLOCK) % 8 == 0
    n_a, n_b = a_norm.shape[0], b_norm.shape[0]

    xp = x_prompt.reshape(n_seq * seq, D_MODEL)
    xs = x_sample.reshape(n_dec, D_MODEL)
    cos_p, sin_p = _rope_tables(jnp.arange(seq, dtype=jnp.int32))
    cos_s, sin_s = _rope_tables(jnp.full((n_dec,), past_len, jnp.int32))

    st_p_re, st_p_im, st_s_re, st_s_im = [], [], [], []
    for l in range(n_a):
        prep = _ssm_prep(a_lam_re[l], a_lam_im[l], a_log_dt[l], a_b_re[l], a_b_im[l], a_c_re[l], a_c_im[l],
                         seq // SSM_TBLK)
        norm, d = a_norm[l][None], a_d[l][None]
        w_in, wg, wo = a_w_in[l].astype(BF16), a_w_glu[l].astype(BF16), a_w_out[l].astype(BF16)
        xp, hr, hi = _s5_layer_prompt(xp, n_seq, norm, w_in, prep, d, wg, wo)
        st_p_re.append(hr)
        st_p_im.append(hi)
        xs, hr, hi = _s5_layer_sample(xs, state_ssm_re[l], state_ssm_im[l], norm, w_in, prep, d, wg, wo)
        st_s_re.append(hr)
        st_s_im.append(hi)

    w_kv_b = w_kv.astype(BF16)
    k_p, v_p, kb, vt, kbar_p = _kv_proj(xp, kv_norm[None], w_kv_b, cos_p, sin_p, n_seq, ROW_TILE, True)
    k_s, v_s = _kv_proj(xs, kv_norm[None], w_kv_b, cos_s, sin_s, 1, n_dec, False)
    nblk = seq // MOBA_BLOCK
    kbar_p = kbar_p.reshape(n_seq, nblk, N_HEADS, HEAD_DIM).transpose(0, 2, 1, 3)
    kbar_p = jnp.pad(kbar_p, ((0, 0), (0, 0), (0, LANES - nblk), (0, 0)))
    cache_k3 = cache_k.reshape(n_pool, PAGE_SIZE, D_MODEL)
    cache_v3 = cache_v.reshape(n_pool, PAGE_SIZE, D_MODEL)
    kbar_s = _page_means(cache_k3, page_table)

    for j in range(n_b):
        w_in, wo = b_w_in[j].astype(BF16), b_w_out[j].astype(BF16)
        final = j == n_b - 1
        q, bias, z = _q_proj(xp, b_norm[j][None], w_in, cos_p, sin_p, kbar_p, n_seq)
        o = _moba_prompt(q, bias, kb, vt).reshape(n_seq * seq, D_MODEL)
        xp = _b_out(o, z, xp, wo, final_norm[None], ROW_TILE, final)

        q, z = _q_sample(xs, b_norm[j][None], w_in, cos_s, sin_s)
        picks = _pick_blocks(q, kbar_s, k_s)[:, :, :MOBA_TOPK].reshape(-1)
        o = _decode_attn(q, k_s, v_s, cache_k3, cache_v3, page_table, picks).reshape(n_dec, D_MODEL)
        xs = _b_out(o, z, xs, wo, final_norm[None], n_dec, final)

    return (xp.reshape(n_seq, seq, D_MODEL), xs.reshape(n_dec, 1, D_MODEL),
            jnp.stack(st_p_re), jnp.stack(st_p_im), jnp.stack(st_s_re), jnp.stack(st_s_im),
            k_p.reshape(n_seq, seq, N_HEADS, HEAD_DIM), v_p.reshape(n_seq, seq, N_HEADS, HEAD_DIM),
            k_s.reshape(n_dec, 1, N_HEADS, HEAD_DIM), v_s.reshape(n_dec, 1, N_HEADS, HEAD_DIM))
```

```python
import functools
import math

import jax
import jax.numpy as jnp
from jax import lax
from jax.experimental import pallas as pl
from jax.experimental.pallas import tpu as pltpu

D_MODEL = 1024
SSM_GROUP = 16
SSM_GROUPS = D_MODEL // SSM_GROUP
SSM_STATE = 64
SSM_TBLK = 16
SSM_BLKW = SSM_TBLK * SSM_GROUP
HEAD_DIM = 128
N_HEADS = D_MODEL // HEAD_DIM
MOBA_BLOCK = 256
MOBA_TOPK = 3
PAGE_SIZE = 128
ROPE_THETA = 10000.0
RMS_EPS = 1e-6
NEG_INF = -1e30
LANES = 128
VMEM_LIMIT = 48 * 1024 * 1024
ROW_TILE = 512
MOBA_HEADS = 4
V_ROWS = HEAD_DIM + 16

F32 = jnp.float32
BF16 = jnp.bfloat16


def _params(*sem):
    return pltpu.CompilerParams(dimension_semantics=sem, vmem_limit_bytes=VMEM_LIMIT)


def _rmsnorm(x, g):
    r = lax.rsqrt(jnp.mean(x * x, axis=-1, keepdims=True) + RMS_EPS)
    return x * r * g


def _dot(a, b):
    return jnp.dot(a, b, preferred_element_type=F32)


def _dot_nt(a, b, precision=None):
    return lax.dot_general(a, b, (((1,), (1,)), ((), ())), precision=precision,
                           preferred_element_type=F32)


def _rope(x, cos2, sin2):
    return x * cos2 + pltpu.roll(x, HEAD_DIM // 2, 1) * sin2


def _silu(z):
    return z * jax.nn.sigmoid(z)


def _top3(s, valid, axis):
    n_lanes = s.shape[axis]
    lane = lax.broadcasted_iota(jnp.int32, s.shape, axis)
    sm = jnp.where(valid, s, NEG_INF)
    sel = jnp.zeros(s.shape, F32)
    picks = []
    for _ in range(MOBA_TOPK):
        m = jnp.max(sm, axis=axis, keepdims=True)
        idx = jnp.min(jnp.where(sm == m, lane, n_lanes), axis=axis, keepdims=True)
        hit = lane == idx
        sel = jnp.where(hit, jnp.where(valid, 1.0, sel), sel)
        sm = jnp.where(hit, NEG_INF, sm)
        picks.append(idx)
    return sel, picks


GROUPS_PER_TILE = LANES // SSM_GROUP
STEP_TILES = SSM_BLKW // LANES


def _slot_transpose(xs, slot):
    xs = list(xs)
    step = GROUPS_PER_TILE // 2
    while step:
        low = (slot & step) == 0
        for i in range(GROUPS_PER_TILE):
            if i & step == 0:
                a, b = xs[i], xs[i + step]
                xs[i] = jnp.where(low, a, pltpu.roll(b, step * SSM_GROUP, 1))
                xs[i + step] = jnp.where(low, pltpu.roll(a, LANES - step * SSM_GROUP, 1), b)
        step //= 2
    return xs


def _to_blocked(u, tiles_ref, ug_ref):
    nb = ug_ref.shape[0]
    slot = lax.broadcasted_iota(jnp.int32, (nb, LANES), 1) // SSM_GROUP
    for tile in range(D_MODEL // LANES):
        tiles_ref[tile] = u[:, tile * LANES:(tile + 1) * LANES]
    for tile in range(D_MODEL // LANES):
        for hh in range(STEP_TILES):
            steps = [tiles_ref[tile, pl.ds(hh * GROUPS_PER_TILE + rp, nb, stride=SSM_TBLK), :]
                     for rp in range(GROUPS_PER_TILE)]
            for gp, piece in enumerate(_slot_transpose(steps, slot)):
                col = (tile * GROUPS_PER_TILE + gp) * SSM_BLKW + hh * LANES
                ug_ref[:, col:col + LANES] = piece.astype(ug_ref.dtype)


def _from_blocked(yg_ref, tiles_ref):
    nb = yg_ref.shape[0]
    slot = lax.broadcasted_iota(jnp.int32, (nb, LANES), 1) // SSM_GROUP
    for tile in range(D_MODEL // LANES):
        for hh in range(STEP_TILES):
            cols = [(tile * GROUPS_PER_TILE + gp) * SSM_BLKW + hh * LANES for gp in range(GROUPS_PER_TILE)]
            for rp, piece in enumerate(_slot_transpose([yg_ref[:, c:c + LANES] for c in cols], slot)):
                tiles_ref[tile, pl.ds(hh * GROUPS_PER_TILE + rp, nb, stride=SSM_TBLK), :] = piece
    return jnp.concatenate([tiles_ref[tile] for tile in range(D_MODEL // LANES)], axis=1)


def _a_in_kernel(blocked, x_ref, g_ref, w_ref, u_ref, z_ref, *rest):
    h = _rmsnorm(x_ref[...], g_ref[...]).astype(BF16)
    uz = _dot(h, w_ref[...])
    u_ref[...] = uz[:, :D_MODEL]
    z_ref[...] = uz[:, D_MODEL:]
    if blocked:
        ug_ref, tiles_ref = rest
        _to_blocked(uz[:, :D_MODEL], tiles_ref, ug_ref)


def _a_in(x, g, w, tm, blocked):
    t = x.shape[0]
    tile = pl.BlockSpec((tm, D_MODEL), lambda i: (i, 0))
    out_specs, out_shape = [tile, tile], [jax.ShapeDtypeStruct((t, D_MODEL), F32)] * 2
    if blocked:
        out_specs.append(pl.BlockSpec((tm // SSM_TBLK, SSM_GROUPS * SSM_BLKW), lambda i: (i, 0)))
        out_shape.append(jax.ShapeDtypeStruct((t // SSM_TBLK, SSM_GROUPS * SSM_BLKW), BF16))
    return pl.pallas_call(
        functools.partial(_a_in_kernel, blocked),
        grid=(t // tm,),
        in_specs=[tile, pl.BlockSpec((1, D_MODEL), lambda i: (0, 0)),
                  pl.BlockSpec((D_MODEL, 2 * D_MODEL), lambda i: (0, 0))],
        out_specs=out_specs,
        out_shape=out_shape,
        scratch_shapes=[pltpu.VMEM((D_MODEL // LANES, tm, LANES), F32)] if blocked else [],
        compiler_params=_params("parallel"),
        name="a_in",
    )(x, g, w)


def _ssm_scan_kernel(n_seq, ug_ref, m_ref, f_ref, e_ref, ap_ref, y_ref, st_ref):
    rows = ug_ref.shape[0]
    nblk = rows // n_seq
    ug = ug_ref[...]
    s = _dot(ug, f_ref[...])
    row = lax.broadcasted_iota(jnp.int32, (nblk, LANES), 0)
    hp = []
    for q in range(n_seq):
        re = s[q * nblk:(q + 1) * nblk, :LANES]
        im = s[q * nblk:(q + 1) * nblk, LANES:]
        step, k = 1, 0
        while step < nblk:
            ar = ap_ref[2 * k:2 * k + 1, :]
            ai = ap_ref[2 * k + 1:2 * k + 2, :]
            sr = jnp.where(row >= step, pltpu.roll(re, step, 0), 0.0)
            si = jnp.where(row >= step, pltpu.roll(im, step, 0), 0.0)
            re, im = re + ar * sr - ai * si, im + ar * si + ai * sr
            step, k = step * 2, k + 1
        st_ref[2 * q:2 * q + 1, :] = re[nblk - 1:nblk, :]
        st_ref[2 * q + 1:2 * q + 2, :] = im[nblk - 1:nblk, :]
        pr = jnp.where(row >= 1, pltpu.roll(re, 1, 0), 0.0)
        pi = jnp.where(row >= 1, pltpu.roll(im, 1, 0), 0.0)
        hp.append(jnp.concatenate([pr, pi], axis=1))
    hprev = jnp.concatenate(hp, axis=0).astype(BF16)
    carry = _dot(hprev, e_ref[...])
    y_ref[:, :SSM_BLKW] = _dot(ug[:, :SSM_BLKW], m_ref[0]) + carry[:, :SSM_BLKW]
    y_ref[:, SSM_BLKW:] = _dot(ug[:, SSM_BLKW:], m_ref[1]) + carry[:, SSM_BLKW:]


def _ssm_scan(ug, prep, n_seq):
    rows = ug.shape[0]
    n_pairs = SSM_GROUPS // 2
    n_pow = prep["apow"].shape[1]
    return pl.pallas_call(
        functools.partial(_ssm_scan_kernel, n_seq),
        grid=(n_pairs,),
        in_specs=[pl.BlockSpec((rows, 2 * SSM_BLKW), lambda p: (0, p)),
                  pl.BlockSpec((2, SSM_BLKW, SSM_BLKW), lambda p: (p, 0, 0)),
                  pl.BlockSpec((None, 2 * SSM_BLKW, 2 * LANES), lambda p: (p, 0, 0)),
                  pl.BlockSpec((None, 2 * LANES, 2 * SSM_BLKW), lambda p: (p, 0, 0)),
                  pl.BlockSpec((None, n_pow, LANES), lambda p: (p, 0, 0))],
        out_specs=[pl.BlockSpec((rows, 2 * SSM_BLKW), lambda p: (0, p)),
                   pl.BlockSpec((None, 2 * n_seq, LANES), lambda p: (p, 0, 0))],
        out_shape=[jax.ShapeDtypeStruct((rows, SSM_GROUPS * SSM_BLKW), F32),
                   jax.ShapeDtypeStruct((n_pairs, 2 * n_seq, LANES), F32)],
        compiler_params=_params("parallel"),
        name="ssm_scan",
    )(ug, prep["m"], prep["f"], prep["e"], prep["apow"])


def _ssm_step_kernel(u_ref, hr_ref, hi_ref, bb_ref, ar_ref, ai_ref, cs_ref, y_ref, or_ref, oi_ref):
    bu = lax.dot_general(u_ref[...], bb_ref[...], (((2,), (1,)), ((0,), (0,))),
                         preferred_element_type=F32)
    h0r, h0i = hr_ref[...], hi_ref[...]
    ar, ai = ar_ref[...], ai_ref[...]
    hr = bu[:, :, :SSM_STATE] + ar * h0r - ai * h0i
    hi = bu[:, :, SSM_STATE:] + ar * h0i + ai * h0r
    or_ref[...] = hr
    oi_ref[...] = hi
    y_ref[...] = lax.dot_general(jnp.concatenate([hr, hi], axis=2), cs_ref[...],
                                 (((2,), (1,)), ((0,), (0,))), preferred_element_type=F32)


def _ssm_step(u_g, h0r, h0i, prep):
    g, n, _ = u_g.shape
    return pl.pallas_call(
        _ssm_step_kernel,
        out_shape=[jax.ShapeDtypeStruct((g, n, SSM_GROUP), F32),
                   jax.ShapeDtypeStruct((g, n, SSM_STATE), F32),
                   jax.ShapeDtypeStruct((g, n, SSM_STATE), F32)],
        compiler_params=pltpu.CompilerParams(vmem_limit_bytes=VMEM_LIMIT),
        name="ssm_step",
    )(u_g, h0r, h0i, prep["bb"], prep["a_re"], prep["a_im"], prep["cs"])


def _a_out_kernel(blocked, ys_ref, u_ref, z_ref, x_ref, d_ref, wg_ref, wo_ref, o_ref, *scratch):
    ys = _from_blocked(ys_ref, scratch[0]) if blocked else ys_ref[...]
    y = ys + d_ref[...] * u_ref[...]
    g = jax.nn.gelu(y)
    y2 = g * jax.nn.sigmoid(_dot(g.astype(BF16), wg_ref[...]))
    v = (y2 * _silu(z_ref[...])).astype(BF16)
    o_ref[...] = x_ref[...] + _dot(v, wo_ref[...])


def _a_out(ys, u, z, x, d, wg, wo, tm, blocked):
    t = x.shape[0]
    tile = pl.BlockSpec((tm, D_MODEL), lambda i: (i, 0))
    full = pl.BlockSpec((D_MODEL, D_MODEL), lambda i: (0, 0))
    ys_spec = pl.BlockSpec((tm // SSM_TBLK, SSM_GROUPS * SSM_BLKW), lambda i: (i, 0)) if blocked else tile
    return pl.pallas_call(
        functools.partial(_a_out_kernel, blocked),
        grid=(t // tm,),
        in_specs=[ys_spec, tile, tile, tile, pl.BlockSpec((1, D_MODEL), lambda i: (0, 0)), full, full],
        out_specs=tile,
        out_shape=jax.ShapeDtypeStruct((t, D_MODEL), F32),
        scratch_shapes=[pltpu.VMEM((D_MODEL // LANES, tm, LANES), F32)] if blocked else [],
        compiler_params=_params("parallel"),
        name="a_out",
    )(ys, u, z, x, d, wg, wo)


def _ssm_prep(lam_re, lam_im, log_dt, b_re, b_im, c_re, c_im, nblk):
    hi = lax.Precision.HIGHEST
    g, p = lam_re.shape
    dt = jnp.exp(log_dt)[:, None]

    def power(n):
        n = jnp.asarray(n, F32).reshape((-1, 1, 1))
        mag = jnp.exp(n * (lam_re * dt)[None])
        ang = n * (lam_im * dt)[None]
        return mag * jnp.cos(ang), mag * jnp.sin(ang)

    pw_re, pw_im = power(jnp.arange(SSM_TBLK + 1))
    abar_re, abar_im = pw_re[1], pw_im[1]
    den = lam_re * lam_re + lam_im * lam_im
    f_re = ((abar_re - 1.0) * lam_re + abar_im * lam_im) / den
    f_im = (abar_im * lam_re - (abar_re - 1.0) * lam_im) / den
    bb_re = f_re[..., None] * b_re - f_im[..., None] * b_im
    bb_im = f_re[..., None] * b_im + f_im[..., None] * b_re
    tr, ti = pw_re[:SSM_TBLK, :, :, None], pw_im[:SSM_TBLK, :, :, None]
    w_re = tr * bb_re[None] - ti * bb_im[None]
    w_im = tr * bb_im[None] + ti * bb_re[None]
    f_re_m = w_re[::-1].transpose(1, 0, 3, 2).reshape(g, SSM_BLKW, p)
    f_im_m = w_im[::-1].transpose(1, 0, 3, 2).reshape(g, SSM_BLKW, p)
    kern = (jnp.einsum('tgpi,gop->gtio', w_re, c_re, precision=hi)
            - jnp.einsum('tgpi,gop->gtio', w_im, c_im, precision=hi))
    lag = jnp.arange(SSM_TBLK)[None, :] - jnp.arange(SSM_TBLK)[:, None]
    m = jnp.where((lag >= 0)[None, :, :, None, None], kern[:, jnp.clip(lag, 0, SSM_TBLK - 1)], 0.0)
    m = m.transpose(0, 1, 3, 2, 4).reshape(g, SSM_BLKW, SSM_BLKW)
    er, ei = pw_re[1:, :, None, :], pw_im[1:, :, None, :]
    ca_re = c_re[None] * er - c_im[None] * ei
    ca_im = c_re[None] * ei + c_im[None] * er
    e_re = ca_re.transpose(1, 3, 0, 2).reshape(g, p, SSM_BLKW)
    e_im = -ca_im.transpose(1, 3, 0, 2).reshape(g, p, SSM_BLKW)

    n_pairs = g // 2
    zf = jnp.zeros((n_pairs, SSM_BLKW, p), F32)
    fp = jnp.concatenate([
        jnp.concatenate([f_re_m[0::2], zf, f_im_m[0::2], zf], axis=2),
        jnp.concatenate([zf, f_re_m[1::2], zf, f_im_m[1::2]], axis=2)], axis=1)
    ze = jnp.zeros((n_pairs, p, SSM_BLKW), F32)
    ep = jnp.concatenate([
        jnp.concatenate([e_re[0::2], ze], axis=2),
        jnp.concatenate([ze, e_re[1::2]], axis=2),
        jnp.concatenate([e_im[0::2], ze], axis=2),
        jnp.concatenate([ze, e_im[1::2]], axis=2)], axis=1)
    n_steps = max(1, math.ceil(math.log2(nblk)))
    ap_re, ap_im = power(SSM_TBLK * 2.0 ** jnp.arange(n_steps))
    ap = jnp.stack([ap_re, ap_im], axis=1)
    ap = ap.reshape(n_steps * 2, n_pairs, 2 * p).transpose(1, 0, 2)
    return {
        "m": m.astype(BF16), "f": fp.astype(BF16), "e": ep.astype(BF16), "apow": ap,
        "bb": jnp.concatenate([bb_re.transpose(0, 2, 1), bb_im.transpose(0, 2, 1)], axis=2),
        "a_re": abar_re[:, None, :], "a_im": abar_im[:, None, :],
        "cs": jnp.concatenate([c_re.transpose(0, 2, 1), -c_im.transpose(0, 2, 1)], axis=1),
    }


def _s5_layer_prompt(x, n_seq, norm, w_in, prep, d, wg, wo):
    u, z, ug = _a_in(x, norm, w_in, ROW_TILE, True)
    yg, st = _ssm_scan(ug, prep, n_seq)
    st = st.reshape(SSM_GROUPS // 2, n_seq, 2, 2, SSM_STATE)
    st = st.transpose(2, 1, 0, 3, 4).reshape(2, n_seq, SSM_GROUPS, SSM_STATE)
    return _a_out(yg, u, z, x, d, wg, wo, ROW_TILE, True), st[0], st[1]


def _s5_layer_sample(x, h0r, h0i, norm, w_in, prep, d, wg, wo):
    n = x.shape[0]
    u, z = _a_in(x, norm, w_in, n, False)
    u_g = u.reshape(n, SSM_GROUPS, SSM_GROUP).transpose(1, 0, 2)
    y_g, hr, hi = _ssm_step(u_g, h0r.transpose(1, 0, 2), h0i.transpose(1, 0, 2), prep)
    ys = y_g.transpose(1, 0, 2).reshape(n, D_MODEL)
    return _a_out(ys, u, z, x, d, wg, wo, n, False), hr.transpose(1, 0, 2), hi.transpose(1, 0, 2)


def _kv_kernel(prompt, x_ref, g_ref, w_ref, cos_ref, sin_ref, k_ref, v_ref, *attn_refs):
    h = _rmsnorm(x_ref[...], g_ref[...]).astype(BF16)
    kv = _dot(h, w_ref[...])
    cos2, sin2 = cos_ref[...], sin_ref[...]
    tm = kv.shape[0]
    for hd in range(N_HEADS):
        sl = slice(hd * HEAD_DIM, (hd + 1) * HEAD_DIM)
        k = _rope(kv[:, sl], cos2, sin2)
        v = kv[:, D_MODEL + hd * HEAD_DIM:D_MODEL + (hd + 1) * HEAD_DIM]
        k_ref[:, sl] = k
        v_ref[:, sl] = v
        if prompt:
            kb_ref, vt_ref, mean_ref = attn_refs
            kb_ref[hd] = k.astype(BF16)
            for b in range(tm // MOBA_BLOCK):
                rows = slice(b * MOBA_BLOCK, (b + 1) * MOBA_BLOCK)
                mean_ref[b, :, sl] = jnp.sum(k[rows], axis=0, keepdims=True) * (1.0 / MOBA_BLOCK)
                vt_ref[hd, b, :HEAD_DIM, :] = v[rows].T.astype(BF16)
                vt_ref[hd, b, HEAD_DIM:, :] = jnp.ones((V_ROWS - HEAD_DIM, MOBA_BLOCK), BF16)


def _kv_proj(x, g, w, cos2, sin2, n_seq, tm, prompt):
    t = x.shape[0]
    per_seq = t // n_seq // tm
    tile = pl.BlockSpec((tm, D_MODEL), lambda i: (i, 0))
    rope_tile = pl.BlockSpec((tm, HEAD_DIM), lambda i: (i % per_seq, 0))
    out_specs = [tile, tile]
    out_shape = [jax.ShapeDtypeStruct((t, D_MODEL), F32)] * 2
    if prompt:
        nb = tm // MOBA_BLOCK
        out_specs += [
            pl.BlockSpec((None, N_HEADS, tm, HEAD_DIM), lambda i: (i // per_seq, 0, i % per_seq, 0)),
            pl.BlockSpec((None, N_HEADS, nb, V_ROWS, MOBA_BLOCK), lambda i: (i // per_seq, 0, i % per_seq, 0, 0)),
            pl.BlockSpec((nb, 1, D_MODEL), lambda i: (i, 0, 0))]
        out_shape += [
            jax.ShapeDtypeStruct((n_seq, N_HEADS, t // n_seq, HEAD_DIM), BF16),
            jax.ShapeDtypeStruct((n_seq, N_HEADS, t // n_seq // MOBA_BLOCK, V_ROWS, MOBA_BLOCK), BF16),
            jax.ShapeDtypeStruct((t // MOBA_BLOCK, 1, D_MODEL), F32)]
    return pl.pallas_call(
        functools.partial(_kv_kernel, prompt),
        grid=(t // tm,),
        in_specs=[tile, pl.BlockSpec((1, D_MODEL), lambda i: (0, 0)),
                  pl.BlockSpec((D_MODEL, 2 * D_MODEL), lambda i: (0, 0)), rope_tile, rope_tile],
        out_specs=out_specs,
        out_shape=out_shape,
        compiler_params=_params("parallel"),
        name="kv_proj",
    )(x, g, w, cos2, sin2)


def _q_kernel(blocks_per_seq, x_ref, g_ref, w_ref, cos_ref, sin_ref, kbar_ref, q_ref, bias_ref, z_ref):
    own = pl.program_id(0) % blocks_per_seq
    h = _rmsnorm(x_ref[...], g_ref[...]).astype(BF16)
    qz = _dot(h, w_ref[...])
    z_ref[...] = qz[:, D_MODEL:]
    cos2, sin2 = cos_ref[...], sin_ref[...]
    blk = lax.broadcasted_iota(jnp.int32, (blocks_per_seq, MOBA_BLOCK), 0)
    valid = blk < own
    for hd in range(N_HEADS):
        q = _rope(qz[:, hd * HEAD_DIM:(hd + 1) * HEAD_DIM], cos2, sin2)
        s = _dot_nt(kbar_ref[hd], q, precision=lax.Precision.HIGHEST)
        sel, _ = _top3(s, valid, 0)
        q_ref[hd] = (q * (HEAD_DIM ** -0.5)).astype(BF16)
        bias_ref[hd] = jnp.where(sel > 0.5, 0.0, NEG_INF)


def _q_proj(x, g, w, cos2, sin2, kbar, n_seq):
    t = x.shape[0]
    bps = t // n_seq // MOBA_BLOCK
    tile = pl.BlockSpec((MOBA_BLOCK, D_MODEL), lambda i: (i, 0))
    rope_tile = pl.BlockSpec((MOBA_BLOCK, HEAD_DIM), lambda i: (i % bps, 0))
    return pl.pallas_call(
        functools.partial(_q_kernel, bps),
        grid=(t // MOBA_BLOCK,),
        in_specs=[tile, pl.BlockSpec((1, D_MODEL), lambda i: (0, 0)),
                  pl.BlockSpec((D_MODEL, 2 * D_MODEL), lambda i: (0, 0)), rope_tile, rope_tile,
                  pl.BlockSpec((None, N_HEADS, bps, HEAD_DIM), lambda i: (i // bps, 0, 0, 0))],
        out_specs=[pl.BlockSpec((None, N_HEADS, MOBA_BLOCK, HEAD_DIM), lambda i: (i // bps, 0, i % bps, 0)),
                   pl.BlockSpec((None, N_HEADS, None, bps, MOBA_BLOCK), lambda i: (i // bps, 0, i % bps, 0, 0)),
                   tile],
        out_shape=[jax.ShapeDtypeStruct((n_seq, N_HEADS, t // n_seq, HEAD_DIM), BF16),
                   jax.ShapeDtypeStruct((n_seq, N_HEADS, bps, bps, MOBA_BLOCK), F32),
                   jax.ShapeDtypeStruct((t, D_MODEL), F32)],
        compiler_params=_params("parallel"),
        name="q_proj",
    )(x, g, w, cos2, sin2, kbar)


def _moba_kernel(q_ref, bias_ref, k_ref, vt_ref, o_ref, s_scr, acc_scr):
    i = pl.program_id(2)
    row = lax.broadcasted_iota(jnp.int32, (MOBA_BLOCK, LANES), 0)
    col = lax.broadcasted_iota(jnp.int32, (MOBA_BLOCK, LANES), 1)
    chains = [(h, c) for h in range(MOBA_HEADS) for c in range(MOBA_BLOCK // LANES)]
    nc = len(chains)

    def scores(h, c, blk):
        off = pl.multiple_of(blk * MOBA_BLOCK, MOBA_BLOCK)
        return _dot_nt(k_ref[h, pl.ds(off, MOBA_BLOCK), :], q_ref[h, c * LANES:(c + 1) * LANES, :])

    ms = []
    for n, (h, c) in enumerate(chains):
        s = jnp.where(row <= col + c * LANES, scores(h, c, i), NEG_INF)
        m = jnp.max(s, axis=0, keepdims=True)
        acc_scr[n] = _dot(vt_ref[h, i], jnp.exp(s - m).astype(BF16))
        ms.append(m)

    def body(t, ms):
        blks = (2 * t, 2 * t + 1)
        mbs = []
        for n, (h, c) in enumerate(chains):
            for u, blk in enumerate(blks):
                s = scores(h, c, blk)
                s_scr[u * nc + n] = s
                mbs.append(jnp.max(s, axis=0, keepdims=True))
        out = []
        for n, (h, c) in enumerate(chains):
            b = [bias_ref[h, pl.ds(blk, 1), :][:, c * LANES:(c + 1) * LANES] for blk in blks]
            m_new = jnp.maximum(ms[n], jnp.maximum(mbs[2 * n] + b[0], mbs[2 * n + 1] + b[1]))
            alpha = jnp.exp(ms[n] - m_new)
            p = [jnp.exp(s_scr[u * nc + n] - (m_new - b[u])).astype(BF16) for u in range(2)]
            acc_scr[n] = alpha * acc_scr[n] + (_dot(vt_ref[h, blks[0]], p[0]) + _dot(vt_ref[h, blks[1]], p[1]))
            out.append(m_new)
        return tuple(out)

    lax.fori_loop(0, (i + 1) // 2, body, tuple(ms))
    for n, (h, c) in enumerate(chains):
        acc = acc_scr[n]
        o = acc[:HEAD_DIM, :] / acc[HEAD_DIM:HEAD_DIM + 1, :]
        o_ref[c * LANES:(c + 1) * LANES, h * HEAD_DIM:(h + 1) * HEAD_DIM] = o.T.astype(o_ref.dtype)


def _moba_prompt(q, bias, kb, vt):
    n_seq, _, seq, _ = q.shape
    nblk = seq // MOBA_BLOCK
    hg = MOBA_HEADS
    n_chains = hg * (MOBA_BLOCK // LANES)
    return pl.pallas_call(
        _moba_kernel,
        grid=(n_seq, N_HEADS // hg, nblk),
        in_specs=[pl.BlockSpec((None, hg, MOBA_BLOCK, HEAD_DIM), lambda n, h, i: (n, h, i, 0)),
                  pl.BlockSpec((None, hg, None, nblk, MOBA_BLOCK), lambda n, h, i: (n, h, i, 0, 0)),
                  pl.BlockSpec((None, hg, seq, HEAD_DIM), lambda n, h, i: (n, h, 0, 0)),
                  pl.BlockSpec((None, hg, nblk, V_ROWS, MOBA_BLOCK), lambda n, h, i: (n, h, 0, 0, 0))],
        out_specs=pl.BlockSpec((None, MOBA_BLOCK, hg * HEAD_DIM), lambda n, h, i: (n, i, h)),
        out_shape=jax.ShapeDtypeStruct((n_seq, seq, D_MODEL), BF16),
        scratch_shapes=[pltpu.VMEM((2 * n_chains, MOBA_BLOCK, LANES), F32),
                        pltpu.VMEM((n_chains, V_ROWS, LANES), F32)],
        compiler_params=_params("parallel", "parallel", "arbitrary"),
        name="moba_prompt",
    )(q, bias, kb, vt)


def _b_out_kernel(final, o_ref, z_ref, x_ref, w_ref, g_ref, y_ref):
    v = (o_ref[...].astype(F32) * _silu(z_ref[...])).astype(BF16)
    x = x_ref[...] + _dot(v, w_ref[...])
    y_ref[...] = _rmsnorm(x, g_ref[...]) if final else x


def _b_out(o, z, x, w, g, tm, final):
    t = x.shape[0]
    tile = pl.BlockSpec((tm, D_MODEL), lambda i: (i, 0))
    return pl.pallas_call(
        functools.partial(_b_out_kernel, final),
        grid=(t // tm,),
        in_specs=[tile, tile, tile, pl.BlockSpec((D_MODEL, D_MODEL), lambda i: (0, 0)),
                  pl.BlockSpec((1, D_MODEL), lambda i: (0, 0))],
        out_specs=tile,
        out_shape=jax.ShapeDtypeStruct((t, D_MODEL), F32),
        compiler_params=_params("parallel"),
        name="b_out",
    )(o, z, x, w, g)


PAGES_PER_STEP = 16


def _page_mean_kernel(*refs):
    pages, out_ref = refs[1:1 + PAGES_PER_STEP], refs[-1]
    per_block = MOBA_BLOCK // PAGE_SIZE
    for b in range(PAGES_PER_STEP // per_block):
        tot = None
        for r in range(per_block):
            page = pages[per_block * b + r][...].reshape(PAGE_SIZE, N_HEADS, HEAD_DIM)
            part = jnp.sum(page, axis=0)
            tot = part if tot is None else tot + part
        tot = tot * (1.0 / MOBA_BLOCK)
        for hd in range(N_HEADS):
            out_ref[hd, b:b + 1, :] = tot[hd:hd + 1, :]


def _page_means(cache_k2, page_table):
    n, n_pages = page_table.shape
    steps = n_pages // PAGES_PER_STEP
    nb = PAGES_PER_STEP * PAGE_SIZE // MOBA_BLOCK

    def page_spec(r):
        return pl.BlockSpec((None, PAGE_SIZE * N_HEADS, HEAD_DIM),
                            lambda s, j, pt, r=r: (pt[s * n_pages + j * PAGES_PER_STEP + r], 0, 0))

    return pl.pallas_call(
        _page_mean_kernel,
        grid_spec=pltpu.PrefetchScalarGridSpec(
            num_scalar_prefetch=1,
            grid=(n, steps),
            in_specs=[page_spec(r) for r in range(PAGES_PER_STEP)],
            out_specs=pl.BlockSpec((None, N_HEADS, nb, HEAD_DIM), lambda s, j, pt: (s, 0, j, 0))),
        out_shape=jax.ShapeDtypeStruct((n, N_HEADS, n_pages * PAGE_SIZE // MOBA_BLOCK, HEAD_DIM), F32),
        compiler_params=_params("parallel", "arbitrary"),
        name="page_means",
    )(page_table.reshape(-1), *([cache_k2] * PAGES_PER_STEP))


def _q_sample_kernel(x_ref, g_ref, w_ref, cos_ref, sin_ref, q_ref, z_ref):
    h = _rmsnorm(x_ref[...], g_ref[...]).astype(BF16)
    qz = _dot(h, w_ref[...])
    z_ref[...] = qz[:, D_MODEL:]
    cos2, sin2 = cos_ref[...], sin_ref[...]
    for hd in range(N_HEADS):
        sl = slice(hd * HEAD_DIM, (hd + 1) * HEAD_DIM)
        q_ref[:, sl] = _rope(qz[:, sl], cos2, sin2)


def _q_sample(x, g, w, cos2, sin2):
    n = x.shape[0]
    return pl.pallas_call(
        _q_sample_kernel,
        out_shape=[jax.ShapeDtypeStruct((n, D_MODEL), F32)] * 2,
        compiler_params=pltpu.CompilerParams(vmem_limit_bytes=VMEM_LIMIT),
        name="q_sample",
    )(x, g, w, cos2, sin2)


def _pick_kernel(q_ref, kbar_ref, knew_ref, idx_ref):
    n_past = kbar_ref.shape[1]
    lane_o = lax.broadcasted_iota(jnp.int32, (8, LANES), 1)
    row_o = lax.broadcasted_iota(jnp.int32, (8, LANES), 0)
    out = jnp.zeros((8, LANES), jnp.int32)
    for hd in range(N_HEADS):
        sl = slice(hd * HEAD_DIM, (hd + 1) * HEAD_DIM)
        q8 = jnp.broadcast_to(q_ref[:, sl], (8, HEAD_DIM))
        own_mean = jnp.broadcast_to(knew_ref[:, sl] * (1.0 / MOBA_BLOCK), (8, HEAD_DIM))
        means = jnp.concatenate([kbar_ref[hd], own_mean], axis=0)
        s = _dot_nt(q8, means, precision=lax.Precision.HIGHEST)
        lane = lax.broadcasted_iota(jnp.int32, s.shape, 1)
        _, picks = _top3(s, lane < n_past, 1)
        for r, idx in enumerate(picks):
            out = jnp.where((row_o == hd) & (lane_o == r), idx[0:1, :], out)
    idx_ref[...] = out


def _pick_blocks(q, kbar, knew):
    n, _, n_past, _ = kbar.shape
    return pl.pallas_call(
        _pick_kernel,
        grid=(n,),
        in_specs=[pl.BlockSpec((None, 1, D_MODEL), lambda s: (s, 0, 0)),
                  pl.BlockSpec((None, N_HEADS, n_past, HEAD_DIM), lambda s: (s, 0, 0, 0)),
                  pl.BlockSpec((None, 1, D_MODEL), lambda s: (s, 0, 0))],
        out_specs=pl.BlockSpec((None, 8, LANES), lambda s: (s, 0, 0)),
        out_shape=jax.ShapeDtypeStruct((n, 8, LANES), jnp.int32),
        compiler_params=_params("parallel"),
        name="pick_blocks",
    )(q.reshape(n, 1, D_MODEL), kbar, knew.reshape(n, 1, D_MODEL))


N_FETCH = MOBA_TOPK * (MOBA_BLOCK // PAGE_SIZE)


def _decode_attn_kernel(*refs):
    q_ref, kn_ref, vn_ref = refs[2:5]
    k_pages = refs[5:5 + N_FETCH]
    v_pages = refs[5 + N_FETCH:5 + 2 * N_FETCH]
    o_ref = refs[-1]
    scale = HEAD_DIM ** -0.5
    q = q_ref[...]
    q8 = jnp.broadcast_to(q, (8, HEAD_DIM))
    head_rows = pl.ds(pl.program_id(1), PAGE_SIZE, stride=N_HEADS)
    kc = jnp.concatenate([r[head_rows, :] for r in k_pages], axis=0)
    vc = jnp.concatenate([r[head_rows, :] for r in v_pages], axis=0)
    s = _dot_nt(q8, kc)[0:1, :] * scale
    s_new = jnp.sum(q * kn_ref[...], axis=1, keepdims=True) * scale
    m = jnp.maximum(jnp.max(s, axis=1, keepdims=True), s_new)
    p = jnp.exp(s - m)
    p_new = jnp.exp(s_new - m)
    l = jnp.sum(p, axis=1, keepdims=True) + p_new
    pv = _dot(jnp.broadcast_to(p, (8, p.shape[1])), vc)[0:1, :]
    o_ref[...] = (pv + p_new * vn_ref[...]) / l


def _decode_attn(q, k_new, v_new, cache_k3, cache_v3, page_table, picks):
    n, n_pages = page_table.shape
    per_block = MOBA_BLOCK // PAGE_SIZE

    def page_spec(f):
        def index(s, h, pk, pt):
            blk = pk[(s * N_HEADS + h) * MOBA_TOPK + f // per_block]
            return (pt[s * n_pages + blk * per_block + f % per_block], 0, 0)
        return pl.BlockSpec((None, PAGE_SIZE * N_HEADS, HEAD_DIM), index)

    row = pl.BlockSpec((None, None, 1, HEAD_DIM), lambda s, h, pk, pt: (s, h, 0, 0))
    return pl.pallas_call(
        _decode_attn_kernel,
        grid_spec=pltpu.PrefetchScalarGridSpec(
            num_scalar_prefetch=2,
            grid=(n, N_HEADS),
            in_specs=[row, row, row] + [page_spec(f) for f in range(N_FETCH)] * 2,
            out_specs=row),
        out_shape=jax.ShapeDtypeStruct((n, N_HEADS, 1, HEAD_DIM), F32),
        compiler_params=_params("parallel", "parallel"),
        name="decode_attn",
    )(picks, page_table.reshape(-1),
      q.reshape(n, N_HEADS, 1, HEAD_DIM), k_new.reshape(n, N_HEADS, 1, HEAD_DIM),
      v_new.reshape(n, N_HEADS, 1, HEAD_DIM),
      *([cache_k3] * N_FETCH), *([cache_v3] * N_FETCH))


def _rope_tables(pos):
    half = HEAD_DIM // 2
    inv = ROPE_THETA ** (-jnp.arange(half, dtype=F32) / half)
    ang = pos.astype(F32)[:, None] * inv[None, :]
    cos, sin = jnp.cos(ang), jnp.sin(ang)
    return jnp.concatenate([cos, cos], axis=1), jnp.concatenate([-sin, sin], axis=1)


def kernel(x_prompt, x_sample, state_ssm_re, state_ssm_im, cache_k, cache_v, page_table, a_norm, a_w_in, a_lam_re, a_lam_im, a_log_dt, a_b_re, a_b_im, a_c_re, a_c_im, a_d, a_w_glu, a_w_out, kv_norm, w_kv, b_norm, b_w_in, b_w_out, final_norm):
    n_seq, seq, _ = x_prompt.shape
    n_dec = x_sample.shape[0]
    n_pool = cache_k.shape[0]
    past_len = page_table.shape[1] * PAGE_SIZE
    assert x_sample.shape[1] == 1 and seq % ROW_TILE == 0 and ROW_TILE % MOBA_BLOCK == 0
    assert past_len % (PAGES_PER_STEP * PAGE_SIZE) == 0 and past_len // MOBA_BLOCK >= MOBA_TOPK
    assert seq // MOBA_BLOCK <= LANES and (seq // MOBA_BLOCK) % 8 == 0
    n_a, n_b = a_norm.shape[0], b_norm.shape[0]

    xp = x_prompt.reshape(n_seq * seq, D_MODEL)
    xs = x_sample.reshape(n_dec, D_MODEL)
    cos_p, sin_p = _rope_tables(jnp.arange(seq, dtype=jnp.int32))
    cos_s, sin_s = _rope_tables(jnp.full((n_dec,), past_len, jnp.int32))

    st_p_re, st_p_im, st_s_re, st_s_im = [], [], [], []
    for l in range(n_a):
        prep = _ssm_prep(a_lam_re[l], a_lam_im[l], a_log_dt[l], a_b_re[l], a_b_im[l], a_c_re[l], a_c_im[l],
                         seq // SSM_TBLK)
        norm, d = a_norm[l][None], a_d[l][None]
        w_in, wg, wo = a_w_in[l].astype(BF16), a_w_glu[l].astype(BF16), a_w_out[l].astype(BF16)
        xp, hr, hi = _s5_layer_prompt(xp, n_seq, norm, w_in, prep, d, wg, wo)
        st_p_re.append(hr)
        st_p_im.append(hi)
        xs, hr, hi = _s5_layer_sample(xs, state_ssm_re[l], state_ssm_im[l], norm, w_in, prep, d, wg, wo)
        st_s_re.append(hr)
        st_s_im.append(hi)

    w_kv_b = w_kv.astype(BF16)
    k_p, v_p, kb, vt, kbar_p = _kv_proj(xp, kv_norm[None], w_kv_b, cos_p, sin_p, n_seq, ROW_TILE, True)
    k_s, v_s = _kv_proj(xs, kv_norm[None], w_kv_b, cos_s, sin_s, 1, n_dec, False)
    nblk = seq // MOBA_BLOCK
    kbar_p = kbar_p.reshape(n_seq, nblk, N_HEADS, HEAD_DIM).transpose(0, 2, 1, 3)
    cache_k3 = cache_k.reshape(n_pool, PAGE_SIZE * N_HEADS, HEAD_DIM)
    cache_v3 = cache_v.reshape(n_pool, PAGE_SIZE * N_HEADS, HEAD_DIM)
    kbar_s = _page_means(cache_k3, page_table)

    for j in range(n_b):
        w_in, wo = b_w_in[j].astype(BF16), b_w_out[j].astype(BF16)
        final = j == n_b - 1
        q, bias, z = _q_proj(xp, b_norm[j][None], w_in, cos_p, sin_p, kbar_p, n_seq)
        o = _moba_prompt(q, bias, kb, vt).reshape(n_seq * seq, D_MODEL)
        xp = _b_out(o, z, xp, wo, final_norm[None], ROW_TILE, final)

        q, z = _q_sample(xs, b_norm[j][None], w_in, cos_s, sin_s)
        picks = _pick_blocks(q, kbar_s, k_s)[:, :, :MOBA_TOPK].reshape(-1)
        o = _decode_attn(q, k_s, v_s, cache_k3, cache_v3, page_table, picks).reshape(n_dec, D_MODEL)
        xs = _b_out(o, z, xs, wo, final_norm[None], n_dec, final)

    return (xp.reshape(n_seq, seq, D_MODEL), xs.reshape(n_dec, 1, D_MODEL),
            jnp.stack(st_p_re), jnp.stack(st_p_im), jnp.stack(st_s_re), jnp.stack(st_s_im),
            k_p.reshape(n_seq, seq, N_HEADS, HEAD_DIM), v_p.reshape(n_seq, seq, N_HEADS, HEAD_DIM),
            k_s.reshape(n_dec, 1, N_HEADS, HEAD_DIM), v_s.reshape(n_dec, 1, N_HEADS, HEAD_DIM))
```

```python
import functools
import math

import jax
import jax.numpy as jnp
from jax import lax
from jax.experimental import pallas as pl
from jax.experimental.pallas import tpu as pltpu

D_MODEL = 1024
SSM_GROUP = 16
SSM_GROUPS = D_MODEL // SSM_GROUP
SSM_STATE = 64
SSM_TBLK = 16
SSM_BLKW = SSM_TBLK * SSM_GROUP
HEAD_DIM = 128
N_HEADS = D_MODEL // HEAD_DIM
MOBA_BLOCK = 256
MOBA_TOPK = 3
PAGE_SIZE = 128
ROPE_THETA = 10000.0
RMS_EPS = 1e-6
NEG_INF = -1e30
LANES = 128
VMEM_LIMIT = 48 * 1024 * 1024
ROW_TILE = 512
MOBA_HEADS = 4
V_ROWS = HEAD_DIM + 16

F32 = jnp.float32
BF16 = jnp.bfloat16


def _params(*sem):
    return pltpu.CompilerParams(dimension_semantics=sem, vmem_limit_bytes=VMEM_LIMIT)


def _rmsnorm(x, g):
    r = lax.rsqrt(jnp.mean(x * x, axis=-1, keepdims=True) + RMS_EPS)
    return x * r * g


def _dot(a, b):
    return jnp.dot(a, b, preferred_element_type=F32)


def _dot_nt(a, b, precision=None):
    return lax.dot_general(a, b, (((1,), (1,)), ((), ())), precision=precision,
                           preferred_element_type=F32)


def _rope(x, cos2, sin2):
    return x * cos2 + pltpu.roll(x, HEAD_DIM // 2, 1) * sin2


def _silu(z):
    return z * jax.nn.sigmoid(z)


def _top3(s, valid, axis):
    n_lanes = s.shape[axis]
    lane = lax.broadcasted_iota(jnp.int32, s.shape, axis)
    sm = jnp.where(valid, s, NEG_INF)
    sel = jnp.zeros(s.shape, F32)
    picks = []
    for _ in range(MOBA_TOPK):
        m = jnp.max(sm, axis=axis, keepdims=True)
        idx = jnp.min(jnp.where(sm == m, lane, n_lanes), axis=axis, keepdims=True)
        hit = lane == idx
        sel = jnp.where(hit, jnp.where(valid, 1.0, sel), sel)
        sm = jnp.where(hit, NEG_INF, sm)
        picks.append(idx)
    return sel, picks


GROUPS_PER_TILE = LANES // SSM_GROUP
STEP_TILES = SSM_BLKW // LANES


def _slot_transpose(xs, slot):
    xs = list(xs)
    step = GROUPS_PER_TILE // 2
    while step:
        low = (slot & step) == 0
        for i in range(GROUPS_PER_TILE):
            if i & step == 0:
                a, b = xs[i], xs[i + step]
                xs[i] = jnp.where(low, a, pltpu.roll(b, step * SSM_GROUP, 1))
                xs[i + step] = jnp.where(low, pltpu.roll(a, LANES - step * SSM_GROUP, 1), b)
        step //= 2
    return xs


def _to_blocked(u, tiles_ref, ug_ref):
    nb = ug_ref.shape[0]
    slot = lax.broadcasted_iota(jnp.int32, (nb, LANES), 1) // SSM_GROUP
    for tile in range(D_MODEL // LANES):
        tiles_ref[tile] = u[:, tile * LANES:(tile + 1) * LANES]
    for tile in range(D_MODEL // LANES):
        for hh in range(STEP_TILES):
            steps = [tiles_ref[tile, pl.ds(hh * GROUPS_PER_TILE + rp, nb, stride=SSM_TBLK), :]
                     for rp in range(GROUPS_PER_TILE)]
            for gp, piece in enumerate(_slot_transpose(steps, slot)):
                col = (tile * GROUPS_PER_TILE + gp) * SSM_BLKW + hh * LANES
                ug_ref[:, col:col + LANES] = piece.astype(ug_ref.dtype)


def _from_blocked(yg_ref, tiles_ref):
    nb = yg_ref.shape[0]
    slot = lax.broadcasted_iota(jnp.int32, (nb, LANES), 1) // SSM_GROUP
    for tile in range(D_MODEL // LANES):
        for hh in range(STEP_TILES):
            cols = [(tile * GROUPS_PER_TILE + gp) * SSM_BLKW + hh * LANES for gp in range(GROUPS_PER_TILE)]
            for rp, piece in enumerate(_slot_transpose([yg_ref[:, c:c + LANES] for c in cols], slot)):
                tiles_ref[tile, pl.ds(hh * GROUPS_PER_TILE + rp, nb, stride=SSM_TBLK), :] = piece
    return jnp.concatenate([tiles_ref[tile] for tile in range(D_MODEL // LANES)], axis=1)


def _a_in_kernel(blocked, x_ref, g_ref, w_ref, u_ref, z_ref, *rest):
    h = _rmsnorm(x_ref[...], g_ref[...]).astype(BF16)
    uz = _dot(h, w_ref[...])
    u_ref[...] = uz[:, :D_MODEL]
    z_ref[...] = uz[:, D_MODEL:]
    if blocked:
        ug_ref, tiles_ref = rest
        _to_blocked(uz[:, :D_MODEL], tiles_ref, ug_ref)


def _a_in(x, g, w, tm, blocked):
    t = x.shape[0]
    tile = pl.BlockSpec((tm, D_MODEL), lambda i: (i, 0))
    out_specs, out_shape = [tile, tile], [jax.ShapeDtypeStruct((t, D_MODEL), F32)] * 2
    if blocked:
        out_specs.append(pl.BlockSpec((tm // SSM_TBLK, SSM_GROUPS * SSM_BLKW), lambda i: (i, 0)))
        out_shape.append(jax.ShapeDtypeStruct((t // SSM_TBLK, SSM_GROUPS * SSM_BLKW), BF16))
    return pl.pallas_call(
        functools.partial(_a_in_kernel, blocked),
        grid=(t // tm,),
        in_specs=[tile, pl.BlockSpec((1, D_MODEL), lambda i: (0, 0)),
                  pl.BlockSpec((D_MODEL, 2 * D_MODEL), lambda i: (0, 0))],
        out_specs=out_specs,
        out_shape=out_shape,
        scratch_shapes=[pltpu.VMEM((D_MODEL // LANES, tm, LANES), F32)] if blocked else [],
        compiler_params=_params("parallel"),
        name="a_in",
    )(x, g, w)


def _ssm_scan_kernel(n_seq, ug_ref, m_ref, f_ref, e_ref, ap_ref, y_ref, st_ref):
    rows = ug_ref.shape[0]
    nblk = rows // n_seq
    ug = ug_ref[...]
    s = _dot(ug, f_ref[...])
    row = lax.broadcasted_iota(jnp.int32, (nblk, LANES), 0)
    hp = []
    for q in range(n_seq):
        re = s[q * nblk:(q + 1) * nblk, :LANES]
        im = s[q * nblk:(q + 1) * nblk, LANES:]
        step, k = 1, 0
        while step < nblk:
            ar = ap_ref[2 * k:2 * k + 1, :]
            ai = ap_ref[2 * k + 1:2 * k + 2, :]
            sr = jnp.where(row >= step, pltpu.roll(re, step, 0), 0.0)
            si = jnp.where(row >= step, pltpu.roll(im, step, 0), 0.0)
            re, im = re + ar * sr - ai * si, im + ar * si + ai * sr
            step, k = step * 2, k + 1
        st_ref[2 * q:2 * q + 1, :] = re[nblk - 1:nblk, :]
        st_ref[2 * q + 1:2 * q + 2, :] = im[nblk - 1:nblk, :]
        pr = jnp.where(row >= 1, pltpu.roll(re, 1, 0), 0.0)
        pi = jnp.where(row >= 1, pltpu.roll(im, 1, 0), 0.0)
        hp.append(jnp.concatenate([pr, pi], axis=1))
    hprev = jnp.concatenate(hp, axis=0).astype(BF16)
    carry = _dot(hprev, e_ref[...])
    y_ref[:, :SSM_BLKW] = _dot(ug[:, :SSM_BLKW], m_ref[0]) + carry[:, :SSM_BLKW]
    y_ref[:, SSM_BLKW:] = _dot(ug[:, SSM_BLKW:], m_ref[1]) + carry[:, SSM_BLKW:]


def _ssm_scan(ug, prep, n_seq):
    rows = ug.shape[0]
    n_pairs = SSM_GROUPS // 2
    n_pow = prep["apow"].shape[1]
    return pl.pallas_call(
        functools.partial(_ssm_scan_kernel, n_seq),
        grid=(n_pairs,),
        in_specs=[pl.BlockSpec((rows, 2 * SSM_BLKW), lambda p: (0, p)),
                  pl.BlockSpec((2, SSM_BLKW, SSM_BLKW), lambda p: (p, 0, 0)),
                  pl.BlockSpec((None, 2 * SSM_BLKW, 2 * LANES), lambda p: (p, 0, 0)),
                  pl.BlockSpec((None, 2 * LANES, 2 * SSM_BLKW), lambda p: (p, 0, 0)),
                  pl.BlockSpec((None, n_pow, LANES), lambda p: (p, 0, 0))],
        out_specs=[pl.BlockSpec((rows, 2 * SSM_BLKW), lambda p: (0, p)),
                   pl.BlockSpec((None, 2 * n_seq, LANES), lambda p: (p, 0, 0))],
        out_shape=[jax.ShapeDtypeStruct((rows, SSM_GROUPS * SSM_BLKW), F32),
                   jax.ShapeDtypeStruct((n_pairs, 2 * n_seq, LANES), F32)],
        compiler_params=_params("parallel"),
        name="ssm_scan",
    )(ug, prep["m"], prep["f"], prep["e"], prep["apow"])


def _ssm_step_kernel(u_ref, hr_ref, hi_ref, bb_ref, ar_ref, ai_ref, cs_ref, y_ref, or_ref, oi_ref):
    bu = lax.dot_general(u_ref[...], bb_ref[...], (((2,), (1,)), ((0,), (0,))),
                         preferred_element_type=F32)
    h0r, h0i = hr_ref[...], hi_ref[...]
    ar, ai = ar_ref[...], ai_ref[...]
    hr = bu[:, :, :SSM_STATE] + ar * h0r - ai * h0i
    hi = bu[:, :, SSM_STATE:] + ar * h0i + ai * h0r
    or_ref[...] = hr
    oi_ref[...] = hi
    y_ref[...] = lax.dot_general(jnp.concatenate([hr, hi], axis=2), cs_ref[...],
                                 (((2,), (1,)), ((0,), (0,))), preferred_element_type=F32)


def _ssm_step(u_g, h0r, h0i, prep):
    g, n, _ = u_g.shape
    return pl.pallas_call(
        _ssm_step_kernel,
        out_shape=[jax.ShapeDtypeStruct((g, n, SSM_GROUP), F32),
                   jax.ShapeDtypeStruct((g, n, SSM_STATE), F32),
                   jax.ShapeDtypeStruct((g, n, SSM_STATE), F32)],
        compiler_params=pltpu.CompilerParams(vmem_limit_bytes=VMEM_LIMIT),
        name="ssm_step",
    )(u_g, h0r, h0i, prep["bb"], prep["a_re"], prep["a_im"], prep["cs"])


def _a_out_kernel(blocked, ys_ref, u_ref, z_ref, x_ref, d_ref, wg_ref, wo_ref, o_ref, *scratch):
    ys = _from_blocked(ys_ref, scratch[0]) if blocked else ys_ref[...]
    y = ys + d_ref[...] * u_ref[...]
    g = jax.nn.gelu(y)
    y2 = g * jax.nn.sigmoid(_dot(g.astype(BF16), wg_ref[...]))
    v = (y2 * _silu(z_ref[...])).astype(BF16)
    o_ref[...] = x_ref[...] + _dot(v, wo_ref[...])


def _a_out(ys, u, z, x, d, wg, wo, tm, blocked):
    t = x.shape[0]
    tile = pl.BlockSpec((tm, D_MODEL), lambda i: (i, 0))
    full = pl.BlockSpec((D_MODEL, D_MODEL), lambda i: (0, 0))
    ys_spec = pl.BlockSpec((tm // SSM_TBLK, SSM_GROUPS * SSM_BLKW), lambda i: (i, 0)) if blocked else tile
    return pl.pallas_call(
        functools.partial(_a_out_kernel, blocked),
        grid=(t // tm,),
        in_specs=[ys_spec, tile, tile, tile, pl.BlockSpec((1, D_MODEL), lambda i: (0, 0)), full, full],
        out_specs=tile,
        out_shape=jax.ShapeDtypeStruct((t, D_MODEL), F32),
        scratch_shapes=[pltpu.VMEM((D_MODEL // LANES, tm, LANES), F32)] if blocked else [],
        compiler_params=_params("parallel"),
        name="a_out",
    )(ys, u, z, x, d, wg, wo)


def _ssm_prep(lam_re, lam_im, log_dt, b_re, b_im, c_re, c_im, nblk):
    hi = lax.Precision.HIGHEST
    g, p = lam_re.shape
    dt = jnp.exp(log_dt)[:, None]

    def power(n):
        n = jnp.asarray(n, F32).reshape((-1, 1, 1))
        mag = jnp.exp(n * (lam_re * dt)[None])
        ang = n * (lam_im * dt)[None]
        return mag * jnp.cos(ang), mag * jnp.sin(ang)

    pw_re, pw_im = power(jnp.arange(SSM_TBLK + 1))
    abar_re, abar_im = pw_re[1], pw_im[1]
    den = lam_re * lam_re + lam_im * lam_im
    f_re = ((abar_re - 1.0) * lam_re + abar_im * lam_im) / den
    f_im = (abar_im * lam_re - (abar_re - 1.0) * lam_im) / den
    bb_re = f_re[..., None] * b_re - f_im[..., None] * b_im
    bb_im = f_re[..., None] * b_im + f_im[..., None] * b_re
    tr, ti = pw_re[:SSM_TBLK, :, :, None], pw_im[:SSM_TBLK, :, :, None]
    w_re = tr * bb_re[None] - ti * bb_im[None]
    w_im = tr * bb_im[None] + ti * bb_re[None]
    f_re_m = w_re[::-1].transpose(1, 0, 3, 2).reshape(g, SSM_BLKW, p)
    f_im_m = w_im[::-1].transpose(1, 0, 3, 2).reshape(g, SSM_BLKW, p)
    kern = (jnp.einsum('tgpi,gop->gtio', w_re, c_re, precision=hi)
            - jnp.einsum('tgpi,gop->gtio', w_im, c_im, precision=hi))
    lag = jnp.arange(SSM_TBLK)[None, :] - jnp.arange(SSM_TBLK)[:, None]
    m = jnp.where((lag >= 0)[None, :, :, None, None], kern[:, jnp.clip(lag, 0, SSM_TBLK - 1)], 0.0)
    m = m.transpose(0, 1, 3, 2, 4).reshape(g, SSM_BLKW, SSM_BLKW)
    er, ei = pw_re[1:, :, None, :], pw_im[1:, :, None, :]
    ca_re = c_re[None] * er - c_im[None] * ei
    ca_im = c_re[None] * ei + c_im[None] * er
    e_re = ca_re.transpose(1, 3, 0, 2).reshape(g, p, SSM_BLKW)
    e_im = -ca_im.transpose(1, 3, 0, 2).reshape(g, p, SSM_BLKW)

    n_pairs = g // 2
    zf = jnp.zeros((n_pairs, SSM_BLKW, p), F32)
    fp = jnp.concatenate([
        jnp.concatenate([f_re_m[0::2], zf, f_im_m[0::2], zf], axis=2),
        jnp.concatenate([zf, f_re_m[1::2], zf, f_im_m[1::2]], axis=2)], axis=1)
    ze = jnp.zeros((n_pairs, p, SSM_BLKW), F32)
    ep = jnp.concatenate([
        jnp.concatenate([e_re[0::2], ze], axis=2),
        jnp.concatenate([ze, e_re[1::2]], axis=2),
        jnp.concatenate([e_im[0::2], ze], axis=2),
        jnp.concatenate([ze, e_im[1::2]], axis=2)], axis=1)
    n_steps = max(1, math.ceil(math.log2(nblk)))
    ap_re, ap_im = power(SSM_TBLK * 2.0 ** jnp.arange(n_steps))
    ap = jnp.stack([ap_re, ap_im], axis=1)
    ap = ap.reshape(n_steps * 2, n_pairs, 2 * p).transpose(1, 0, 2)
    return {
        "m": m.astype(BF16), "f": fp.astype(BF16), "e": ep.astype(BF16), "apow": ap,
        "bb": jnp.concatenate([bb_re.transpose(0, 2, 1), bb_im.transpose(0, 2, 1)], axis=2),
        "a_re": abar_re[:, None, :], "a_im": abar_im[:, None, :],
        "cs": jnp.concatenate([c_re.transpose(0, 2, 1), -c_im.transpose(0, 2, 1)], axis=1),
    }


def _s5_layer_prompt(x, n_seq, norm, w_in, prep, d, wg, wo):
    u, z, ug = _a_in(x, norm, w_in, ROW_TILE, True)
    yg, st = _ssm_scan(ug, prep, n_seq)
    st = st.reshape(SSM_GROUPS // 2, n_seq, 2, 2, SSM_STATE)
    st = st.transpose(2, 1, 0, 3, 4).reshape(2, n_seq, SSM_GROUPS, SSM_STATE)
    return _a_out(yg, u, z, x, d, wg, wo, ROW_TILE, True), st[0], st[1]


def _s5_layer_sample(x, h0r, h0i, norm, w_in, prep, d, wg, wo):
    n = x.shape[0]
    u, z = _a_in(x, norm, w_in, n, False)
    u_g = u.reshape(n, SSM_GROUPS, SSM_GROUP).transpose(1, 0, 2)
    y_g, hr, hi = _ssm_step(u_g, h0r.transpose(1, 0, 2), h0i.transpose(1, 0, 2), prep)
    ys = y_g.transpose(1, 0, 2).reshape(n, D_MODEL)
    return _a_out(ys, u, z, x, d, wg, wo, n, False), hr.transpose(1, 0, 2), hi.transpose(1, 0, 2)


def _kv_kernel(prompt, x_ref, g_ref, w_ref, cos_ref, sin_ref, k_ref, v_ref, *attn_refs):
    h = _rmsnorm(x_ref[...], g_ref[...]).astype(BF16)
    kv = _dot(h, w_ref[...])
    cos2, sin2 = cos_ref[...], sin_ref[...]
    tm = kv.shape[0]
    for hd in range(N_HEADS):
        sl = slice(hd * HEAD_DIM, (hd + 1) * HEAD_DIM)
        k = _rope(kv[:, sl], cos2, sin2)
        v = kv[:, D_MODEL + hd * HEAD_DIM:D_MODEL + (hd + 1) * HEAD_DIM]
        k_ref[:, sl] = k
        v_ref[:, sl] = v
        if prompt:
            kb_ref, vt_ref, mean_ref = attn_refs
            kb_ref[hd] = k.astype(BF16)
            for b in range(tm // MOBA_BLOCK):
                rows = slice(b * MOBA_BLOCK, (b + 1) * MOBA_BLOCK)
                mean_ref[b, :, sl] = jnp.sum(k[rows], axis=0, keepdims=True) * (1.0 / MOBA_BLOCK)
                vt_ref[hd, b, :HEAD_DIM, :] = v[rows].T.astype(BF16)
                vt_ref[hd, b, HEAD_DIM:, :] = jnp.ones((V_ROWS - HEAD_DIM, MOBA_BLOCK), BF16)


def _kv_proj(x, g, w, cos2, sin2, n_seq, tm, prompt):
    t = x.shape[0]
    per_seq = t // n_seq // tm
    tile = pl.BlockSpec((tm, D_MODEL), lambda i: (i, 0))
    rope_tile = pl.BlockSpec((tm, HEAD_DIM), lambda i: (i % per_seq, 0))
    out_specs = [tile, tile]
    out_shape = [jax.ShapeDtypeStruct((t, D_MODEL), F32)] * 2
    if prompt:
        nb = tm // MOBA_BLOCK
        out_specs += [
            pl.BlockSpec((None, N_HEADS, tm, HEAD_DIM), lambda i: (i // per_seq, 0, i % per_seq, 0)),
            pl.BlockSpec((None, N_HEADS, nb, V_ROWS, MOBA_BLOCK), lambda i: (i // per_seq, 0, i % per_seq, 0, 0)),
            pl.BlockSpec((nb, 1, D_MODEL), lambda i: (i, 0, 0))]
        out_shape += [
            jax.ShapeDtypeStruct((n_seq, N_HEADS, t // n_seq, HEAD_DIM), BF16),
            jax.ShapeDtypeStruct((n_seq, N_HEADS, t // n_seq // MOBA_BLOCK, V_ROWS, MOBA_BLOCK), BF16),
            jax.ShapeDtypeStruct((t // MOBA_BLOCK, 1, D_MODEL), F32)]
    return pl.pallas_call(
        functools.partial(_kv_kernel, prompt),
        grid=(t // tm,),
        in_specs=[tile, pl.BlockSpec((1, D_MODEL), lambda i: (0, 0)),
                  pl.BlockSpec((D_MODEL, 2 * D_MODEL), lambda i: (0, 0)), rope_tile, rope_tile],
        out_specs=out_specs,
        out_shape=out_shape,
        compiler_params=_params("parallel"),
        name="kv_proj",
    )(x, g, w, cos2, sin2)


def _q_kernel(blocks_per_seq, x_ref, g_ref, w_ref, cos_ref, sin_ref, kbar_ref, q_ref, bias_ref, z_ref):
    own = pl.program_id(0) % blocks_per_seq
    h = _rmsnorm(x_ref[...], g_ref[...]).astype(BF16)
    qz = _dot(h, w_ref[...])
    z_ref[...] = qz[:, D_MODEL:]
    cos2, sin2 = cos_ref[...], sin_ref[...]
    blk = lax.broadcasted_iota(jnp.int32, (blocks_per_seq, MOBA_BLOCK), 0)
    valid = blk < own
    for hd in range(N_HEADS):
        q = _rope(qz[:, hd * HEAD_DIM:(hd + 1) * HEAD_DIM], cos2, sin2)
        s = _dot_nt(kbar_ref[hd], q, precision=lax.Precision.HIGHEST)
        sel, _ = _top3(s, valid, 0)
        q_ref[hd] = (q * (HEAD_DIM ** -0.5)).astype(BF16)
        bias_ref[hd] = jnp.where(sel > 0.5, 0.0, NEG_INF)


def _q_proj(x, g, w, cos2, sin2, kbar, n_seq):
    t = x.shape[0]
    bps = t // n_seq // MOBA_BLOCK
    tile = pl.BlockSpec((MOBA_BLOCK, D_MODEL), lambda i: (i, 0))
    rope_tile = pl.BlockSpec((MOBA_BLOCK, HEAD_DIM), lambda i: (i % bps, 0))
    return pl.pallas_call(
        functools.partial(_q_kernel, bps),
        grid=(t // MOBA_BLOCK,),
        in_specs=[tile, pl.BlockSpec((1, D_MODEL), lambda i: (0, 0)),
                  pl.BlockSpec((D_MODEL, 2 * D_MODEL), lambda i: (0, 0)), rope_tile, rope_tile,
                  pl.BlockSpec((None, N_HEADS, bps, HEAD_DIM), lambda i: (i // bps, 0, 0, 0))],
        out_specs=[pl.BlockSpec((None, N_HEADS, MOBA_BLOCK, HEAD_DIM), lambda i: (i // bps, 0, i % bps, 0)),
                   pl.BlockSpec((None, N_HEADS, None, bps, MOBA_BLOCK), lambda i: (i // bps, 0, i % bps, 0, 0)),
                   tile],
        out_shape=[jax.ShapeDtypeStruct((n_seq, N_HEADS, t // n_seq, HEAD_DIM), BF16),
                   jax.ShapeDtypeStruct((n_seq, N_HEADS, bps, bps, MOBA_BLOCK), F32),
                   jax.ShapeDtypeStruct((t, D_MODEL), F32)],
        compiler_params=_params("parallel"),
        name="q_proj",
    )(x, g, w, cos2, sin2, kbar)


def _moba_layer_kernel(items, pk_ref, pt_ref, q_ref, bias_ref, k_ref, vt_ref, qs_ref, kn_ref, vn_ref, *rest):
    n_pages = items * N_FETCH
    k_pages, v_pages = rest[:n_pages], rest[n_pages:2 * n_pages]
    o_ref, os_ref, s_scr, acc_scr = rest[2 * n_pages:]
    step = (pl.program_id(0) * pl.num_programs(1) + pl.program_id(1)) * pl.num_programs(2) + pl.program_id(2)
    head0 = (step * items) % N_HEADS
    for a in range(items):
        pages = slice(a * N_FETCH, (a + 1) * N_FETCH)
        os_ref[a] = _decode_attend(head0 + a, qs_ref[a], kn_ref[a], vn_ref[a], k_pages[pages], v_pages[pages])
    _moba_tile(q_ref, bias_ref, k_ref, vt_ref, o_ref, s_scr, acc_scr)


def _moba_tile(q_ref, bias_ref, k_ref, vt_ref, o_ref, s_scr, acc_scr):
    i = pl.program_id(2)
    row = lax.broadcasted_iota(jnp.int32, (MOBA_BLOCK, LANES), 0)
    col = lax.broadcasted_iota(jnp.int32, (MOBA_BLOCK, LANES), 1)
    chains = [(h, c) for h in range(MOBA_HEADS) for c in range(MOBA_BLOCK // LANES)]
    nc = len(chains)

    def scores(h, c, blk):
        off = pl.multiple_of(blk * MOBA_BLOCK, MOBA_BLOCK)
        return _dot_nt(k_ref[h, pl.ds(off, MOBA_BLOCK), :], q_ref[h, c * LANES:(c + 1) * LANES, :])

    ms = []
    for n, (h, c) in enumerate(chains):
        s = jnp.where(row <= col + c * LANES, scores(h, c, i), NEG_INF)
        m = jnp.max(s, axis=0, keepdims=True)
        acc_scr[n] = _dot(vt_ref[h, i], jnp.exp(s - m).astype(BF16))
        ms.append(m)

    def body(t, ms):
        blks = (2 * t, 2 * t + 1)
        mbs = []
        for n, (h, c) in enumerate(chains):
            for u, blk in enumerate(blks):
                s = scores(h, c, blk)
                s_scr[u * nc + n] = s
                mbs.append(jnp.max(s, axis=0, keepdims=True))
        out = []
        for n, (h, c) in enumerate(chains):
            b = [bias_ref[h, pl.ds(blk, 1), :][:, c * LANES:(c + 1) * LANES] for blk in blks]
            m_new = jnp.maximum(ms[n], jnp.maximum(mbs[2 * n] + b[0], mbs[2 * n + 1] + b[1]))
            alpha = jnp.exp(ms[n] - m_new)
            p = [jnp.exp(s_scr[u * nc + n] - (m_new - b[u])).astype(BF16) for u in range(2)]
            acc_scr[n] = alpha * acc_scr[n] + (_dot(vt_ref[h, blks[0]], p[0]) + _dot(vt_ref[h, blks[1]], p[1]))
            out.append(m_new)
        return tuple(out)

    lax.fori_loop(0, (i + 1) // 2, body, tuple(ms))
    for n, (h, c) in enumerate(chains):
        acc = acc_scr[n]
        o = acc[:HEAD_DIM, :] / acc[HEAD_DIM:HEAD_DIM + 1, :]
        o_ref[c * LANES:(c + 1) * LANES, h * HEAD_DIM:(h + 1) * HEAD_DIM] = o.T.astype(o_ref.dtype)


def _moba_layer(q, bias, kb, vt, q_s, k_new, v_new, cache_k3, cache_v3, page_table, picks):
    n_seq, _, seq, _ = q.shape
    n_dec, n_pages = page_table.shape
    nblk = seq // MOBA_BLOCK
    hg = MOBA_HEADS
    n_hg = N_HEADS // hg
    n_chains = hg * (MOBA_BLOCK // LANES)
    steps = n_seq * n_hg * nblk
    items = n_dec * N_HEADS // steps
    assert items * steps == n_dec * N_HEADS and N_HEADS % items == 0
    per_block = MOBA_BLOCK // PAGE_SIZE

    def first_item(n, h, i):
        return ((n * n_hg + h) * nblk + i) * items

    def dec_index(n, h, i, pk, pt):
        e = first_item(n, h, i)
        return (e // N_HEADS, (e % N_HEADS) // items, 0, 0)

    def page_spec(a, f):
        def index(n, h, i, pk, pt):
            e = first_item(n, h, i) + a
            blk = pk[e * MOBA_TOPK + f // per_block]
            return (pt[(e // N_HEADS) * n_pages + blk * per_block + f % per_block], 0, 0)
        return pl.BlockSpec((None, PAGE_SIZE * N_HEADS, HEAD_DIM), index)

    dec_rows = pl.BlockSpec((None, items, 1, HEAD_DIM), dec_index)
    page_specs = [page_spec(a, f) for a in range(items) for f in range(N_FETCH)]
    resident = pl.Buffered(1)
    dec_shape = (n_dec, N_HEADS, 1, HEAD_DIM)
    return pl.pallas_call(
        functools.partial(_moba_layer_kernel, items),
        grid_spec=pltpu.PrefetchScalarGridSpec(
            num_scalar_prefetch=2,
            grid=(n_seq, n_hg, nblk),
            in_specs=[pl.BlockSpec((None, hg, MOBA_BLOCK, HEAD_DIM), lambda n, h, i, pk, pt: (n, h, i, 0)),
                      pl.BlockSpec((None, hg, None, nblk, MOBA_BLOCK), lambda n, h, i, pk, pt: (n, h, i, 0, 0)),
                      pl.BlockSpec((None, hg, seq, HEAD_DIM), lambda n, h, i, pk, pt: (n, h, 0, 0),
                                   pipeline_mode=resident),
                      pl.BlockSpec((None, hg, nblk, V_ROWS, MOBA_BLOCK), lambda n, h, i, pk, pt: (n, h, 0, 0, 0),
                                   pipeline_mode=resident),
                      dec_rows, dec_rows, dec_rows] + page_specs + page_specs,
            out_specs=[pl.BlockSpec((None, MOBA_BLOCK, hg * HEAD_DIM), lambda n, h, i, pk, pt: (n, i, h)),
                       dec_rows],
            scratch_shapes=[pltpu.VMEM((2 * n_chains, MOBA_BLOCK, LANES), F32),
                            pltpu.VMEM((n_chains, V_ROWS, LANES), F32)]),
        out_shape=[jax.ShapeDtypeStruct((n_seq, seq, D_MODEL), BF16),
                   jax.ShapeDtypeStruct(dec_shape, F32)],
        compiler_params=_params("arbitrary", "arbitrary", "arbitrary"),
        name="moba_layer",
    )(picks, page_table.reshape(-1), q, bias, kb, vt,
      q_s.reshape(dec_shape), k_new.reshape(dec_shape), v_new.reshape(dec_shape),
      *([cache_k3] * len(page_specs)), *([cache_v3] * len(page_specs)))


def _b_out_kernel(final, o_ref, z_ref, x_ref, w_ref, g_ref, y_ref):
    v = (o_ref[...].astype(F32) * _silu(z_ref[...])).astype(BF16)
    x = x_ref[...] + _dot(v, w_ref[...])
    y_ref[...] = _rmsnorm(x, g_ref[...]) if final else x


def _b_out(o, z, x, w, g, tm, final):
    t = x.shape[0]
    tile = pl.BlockSpec((tm, D_MODEL), lambda i: (i, 0))
    return pl.pallas_call(
        functools.partial(_b_out_kernel, final),
        grid=(t // tm,),
        in_specs=[tile, tile, tile, pl.BlockSpec((D_MODEL, D_MODEL), lambda i: (0, 0)),
                  pl.BlockSpec((1, D_MODEL), lambda i: (0, 0))],
        out_specs=tile,
        out_shape=jax.ShapeDtypeStruct((t, D_MODEL), F32),
        compiler_params=_params("parallel"),
        name="b_out",
    )(o, z, x, w, g)


PAGES_PER_STEP = 16


def _page_mean_kernel(*refs):
    pages, out_ref = refs[1:1 + PAGES_PER_STEP], refs[-1]
    per_block = MOBA_BLOCK // PAGE_SIZE
    for b in range(PAGES_PER_STEP // per_block):
        tot = None
        for r in range(per_block):
            page = pages[per_block * b + r][...].reshape(PAGE_SIZE, N_HEADS, HEAD_DIM)
            part = jnp.sum(page, axis=0)
            tot = part if tot is None else tot + part
        tot = tot * (1.0 / MOBA_BLOCK)
        for hd in range(N_HEADS):
            out_ref[hd, b:b + 1, :] = tot[hd:hd + 1, :]


def _page_means(cache_k2, page_table):
    n, n_pages = page_table.shape
    steps = n_pages // PAGES_PER_STEP
    nb = PAGES_PER_STEP * PAGE_SIZE // MOBA_BLOCK

    def page_spec(r):
        return pl.BlockSpec((None, PAGE_SIZE * N_HEADS, HEAD_DIM),
                            lambda s, j, pt, r=r: (pt[s * n_pages + j * PAGES_PER_STEP + r], 0, 0))

    return pl.pallas_call(
        _page_mean_kernel,
        grid_spec=pltpu.PrefetchScalarGridSpec(
            num_scalar_prefetch=1,
            grid=(n, steps),
            in_specs=[page_spec(r) for r in range(PAGES_PER_STEP)],
            out_specs=pl.BlockSpec((None, N_HEADS, nb, HEAD_DIM), lambda s, j, pt: (s, 0, j, 0))),
        out_shape=jax.ShapeDtypeStruct((n, N_HEADS, n_pages * PAGE_SIZE // MOBA_BLOCK, HEAD_DIM), F32),
        compiler_params=_params("parallel", "arbitrary"),
        name="page_means",
    )(page_table.reshape(-1), *([cache_k2] * PAGES_PER_STEP))


def _q_sample_kernel(x_ref, g_ref, w_ref, cos_ref, sin_ref, q_ref, z_ref):
    h = _rmsnorm(x_ref[...], g_ref[...]).astype(BF16)
    qz = _dot(h, w_ref[...])
    z_ref[...] = qz[:, D_MODEL:]
    cos2, sin2 = cos_ref[...], sin_ref[...]
    for hd in range(N_HEADS):
        sl = slice(hd * HEAD_DIM, (hd + 1) * HEAD_DIM)
        q_ref[:, sl] = _rope(qz[:, sl], cos2, sin2)


def _q_sample(x, g, w, cos2, sin2):
    n = x.shape[0]
    return pl.pallas_call(
        _q_sample_kernel,
        out_shape=[jax.ShapeDtypeStruct((n, D_MODEL), F32)] * 2,
        compiler_params=pltpu.CompilerParams(vmem_limit_bytes=VMEM_LIMIT),
        name="q_sample",
    )(x, g, w, cos2, sin2)


def _pick_kernel(q_ref, kbar_ref, knew_ref, idx_ref):
    n_past = kbar_ref.shape[1]
    lane_o = lax.broadcasted_iota(jnp.int32, (8, LANES), 1)
    row_o = lax.broadcasted_iota(jnp.int32, (8, LANES), 0)
    out = jnp.zeros((8, LANES), jnp.int32)
    for hd in range(N_HEADS):
        sl = slice(hd * HEAD_DIM, (hd + 1) * HEAD_DIM)
        q8 = jnp.broadcast_to(q_ref[:, sl], (8, HEAD_DIM))
        own_mean = jnp.broadcast_to(knew_ref[:, sl] * (1.0 / MOBA_BLOCK), (8, HEAD_DIM))
        means = jnp.concatenate([kbar_ref[hd], own_mean], axis=0)
        s = _dot_nt(q8, means, precision=lax.Precision.HIGHEST)
        lane = lax.broadcasted_iota(jnp.int32, s.shape, 1)
        _, picks = _top3(s, lane < n_past, 1)
        for r, idx in enumerate(picks):
            out = jnp.where((row_o == hd) & (lane_o == r), idx[0:1, :], out)
    idx_ref[...] = out


def _pick_blocks(q, kbar, knew):
    n, _, n_past, _ = kbar.shape
    return pl.pallas_call(
        _pick_kernel,
        grid=(n,),
        in_specs=[pl.BlockSpec((None, 1, D_MODEL), lambda s: (s, 0, 0)),
                  pl.BlockSpec((None, N_HEADS, n_past, HEAD_DIM), lambda s: (s, 0, 0, 0)),
                  pl.BlockSpec((None, 1, D_MODEL), lambda s: (s, 0, 0))],
        out_specs=pl.BlockSpec((None, 8, LANES), lambda s: (s, 0, 0)),
        out_shape=jax.ShapeDtypeStruct((n, 8, LANES), jnp.int32),
        compiler_params=_params("parallel"),
        name="pick_blocks",
    )(q.reshape(n, 1, D_MODEL), kbar, knew.reshape(n, 1, D_MODEL))


N_FETCH = MOBA_TOPK * (MOBA_BLOCK // PAGE_SIZE)


def _decode_attend(head, q, k_new, v_new, k_pages, v_pages):
    scale = HEAD_DIM ** -0.5
    q8 = jnp.broadcast_to(q, (8, HEAD_DIM))
    head_rows = pl.ds(head, PAGE_SIZE, stride=N_HEADS)
    kc = jnp.concatenate([r[head_rows, :] for r in k_pages], axis=0)
    vc = jnp.concatenate([r[head_rows, :] for r in v_pages], axis=0)
    s = _dot_nt(q8, kc)[0:1, :] * scale
    s_new = jnp.sum(q * k_new, axis=1, keepdims=True) * scale
    m = jnp.maximum(jnp.max(s, axis=1, keepdims=True), s_new)
    p = jnp.exp(s - m)
    p_new = jnp.exp(s_new - m)
    l = jnp.sum(p, axis=1, keepdims=True) + p_new
    pv = _dot(jnp.broadcast_to(p, (8, p.shape[1])), vc)[0:1, :]
    return (pv + p_new * v_new) / l


def _rope_tables(pos):
    half = HEAD_DIM // 2
    inv = ROPE_THETA ** (-jnp.arange(half, dtype=F32) / half)
    ang = pos.astype(F32)[:, None] * inv[None, :]
    cos, sin = jnp.cos(ang), jnp.sin(ang)
    return jnp.concatenate([cos, cos], axis=1), jnp.concatenate([-sin, sin], axis=1)


def kernel(x_prompt, x_sample, state_ssm_re, state_ssm_im, cache_k, cache_v, page_table, a_norm, a_w_in, a_lam_re, a_lam_im, a_log_dt, a_b_re, a_b_im, a_c_re, a_c_im, a_d, a_w_glu, a_w_out, kv_norm, w_kv, b_norm, b_w_in, b_w_out, final_norm):
    n_seq, seq, _ = x_prompt.shape
    n_dec = x_sample.shape[0]
    n_pool = cache_k.shape[0]
    past_len = page_table.shape[1] * PAGE_SIZE
    assert x_sample.shape[1] == 1 and seq % ROW_TILE == 0 and ROW_TILE % MOBA_BLOCK == 0
    assert past_len % (PAGES_PER_STEP * PAGE_SIZE) == 0 and past_len // MOBA_BLOCK >= MOBA_TOPK
    assert seq // MOBA_BLOCK <= LANES and (seq // MOBA_BLOCK) % 8 == 0
    n_a, n_b = a_norm.shape[0], b_norm.shape[0]

    xp = x_prompt.reshape(n_seq * seq, D_MODEL)
    xs = x_sample.reshape(n_dec, D_MODEL)
    cos_p, sin_p = _rope_tables(jnp.arange(seq, dtype=jnp.int32))
    cos_s, sin_s = _rope_tables(jnp.full((n_dec,), past_len, jnp.int32))

    st_p_re, st_p_im, st_s_re, st_s_im = [], [], [], []
    for l in range(n_a):
        prep = _ssm_prep(a_lam_re[l], a_lam_im[l], a_log_dt[l], a_b_re[l], a_b_im[l], a_c_re[l], a_c_im[l],
                         seq // SSM_TBLK)
        norm, d = a_norm[l][None], a_d[l][None]
        w_in, wg, wo = a_w_in[l].astype(BF16), a_w_glu[l].astype(BF16), a_w_out[l].astype(BF16)
        xp, hr, hi = _s5_layer_prompt(xp, n_seq, norm, w_in, prep, d, wg, wo)
        st_p_re.append(hr)
        st_p_im.append(hi)
        xs, hr, hi = _s5_layer_sample(xs, state_ssm_re[l], state_ssm_im[l], norm, w_in, prep, d, wg, wo)
        st_s_re.append(hr)
        st_s_im.append(hi)

    w_kv_b = w_kv.astype(BF16)
    k_p, v_p, kb, vt, kbar_p = _kv_proj(xp, kv_norm[None], w_kv_b, cos_p, sin_p, n_seq, ROW_TILE, True)
    k_s, v_s = _kv_proj(xs, kv_norm[None], w_kv_b, cos_s, sin_s, 1, n_dec, False)
    nblk = seq // MOBA_BLOCK
    kbar_p = kbar_p.reshape(n_seq, nblk, N_HEADS, HEAD_DIM).transpose(0, 2, 1, 3)
    cache_k3 = cache_k.reshape(n_pool, PAGE_SIZE * N_HEADS, HEAD_DIM)
    cache_v3 = cache_v.reshape(n_pool, PAGE_SIZE * N_HEADS, HEAD_DIM)
    kbar_s = _page_means(cache_k3, page_table)

    for j in range(n_b):
        w_in, wo = b_w_in[j].astype(BF16), b_w_out[j].astype(BF16)
        final = j == n_b - 1
        q, bias, z = _q_proj(xp, b_norm[j][None], w_in, cos_p, sin_p, kbar_p, n_seq)
        q_s, z_s = _q_sample(xs, b_norm[j][None], w_in, cos_s, sin_s)
        picks = _pick_blocks(q_s, kbar_s, k_s)[:, :, :MOBA_TOPK].reshape(-1)
        o, o_s = _moba_layer(q, bias, kb, vt, q_s, k_s, v_s, cache_k3, cache_v3, page_table, picks)
        xp = _b_out(o.reshape(n_seq * seq, D_MODEL), z, xp, wo, final_norm[None], ROW_TILE, final)
        xs = _b_out(o_s.reshape(n_dec, D_MODEL), z_s, xs, wo, final_norm[None], n_dec, final)

    return (xp.reshape(n_seq, seq, D_MODEL), xs.reshape(n_dec, 1, D_MODEL),
            jnp.stack(st_p_re), jnp.stack(st_p_im), jnp.stack(st_s_re), jnp.stack(st_s_im),
            k_p.reshape(n_seq, seq, N_HEADS, HEAD_DIM), v_p.reshape(n_seq, seq, N_HEADS, HEAD_DIM),
            k_s.reshape(n_dec, 1, N_HEADS, HEAD_DIM), v_s.reshape(n_dec, 1, N_HEADS, HEAD_DIM))
```

```python
import functools
import math

import jax
import jax.numpy as jnp
from jax import lax
from jax.experimental import pallas as pl
from jax.experimental.pallas import tpu as pltpu

D_MODEL = 1024
SSM_GROUP = 16
SSM_GROUPS = D_MODEL // SSM_GROUP
SSM_STATE = 64
SSM_TBLK = 16
SSM_BLKW = SSM_TBLK * SSM_GROUP
HEAD_DIM = 128
N_HEADS = D_MODEL // HEAD_DIM
MOBA_BLOCK = 256
MOBA_TOPK = 3
PAGE_SIZE = 128
ROPE_THETA = 10000.0
RMS_EPS = 1e-6
NEG_INF = -1e30
LANES = 128
VMEM_LIMIT = 48 * 1024 * 1024
ROW_TILE = 512
MOBA_HEADS = 4
V_ROWS = HEAD_DIM + 16

F32 = jnp.float32
BF16 = jnp.bfloat16


def _params(*sem):
    return pltpu.CompilerParams(dimension_semantics=sem, vmem_limit_bytes=VMEM_LIMIT)


def _rmsnorm(x, g):
    r = lax.rsqrt(jnp.mean(x * x, axis=-1, keepdims=True) + RMS_EPS)
    return x * r * g


def _dot(a, b):
    return jnp.dot(a, b, preferred_element_type=F32)


def _dot_nt(a, b, precision=None):
    return lax.dot_general(a, b, (((1,), (1,)), ((), ())), precision=precision,
                           preferred_element_type=F32)


def _weight_spec(shape):
    return pl.BlockSpec(shape, lambda *_: (0,) * len(shape), pipeline_mode=pl.Buffered(1))


def _cast_once(*pairs):
    @pl.when(pl.program_id(0) == 0)
    def _():
        for w_ref, w_scr in pairs:
            w_scr[...] = w_ref[...].astype(BF16)


def _rope(x, cos2, sin2):
    return x * cos2 + pltpu.roll(x, HEAD_DIM // 2, 1) * sin2


def _silu(z):
    return z * jax.nn.sigmoid(z)


def _top3(s, valid, axis):
    n_lanes = s.shape[axis]
    lane = lax.broadcasted_iota(jnp.int32, s.shape, axis)
    sm = jnp.where(valid, s, NEG_INF)
    sel = jnp.zeros(s.shape, F32)
    picks = []
    for _ in range(MOBA_TOPK):
        m = jnp.max(sm, axis=axis, keepdims=True)
        idx = jnp.min(jnp.where(sm == m, lane, n_lanes), axis=axis, keepdims=True)
        hit = lane == idx
        sel = jnp.where(hit, jnp.where(valid, 1.0, sel), sel)
        sm = jnp.where(hit, NEG_INF, sm)
        picks.append(idx)
    return sel, picks


GROUPS_PER_TILE = LANES // SSM_GROUP
STEP_TILES = SSM_BLKW // LANES


def _slot_transpose(xs, slot):
    xs = list(xs)
    step = GROUPS_PER_TILE // 2
    while step:
        low = (slot & step) == 0
        for i in range(GROUPS_PER_TILE):
            if i & step == 0:
                a, b = xs[i], xs[i + step]
                xs[i] = jnp.where(low, a, pltpu.roll(b, step * SSM_GROUP, 1))
                xs[i + step] = jnp.where(low, pltpu.roll(a, LANES - step * SSM_GROUP, 1), b)
        step //= 2
    return xs


def _to_blocked(u, tiles_ref, ug_ref):
    nb = ug_ref.shape[0]
    slot = lax.broadcasted_iota(jnp.int32, (nb, LANES), 1) // SSM_GROUP
    for tile in range(D_MODEL // LANES):
        tiles_ref[tile] = u[:, tile * LANES:(tile + 1) * LANES]
    for tile in range(D_MODEL // LANES):
        for hh in range(STEP_TILES):
            steps = [tiles_ref[tile, pl.ds(hh * GROUPS_PER_TILE + rp, nb, stride=SSM_TBLK), :]
                     for rp in range(GROUPS_PER_TILE)]
            for gp, piece in enumerate(_slot_transpose(steps, slot)):
                col = (tile * GROUPS_PER_TILE + gp) * SSM_BLKW + hh * LANES
                ug_ref[:, col:col + LANES] = piece.astype(ug_ref.dtype)


def _from_blocked(yg_ref, tiles_ref):
    nb = yg_ref.shape[0]
    slot = lax.broadcasted_iota(jnp.int32, (nb, LANES), 1) // SSM_GROUP
    for tile in range(D_MODEL // LANES):
        for hh in range(STEP_TILES):
            cols = [(tile * GROUPS_PER_TILE + gp) * SSM_BLKW + hh * LANES for gp in range(GROUPS_PER_TILE)]
            for rp, piece in enumerate(_slot_transpose([yg_ref[:, c:c + LANES] for c in cols], slot)):
                tiles_ref[tile, pl.ds(hh * GROUPS_PER_TILE + rp, nb, stride=SSM_TBLK), :] = piece
    return jnp.concatenate([tiles_ref[tile] for tile in range(D_MODEL // LANES)], axis=1)


def _a_in_kernel(blocked, x_ref, g_ref, w_ref, u_ref, z_ref, *rest):
    w_scr = rest[-1]
    _cast_once((w_ref, w_scr))
    h = _rmsnorm(x_ref[...], g_ref[...]).astype(BF16)
    uz = _dot(h, w_scr[...])
    u_ref[...] = uz[:, :D_MODEL]
    z_ref[...] = uz[:, D_MODEL:]
    if blocked:
        ug_ref, tiles_ref = rest[:2]
        _to_blocked(uz[:, :D_MODEL], tiles_ref, ug_ref)


def _a_in(x, g, w, tm, blocked):
    t = x.shape[0]
    tile = pl.BlockSpec((tm, D_MODEL), lambda i: (i, 0))
    out_specs, out_shape = [tile, tile], [jax.ShapeDtypeStruct((t, D_MODEL), F32)] * 2
    if blocked:
        out_specs.append(pl.BlockSpec((tm // SSM_TBLK, SSM_GROUPS * SSM_BLKW), lambda i: (i, 0)))
        out_shape.append(jax.ShapeDtypeStruct((t // SSM_TBLK, SSM_GROUPS * SSM_BLKW), BF16))
    return pl.pallas_call(
        functools.partial(_a_in_kernel, blocked),
        grid=(t // tm,),
        in_specs=[tile, pl.BlockSpec((1, D_MODEL), lambda i: (0, 0)), _weight_spec((D_MODEL, 2 * D_MODEL))],
        out_specs=out_specs,
        out_shape=out_shape,
        scratch_shapes=([pltpu.VMEM((D_MODEL // LANES, tm, LANES), F32)] if blocked else [])
        + [pltpu.VMEM((D_MODEL, 2 * D_MODEL), BF16)],
        compiler_params=_params("arbitrary"),
        name="a_in",
    )(x, g, w)


def _ssm_scan_kernel(n_seq, ug_ref, m_ref, f_ref, e_ref, ap_ref, y_ref, st_ref):
    rows = ug_ref.shape[0]
    nblk = rows // n_seq
    ug = ug_ref[...]
    s = _dot(ug, f_ref[...])
    row = lax.broadcasted_iota(jnp.int32, (nblk, LANES), 0)
    hp = []
    for q in range(n_seq):
        re = s[q * nblk:(q + 1) * nblk, :LANES]
        im = s[q * nblk:(q + 1) * nblk, LANES:]
        step, k = 1, 0
        while step < nblk:
            ar = ap_ref[2 * k:2 * k + 1, :]
            ai = ap_ref[2 * k + 1:2 * k + 2, :]
            sr = jnp.where(row >= step, pltpu.roll(re, step, 0), 0.0)
            si = jnp.where(row >= step, pltpu.roll(im, step, 0), 0.0)
            re, im = re + ar * sr - ai * si, im + ar * si + ai * sr
            step, k = step * 2, k + 1
        st_ref[2 * q:2 * q + 1, :] = re[nblk - 1:nblk, :]
        st_ref[2 * q + 1:2 * q + 2, :] = im[nblk - 1:nblk, :]
        pr = jnp.where(row >= 1, pltpu.roll(re, 1, 0), 0.0)
        pi = jnp.where(row >= 1, pltpu.roll(im, 1, 0), 0.0)
        hp.append(jnp.concatenate([pr, pi], axis=1))
    hprev = jnp.concatenate(hp, axis=0).astype(BF16)
    carry = _dot(hprev, e_ref[...])
    y_ref[:, :SSM_BLKW] = _dot(ug[:, :SSM_BLKW], m_ref[0]) + carry[:, :SSM_BLKW]
    y_ref[:, SSM_BLKW:] = _dot(ug[:, SSM_BLKW:], m_ref[1]) + carry[:, SSM_BLKW:]


def _ssm_scan(ug, prep, n_seq):
    rows = ug.shape[0]
    n_pairs = SSM_GROUPS // 2
    n_pow = prep["apow"].shape[1]
    return pl.pallas_call(
        functools.partial(_ssm_scan_kernel, n_seq),
        grid=(n_pairs,),
        in_specs=[pl.BlockSpec((rows, 2 * SSM_BLKW), lambda p: (0, p)),
                  pl.BlockSpec((2, SSM_BLKW, SSM_BLKW), lambda p: (p, 0, 0)),
                  pl.BlockSpec((None, 2 * SSM_BLKW, 2 * LANES), lambda p: (p, 0, 0)),
                  pl.BlockSpec((None, 2 * LANES, 2 * SSM_BLKW), lambda p: (p, 0, 0)),
                  pl.BlockSpec((None, n_pow, LANES), lambda p: (p, 0, 0))],
        out_specs=[pl.BlockSpec((rows, 2 * SSM_BLKW), lambda p: (0, p)),
                   pl.BlockSpec((None, 2 * n_seq, LANES), lambda p: (p, 0, 0))],
        out_shape=[jax.ShapeDtypeStruct((rows, SSM_GROUPS * SSM_BLKW), F32),
                   jax.ShapeDtypeStruct((n_pairs, 2 * n_seq, LANES), F32)],
        compiler_params=_params("parallel"),
        name="ssm_scan",
    )(ug, prep["m"], prep["f"], prep["e"], prep["apow"])


def _ssm_step_kernel(u_ref, hr_ref, hi_ref, bb_ref, ar_ref, ai_ref, cs_ref, y_ref, or_ref, oi_ref):
    bu = lax.dot_general(u_ref[...], bb_ref[...], (((2,), (1,)), ((0,), (0,))),
                         preferred_element_type=F32)
    h0r, h0i = hr_ref[...], hi_ref[...]
    ar, ai = ar_ref[...], ai_ref[...]
    hr = bu[:, :, :SSM_STATE] + ar * h0r - ai * h0i
    hi = bu[:, :, SSM_STATE:] + ar * h0i + ai * h0r
    or_ref[...] = hr
    oi_ref[...] = hi
    y_ref[...] = lax.dot_general(jnp.concatenate([hr, hi], axis=2), cs_ref[...],
                                 (((2,), (1,)), ((0,), (0,))), preferred_element_type=F32)


def _ssm_step(u_g, h0r, h0i, prep):
    g, n, _ = u_g.shape
    return pl.pallas_call(
        _ssm_step_kernel,
        out_shape=[jax.ShapeDtypeStruct((g, n, SSM_GROUP), F32),
                   jax.ShapeDtypeStruct((g, n, SSM_STATE), F32),
                   jax.ShapeDtypeStruct((g, n, SSM_STATE), F32)],
        compiler_params=pltpu.CompilerParams(vmem_limit_bytes=VMEM_LIMIT),
        name="ssm_step",
    )(u_g, h0r, h0i, prep["bb"], prep["a_re"], prep["a_im"], prep["cs"])


def _a_out_kernel(blocked, ys_ref, u_ref, z_ref, x_ref, d_ref, wg_ref, wo_ref, o_ref, *scratch):
    wg_scr, wo_scr = scratch[-2:]
    _cast_once((wg_ref, wg_scr), (wo_ref, wo_scr))
    ys = _from_blocked(ys_ref, scratch[0]) if blocked else ys_ref[...]
    y = ys + d_ref[...] * u_ref[...]
    g = jax.nn.gelu(y)
    y2 = g * jax.nn.sigmoid(_dot(g.astype(BF16), wg_scr[...]))
    v = (y2 * _silu(z_ref[...])).astype(BF16)
    o_ref[...] = x_ref[...] + _dot(v, wo_scr[...])


def _a_out(ys, u, z, x, d, wg, wo, tm, blocked):
    t = x.shape[0]
    tile = pl.BlockSpec((tm, D_MODEL), lambda i: (i, 0))
    full = _weight_spec((D_MODEL, D_MODEL))
    ys_spec = pl.BlockSpec((tm // SSM_TBLK, SSM_GROUPS * SSM_BLKW), lambda i: (i, 0)) if blocked else tile
    return pl.pallas_call(
        functools.partial(_a_out_kernel, blocked),
        grid=(t // tm,),
        in_specs=[ys_spec, tile, tile, tile, pl.BlockSpec((1, D_MODEL), lambda i: (0, 0)), full, full],
        out_specs=tile,
        out_shape=jax.ShapeDtypeStruct((t, D_MODEL), F32),
        scratch_shapes=([pltpu.VMEM((D_MODEL // LANES, tm, LANES), F32)] if blocked else [])
        + [pltpu.VMEM((D_MODEL, D_MODEL), BF16)] * 2,
        compiler_params=_params("arbitrary"),
        name="a_out",
    )(ys, u, z, x, d, wg, wo)


def _ssm_prep(lam_re, lam_im, log_dt, b_re, b_im, c_re, c_im, nblk):
    hi = lax.Precision.HIGHEST
    g, p = lam_re.shape
    dt = jnp.exp(log_dt)[:, None]

    def power(n):
        n = jnp.asarray(n, F32).reshape((-1, 1, 1))
        mag = jnp.exp(n * (lam_re * dt)[None])
        ang = n * (lam_im * dt)[None]
        return mag * jnp.cos(ang), mag * jnp.sin(ang)

    pw_re, pw_im = power(jnp.arange(SSM_TBLK + 1))
    abar_re, abar_im = pw_re[1], pw_im[1]
    den = lam_re * lam_re + lam_im * lam_im
    f_re = ((abar_re - 1.0) * lam_re + abar_im * lam_im) / den
    f_im = (abar_im * lam_re - (abar_re - 1.0) * lam_im) / den
    bb_re = f_re[..., None] * b_re - f_im[..., None] * b_im
    bb_im = f_re[..., None] * b_im + f_im[..., None] * b_re
    tr, ti = pw_re[:SSM_TBLK, :, :, None], pw_im[:SSM_TBLK, :, :, None]
    w_re = tr * bb_re[None] - ti * bb_im[None]
    w_im = tr * bb_im[None] + ti * bb_re[None]
    f_re_m = w_re[::-1].transpose(1, 0, 3, 2).reshape(g, SSM_BLKW, p)
    f_im_m = w_im[::-1].transpose(1, 0, 3, 2).reshape(g, SSM_BLKW, p)
    kern = (jnp.einsum('tgpi,gop->gtio', w_re, c_re, precision=hi)
            - jnp.einsum('tgpi,gop->gtio', w_im, c_im, precision=hi))
    lag = jnp.arange(SSM_TBLK)[None, :] - jnp.arange(SSM_TBLK)[:, None]
    m = jnp.where((lag >= 0)[None, :, :, None, None], kern[:, jnp.clip(lag, 0, SSM_TBLK - 1)], 0.0)
    m = m.transpose(0, 1, 3, 2, 4).reshape(g, SSM_BLKW, SSM_BLKW)
    er, ei = pw_re[1:, :, None, :], pw_im[1:, :, None, :]
    ca_re = c_re[None] * er - c_im[None] * ei
    ca_im = c_re[None] * ei + c_im[None] * er
    e_re = ca_re.transpose(1, 3, 0, 2).reshape(g, p, SSM_BLKW)
    e_im = -ca_im.transpose(1, 3, 0, 2).reshape(g, p, SSM_BLKW)

    n_pairs = g // 2
    zf = jnp.zeros((n_pairs, SSM_BLKW, p), F32)
    fp = jnp.concatenate([
        jnp.concatenate([f_re_m[0::2], zf, f_im_m[0::2], zf], axis=2),
        jnp.concatenate([zf, f_re_m[1::2], zf, f_im_m[1::2]], axis=2)], axis=1)
    ze = jnp.zeros((n_pairs, p, SSM_BLKW), F32)
    ep = jnp.concatenate([
        jnp.concatenate([e_re[0::2], ze], axis=2),
        jnp.concatenate([ze, e_re[1::2]], axis=2),
        jnp.concatenate([e_im[0::2], ze], axis=2),
        jnp.concatenate([ze, e_im[1::2]], axis=2)], axis=1)
    n_steps = max(1, math.ceil(math.log2(nblk)))
    ap_re, ap_im = power(SSM_TBLK * 2.0 ** jnp.arange(n_steps))
    ap = jnp.stack([ap_re, ap_im], axis=1)
    ap = ap.reshape(n_steps * 2, n_pairs, 2 * p).transpose(1, 0, 2)
    return {
        "m": m.astype(BF16), "f": fp.astype(BF16), "e": ep.astype(BF16), "apow": ap,
        "bb": jnp.concatenate([bb_re.transpose(0, 2, 1), bb_im.transpose(0, 2, 1)], axis=2),
        "a_re": abar_re[:, None, :], "a_im": abar_im[:, None, :],
        "cs": jnp.concatenate([c_re.transpose(0, 2, 1), -c_im.transpose(0, 2, 1)], axis=1),
    }


def _s5_layer_prompt(x, n_seq, norm, w_in, prep, d, wg, wo):
    u, z, ug = _a_in(x, norm, w_in, ROW_TILE, True)
    yg, st = _ssm_scan(ug, prep, n_seq)
    st = st.reshape(SSM_GROUPS // 2, n_seq, 2, 2, SSM_STATE)
    st = st.transpose(2, 1, 0, 3, 4).reshape(2, n_seq, SSM_GROUPS, SSM_STATE)
    return _a_out(yg, u, z, x, d, wg, wo, ROW_TILE, True), st[0], st[1]


def _s5_layer_sample(x, h0r, h0i, norm, w_in, prep, d, wg, wo):
    n = x.shape[0]
    u, z = _a_in(x, norm, w_in, n, False)
    u_g = u.reshape(n, SSM_GROUPS, SSM_GROUP).transpose(1, 0, 2)
    y_g, hr, hi = _ssm_step(u_g, h0r.transpose(1, 0, 2), h0i.transpose(1, 0, 2), prep)
    ys = y_g.transpose(1, 0, 2).reshape(n, D_MODEL)
    return _a_out(ys, u, z, x, d, wg, wo, n, False), hr.transpose(1, 0, 2), hi.transpose(1, 0, 2)


def _kv_kernel(prompt, x_ref, g_ref, w_ref, cos_ref, sin_ref, k_ref, v_ref, *rest):
    w_scr = rest[-1]
    _cast_once((w_ref, w_scr))
    h = _rmsnorm(x_ref[...], g_ref[...]).astype(BF16)
    kv = _dot(h, w_scr[...])
    cos2, sin2 = cos_ref[...], sin_ref[...]
    tm = kv.shape[0]
    for hd in range(N_HEADS):
        sl = slice(hd * HEAD_DIM, (hd + 1) * HEAD_DIM)
        k = _rope(kv[:, sl], cos2, sin2)
        v = kv[:, D_MODEL + hd * HEAD_DIM:D_MODEL + (hd + 1) * HEAD_DIM]
        k_ref[:, sl] = k
        v_ref[:, sl] = v
        if prompt:
            kb_ref, vt_ref, mean_ref = rest[:3]
            kb_ref[hd] = k.astype(BF16)
            for b in range(tm // MOBA_BLOCK):
                rows = slice(b * MOBA_BLOCK, (b + 1) * MOBA_BLOCK)
                mean_ref[b, :, sl] = jnp.sum(k[rows], axis=0, keepdims=True) * (1.0 / MOBA_BLOCK)
                vt_ref[hd, b, :HEAD_DIM, :] = v[rows].T.astype(BF16)
                vt_ref[hd, b, HEAD_DIM:, :] = jnp.ones((V_ROWS - HEAD_DIM, MOBA_BLOCK), BF16)


def _kv_proj(x, g, w, cos2, sin2, n_seq, tm, prompt):
    t = x.shape[0]
    per_seq = t // n_seq // tm
    tile = pl.BlockSpec((tm, D_MODEL), lambda i: (i, 0))
    rope_tile = pl.BlockSpec((tm, HEAD_DIM), lambda i: (i % per_seq, 0))
    out_specs = [tile, tile]
    out_shape = [jax.ShapeDtypeStruct((t, D_MODEL), F32)] * 2
    if prompt:
        nb = tm // MOBA_BLOCK
        out_specs += [
            pl.BlockSpec((None, N_HEADS, tm, HEAD_DIM), lambda i: (i // per_seq, 0, i % per_seq, 0)),
            pl.BlockSpec((None, N_HEADS, nb, V_ROWS, MOBA_BLOCK), lambda i: (i // per_seq, 0, i % per_seq, 0, 0)),
            pl.BlockSpec((nb, 1, D_MODEL), lambda i: (i, 0, 0))]
        out_shape += [
            jax.ShapeDtypeStruct((n_seq, N_HEADS, t // n_seq, HEAD_DIM), BF16),
            jax.ShapeDtypeStruct((n_seq, N_HEADS, t // n_seq // MOBA_BLOCK, V_ROWS, MOBA_BLOCK), BF16),
            jax.ShapeDtypeStruct((t // MOBA_BLOCK, 1, D_MODEL), F32)]
    return pl.pallas_call(
        functools.partial(_kv_kernel, prompt),
        grid=(t // tm,),
        in_specs=[tile, pl.BlockSpec((1, D_MODEL), lambda i: (0, 0)), _weight_spec((D_MODEL, 2 * D_MODEL)),
                  rope_tile, rope_tile],
        out_specs=out_specs,
        out_shape=out_shape,
        scratch_shapes=[pltpu.VMEM((D_MODEL, 2 * D_MODEL), BF16)],
        compiler_params=_params("arbitrary"),
        name="kv_proj",
    )(x, g, w, cos2, sin2)


def _q_kernel(blocks_per_seq, x_ref, g_ref, w_ref, cos_ref, sin_ref, kbar_ref, q_ref, bias_ref, z_ref, w_scr):
    _cast_once((w_ref, w_scr))
    tm = x_ref.shape[0]
    per_tile = tm // MOBA_BLOCK
    first = (pl.program_id(0) * per_tile) % blocks_per_seq
    h = _rmsnorm(x_ref[...], g_ref[...]).astype(BF16)
    qz = _dot(h, w_scr[...])
    z_ref[...] = qz[:, D_MODEL:]
    cos2, sin2 = cos_ref[...], sin_ref[...]
    blk = lax.broadcasted_iota(jnp.int32, (blocks_per_seq, MOBA_BLOCK), 0)
    for hd in range(N_HEADS):
        q = _rope(qz[:, hd * HEAD_DIM:(hd + 1) * HEAD_DIM], cos2, sin2)
        q_ref[hd] = (q * (HEAD_DIM ** -0.5)).astype(BF16)
        for b in range(per_tile):
            rows = slice(b * MOBA_BLOCK, (b + 1) * MOBA_BLOCK)
            s = _dot_nt(kbar_ref[hd], q[rows], precision=lax.Precision.HIGHEST)
            sel, _ = _top3(s, blk < first + b, 0)
            bias_ref[hd, b] = jnp.where(sel > 0.5, 0.0, NEG_INF)


def _q_proj(x, g, w, cos2, sin2, kbar, n_seq):
    t = x.shape[0]
    tm = ROW_TILE
    bps = t // n_seq // MOBA_BLOCK
    per_tile = tm // MOBA_BLOCK
    tps = bps // per_tile
    tile = pl.BlockSpec((tm, D_MODEL), lambda i: (i, 0))
    rope_tile = pl.BlockSpec((tm, HEAD_DIM), lambda i: (i % tps, 0))
    return pl.pallas_call(
        functools.partial(_q_kernel, bps),
        grid=(t // tm,),
        in_specs=[tile, pl.BlockSpec((1, D_MODEL), lambda i: (0, 0)), _weight_spec((D_MODEL, 2 * D_MODEL)),
                  rope_tile, rope_tile,
                  pl.BlockSpec((None, N_HEADS, bps, HEAD_DIM), lambda i: (i // tps, 0, 0, 0))],
        out_specs=[pl.BlockSpec((None, N_HEADS, tm, HEAD_DIM), lambda i: (i // tps, 0, i % tps, 0)),
                   pl.BlockSpec((None, N_HEADS, per_tile, bps, MOBA_BLOCK), lambda i: (i // tps, 0, i % tps, 0, 0)),
                   tile],
        out_shape=[jax.ShapeDtypeStruct((n_seq, N_HEADS, t // n_seq, HEAD_DIM), BF16),
                   jax.ShapeDtypeStruct((n_seq, N_HEADS, bps, bps, MOBA_BLOCK), F32),
                   jax.ShapeDtypeStruct((t, D_MODEL), F32)],
        scratch_shapes=[pltpu.VMEM((D_MODEL, 2 * D_MODEL), BF16)],
        compiler_params=_params("arbitrary"),
        name="q_proj",
    )(x, g, w, cos2, sin2, kbar)


def _moba_layer_kernel(items, pk_ref, pt_ref, q_ref, bias_ref, k_ref, vt_ref, qs_ref, kn_ref, vn_ref, *rest):
    n_pages = items * N_FETCH
    k_pages, v_pages = rest[:n_pages], rest[n_pages:2 * n_pages]
    o_ref, os_ref, s_a, s_b, acc_scr = rest[2 * n_pages:]
    step = (pl.program_id(0) * pl.num_programs(1) + pl.program_id(1)) * pl.num_programs(2) + pl.program_id(2)
    head0 = (step * items) % N_HEADS
    for a in range(items):
        pages = slice(a * N_FETCH, (a + 1) * N_FETCH)
        os_ref[a] = _decode_attend(head0 + a, qs_ref[a], kn_ref[a], vn_ref[a], k_pages[pages], v_pages[pages])
    _moba_tile(q_ref, bias_ref, k_ref, vt_ref, o_ref, s_a, s_b, acc_scr)


def _moba_tile(q_ref, bias_ref, k_ref, vt_ref, o_ref, s_a, s_b, acc_scr):
    i = pl.program_id(2)
    nblk = bias_ref.shape[1]
    row = lax.broadcasted_iota(jnp.int32, (MOBA_BLOCK, LANES), 0)
    col = lax.broadcasted_iota(jnp.int32, (MOBA_BLOCK, LANES), 1)
    chains = [(h, c) for h in range(MOBA_HEADS) for c in range(MOBA_BLOCK // LANES)]
    nc = len(chains)
    trips = (i + 1) // 2
    last_trip = nblk // 2 - 1

    def scores(h, c, blk):
        off = pl.multiple_of(blk * MOBA_BLOCK, MOBA_BLOCK)
        return _dot_nt(k_ref[h, pl.ds(off, MOBA_BLOCK), :], q_ref[h, c * LANES:(c + 1) * LANES, :])

    def stage_scores(t, buf):
        mbs = []
        for n, (h, c) in enumerate(chains):
            for u in range(2):
                s = scores(h, c, 2 * t + u)
                buf[u * nc + n] = s
                mbs.append(jnp.max(s, axis=0, keepdims=True))
        return tuple(mbs)

    def consume(t, buf, ms, mbs):
        out = []
        for n, (h, c) in enumerate(chains):
            b = [bias_ref[h, pl.ds(2 * t + u, 1), :][:, c * LANES:(c + 1) * LANES] for u in range(2)]
            m_new = jnp.maximum(ms[n], jnp.maximum(mbs[2 * n] + b[0], mbs[2 * n + 1] + b[1]))
            alpha = jnp.exp(ms[n] - m_new)
            p = [jnp.exp(buf[u * nc + n] - (m_new - b[u])).astype(BF16) for u in range(2)]
            acc_scr[n] = alpha * acc_scr[n] + (_dot(vt_ref[h, 2 * t], p[0]) + _dot(vt_ref[h, 2 * t + 1], p[1]))
            out.append(m_new)
        return tuple(out)

    ms = []
    for n, (h, c) in enumerate(chains):
        s = jnp.where(row <= col + c * LANES, scores(h, c, i), NEG_INF)
        m = jnp.max(s, axis=0, keepdims=True)
        acc_scr[n] = _dot(vt_ref[h, i], jnp.exp(s - m).astype(BF16))
        ms.append(m)
    mbs_a = stage_scores(0, s_a)

    def body(tt, carry):
        ms, mbs_a = carry
        t0 = 2 * tt
        mbs_b = stage_scores(t0 + 1, s_b)
        ms = consume(t0, s_a, ms, mbs_a)
        mbs_a = stage_scores(jnp.minimum(t0 + 2, last_trip), s_a)
        ms = consume(t0 + 1, s_b, ms, mbs_b)
        return ms, mbs_a

    lax.fori_loop(0, (trips + 1) // 2, body, (tuple(ms), mbs_a))
    for n, (h, c) in enumerate(chains):
        acc = acc_scr[n]
        o = acc[:HEAD_DIM, :] / acc[HEAD_DIM:HEAD_DIM + 1, :]
        o_ref[c * LANES:(c + 1) * LANES, h * HEAD_DIM:(h + 1) * HEAD_DIM] = o.T.astype(o_ref.dtype)


def _moba_layer(q, bias, kb, vt, q_s, k_new, v_new, cache_k3, cache_v3, page_table, picks):
    n_seq, _, seq, _ = q.shape
    n_dec, n_pages = page_table.shape
    nblk = seq // MOBA_BLOCK
    hg = MOBA_HEADS
    n_hg = N_HEADS // hg
    n_chains = hg * (MOBA_BLOCK // LANES)
    steps = n_seq * n_hg * nblk
    items = n_dec * N_HEADS // steps
    assert items * steps == n_dec * N_HEADS and N_HEADS % items == 0
    per_block = MOBA_BLOCK // PAGE_SIZE

    def first_item(n, h, i):
        return ((n * n_hg + h) * nblk + i) * items

    def dec_index(n, h, i, pk, pt):
        e = first_item(n, h, i)
        return (e // N_HEADS, (e % N_HEADS) // items, 0, 0)

    def page_spec(a, f):
        def index(n, h, i, pk, pt):
            e = first_item(n, h, i) + a
            blk = pk[e * MOBA_TOPK + f // per_block]
            return (pt[(e // N_HEADS) * n_pages + blk * per_block + f % per_block], 0, 0)
        return pl.BlockSpec((None, PAGE_SIZE * N_HEADS, HEAD_DIM), index)

    dec_rows = pl.BlockSpec((None, items, 1, HEAD_DIM), dec_index)
    page_specs = [page_spec(a, f) for a in range(items) for f in range(N_FETCH)]
    resident = pl.Buffered(1)
    dec_shape = (n_dec, N_HEADS, 1, HEAD_DIM)
    return pl.pallas_call(
        functools.partial(_moba_layer_kernel, items),
        grid_spec=pltpu.PrefetchScalarGridSpec(
            num_scalar_prefetch=2,
            grid=(n_seq, n_hg, nblk),
            in_specs=[pl.BlockSpec((None, hg, MOBA_BLOCK, HEAD_DIM), lambda n, h, i, pk, pt: (n, h, i, 0)),
                      pl.BlockSpec((None, hg, None, nblk, MOBA_BLOCK), lambda n, h, i, pk, pt: (n, h, i, 0, 0)),
                      pl.BlockSpec((None, hg, seq, HEAD_DIM), lambda n, h, i, pk, pt: (n, h, 0, 0),
                                   pipeline_mode=resident),
                      pl.BlockSpec((None, hg, nblk, V_ROWS, MOBA_BLOCK), lambda n, h, i, pk, pt: (n, h, 0, 0, 0),
                                   pipeline_mode=resident),
                      dec_rows, dec_rows, dec_rows] + page_specs + page_specs,
            out_specs=[pl.BlockSpec((None, MOBA_BLOCK, hg * HEAD_DIM), lambda n, h, i, pk, pt: (n, i, h)),
                       dec_rows],
            scratch_shapes=[pltpu.VMEM((2 * n_chains, MOBA_BLOCK, LANES), F32),
                            pltpu.VMEM((2 * n_chains, MOBA_BLOCK, LANES), F32),
                            pltpu.VMEM((n_chains, V_ROWS, LANES), F32)]),
        out_shape=[jax.ShapeDtypeStruct((n_seq, seq, D_MODEL), BF16),
                   jax.ShapeDtypeStruct(dec_shape, F32)],
        compiler_params=_params("arbitrary", "arbitrary", "arbitrary"),
        name="moba_layer",
    )(picks, page_table.reshape(-1), q, bias, kb, vt,
      q_s.reshape(dec_shape), k_new.reshape(dec_shape), v_new.reshape(dec_shape),
      *([cache_k3] * len(page_specs)), *([cache_v3] * len(page_specs)))


def _b_out_kernel(final, o_ref, z_ref, x_ref, w_ref, g_ref, y_ref, w_scr):
    _cast_once((w_ref, w_scr))
    v = (o_ref[...].astype(F32) * _silu(z_ref[...])).astype(BF16)
    x = x_ref[...] + _dot(v, w_scr[...])
    y_ref[...] = _rmsnorm(x, g_ref[...]) if final else x


def _b_out(o, z, x, w, g, tm, final):
    t = x.shape[0]
    tile = pl.BlockSpec((tm, D_MODEL), lambda i: (i, 0))
    return pl.pallas_call(
        functools.partial(_b_out_kernel, final),
        grid=(t // tm,),
        in_specs=[tile, tile, tile, _weight_spec((D_MODEL, D_MODEL)), pl.BlockSpec((1, D_MODEL), lambda i: (0, 0))],
        out_specs=tile,
        out_shape=jax.ShapeDtypeStruct((t, D_MODEL), F32),
        scratch_shapes=[pltpu.VMEM((D_MODEL, D_MODEL), BF16)],
        compiler_params=_params("arbitrary"),
        name="b_out",
    )(o, z, x, w, g)


PAGES_PER_STEP = 16


def _page_mean_kernel(*refs):
    pages, out_ref = refs[1:1 + PAGES_PER_STEP], refs[-1]
    per_block = MOBA_BLOCK // PAGE_SIZE
    for b in range(PAGES_PER_STEP // per_block):
        tot = None
        for r in range(per_block):
            page = pages[per_block * b + r][...].reshape(PAGE_SIZE, N_HEADS, HEAD_DIM)
            part = jnp.sum(page, axis=0)
            tot = part if tot is None else tot + part
        tot = tot * (1.0 / MOBA_BLOCK)
        for hd in range(N_HEADS):
            out_ref[hd, b:b + 1, :] = tot[hd:hd + 1, :]


def _page_means(cache_k2, page_table):
    n, n_pages = page_table.shape
    steps = n_pages // PAGES_PER_STEP
    nb = PAGES_PER_STEP * PAGE_SIZE // MOBA_BLOCK

    def page_spec(r):
        return pl.BlockSpec((None, PAGE_SIZE * N_HEADS, HEAD_DIM),
                            lambda s, j, pt, r=r: (pt[s * n_pages + j * PAGES_PER_STEP + r], 0, 0))

    return pl.pallas_call(
        _page_mean_kernel,
        grid_spec=pltpu.PrefetchScalarGridSpec(
            num_scalar_prefetch=1,
            grid=(n, steps),
            in_specs=[page_spec(r) for r in range(PAGES_PER_STEP)],
            out_specs=pl.BlockSpec((None, N_HEADS, nb, HEAD_DIM), lambda s, j, pt: (s, 0, j, 0))),
        out_shape=jax.ShapeDtypeStruct((n, N_HEADS, n_pages * PAGE_SIZE // MOBA_BLOCK, HEAD_DIM), F32),
        compiler_params=_params("parallel", "arbitrary"),
        name="page_means",
    )(page_table.reshape(-1), *([cache_k2] * PAGES_PER_STEP))


def _q_sample_kernel(x_ref, g_ref, w_ref, cos_ref, sin_ref, q_ref, z_ref):
    h = _rmsnorm(x_ref[...], g_ref[...]).astype(BF16)
    qz = _dot(h, w_ref[...].astype(BF16))
    z_ref[...] = qz[:, D_MODEL:]
    cos2, sin2 = cos_ref[...], sin_ref[...]
    for hd in range(N_HEADS):
        sl = slice(hd * HEAD_DIM, (hd + 1) * HEAD_DIM)
        q_ref[:, sl] = _rope(qz[:, sl], cos2, sin2)


def _q_sample(x, g, w, cos2, sin2):
    n = x.shape[0]
    return pl.pallas_call(
        _q_sample_kernel,
        out_shape=[jax.ShapeDtypeStruct((n, D_MODEL), F32)] * 2,
        compiler_params=pltpu.CompilerParams(vmem_limit_bytes=VMEM_LIMIT),
        name="q_sample",
    )(x, g, w, cos2, sin2)


def _pick_kernel(q_ref, kbar_ref, knew_ref, idx_ref):
    n_seq, _, n_past, _ = kbar_ref.shape
    lane_o = lax.broadcasted_iota(jnp.int32, (8, LANES), 1)
    row_o = lax.broadcasted_iota(jnp.int32, (8, LANES), 0)

    def one_sequence(sq, carry):
        out = jnp.zeros((8, LANES), jnp.int32)
        q_row, knew_row = q_ref[pl.ds(sq, 1), :], knew_ref[pl.ds(sq, 1), :]
        for hd in range(N_HEADS):
            sl = slice(hd * HEAD_DIM, (hd + 1) * HEAD_DIM)
            q8 = jnp.broadcast_to(q_row[:, sl], (8, HEAD_DIM))
            own_mean = jnp.broadcast_to(knew_row[:, sl] * (1.0 / MOBA_BLOCK), (8, HEAD_DIM))
            means = jnp.concatenate([kbar_ref[sq, hd], own_mean], axis=0)
            s = _dot_nt(q8, means, precision=lax.Precision.HIGHEST)
            lane = lax.broadcasted_iota(jnp.int32, s.shape, 1)
            _, picks = _top3(s, lane < n_past, 1)
            for r, idx in enumerate(picks):
                out = jnp.where((row_o == hd) & (lane_o == r), idx[0:1, :], out)
        idx_ref[sq] = out
        return carry

    lax.fori_loop(0, n_seq, one_sequence, 0)


def _pick_blocks(q, kbar, knew):
    n = kbar.shape[0]
    return pl.pallas_call(
        _pick_kernel,
        out_shape=jax.ShapeDtypeStruct((n, 8, LANES), jnp.int32),
        compiler_params=pltpu.CompilerParams(vmem_limit_bytes=VMEM_LIMIT),
        name="pick_blocks",
    )(q, kbar, knew)


N_FETCH = MOBA_TOPK * (MOBA_BLOCK // PAGE_SIZE)


def _decode_attend(head, q, k_new, v_new, k_pages, v_pages):
    scale = HEAD_DIM ** -0.5
    q8 = jnp.broadcast_to(q, (8, HEAD_DIM))
    head_rows = pl.ds(head, PAGE_SIZE, stride=N_HEADS)
    kc = jnp.concatenate([r[head_rows, :] for r in k_pages], axis=0)
    vc = jnp.concatenate([r[head_rows, :] for r in v_pages], axis=0)
    s = _dot_nt(q8, kc)[0:1, :] * scale
    s_new = jnp.sum(q * k_new, axis=1, keepdims=True) * scale
    m = jnp.maximum(jnp.max(s, axis=1, keepdims=True), s_new)
    p = jnp.exp(s - m)
    p_new = jnp.exp(s_new - m)
    l = jnp.sum(p, axis=1, keepdims=True) + p_new
    pv = _dot(jnp.broadcast_to(p, (8, p.shape[1])), vc)[0:1, :]
    return (pv + p_new * v_new) / l


def _rope_tables(pos):
    half = HEAD_DIM // 2
    inv = ROPE_THETA ** (-jnp.arange(half, dtype=F32) / half)
    ang = pos.astype(F32)[:, None] * inv[None, :]
    cos, sin = jnp.cos(ang), jnp.sin(ang)
    return jnp.concatenate([cos, cos], axis=1), jnp.concatenate([-sin, sin], axis=1)


def kernel(x_prompt, x_sample, state_ssm_re, state_ssm_im, cache_k, cache_v, page_table, a_norm, a_w_in, a_lam_re, a_lam_im, a_log_dt, a_b_re, a_b_im, a_c_re, a_c_im, a_d, a_w_glu, a_w_out, kv_norm, w_kv, b_norm, b_w_in, b_w_out, final_norm):
    n_seq, seq, _ = x_prompt.shape
    n_dec = x_sample.shape[0]
    n_pool = cache_k.shape[0]
    past_len = page_table.shape[1] * PAGE_SIZE
    assert x_sample.shape[1] == 1 and seq % ROW_TILE == 0 and ROW_TILE % MOBA_BLOCK == 0
    assert past_len % (PAGES_PER_STEP * PAGE_SIZE) == 0 and past_len // MOBA_BLOCK >= MOBA_TOPK
    assert seq // MOBA_BLOCK <= LANES and (seq // MOBA_BLOCK) % 8 == 0
    n_a, n_b = a_norm.shape[0], b_norm.shape[0]

    xp = x_prompt.reshape(n_seq * seq, D_MODEL)
    xs = x_sample.reshape(n_dec, D_MODEL)
    cos_p, sin_p = _rope_tables(jnp.arange(seq, dtype=jnp.int32))
    cos_s, sin_s = _rope_tables(jnp.full((n_dec,), past_len, jnp.int32))

    st_p_re, st_p_im, st_s_re, st_s_im = [], [], [], []
    for l in range(n_a):
        prep = _ssm_prep(a_lam_re[l], a_lam_im[l], a_log_dt[l], a_b_re[l], a_b_im[l], a_c_re[l], a_c_im[l],
                         seq // SSM_TBLK)
        norm, d = a_norm[l][None], a_d[l][None]
        w_in, wg, wo = a_w_in[l], a_w_glu[l], a_w_out[l]
        xp, hr, hi = _s5_layer_prompt(xp, n_seq, norm, w_in, prep, d, wg, wo)
        st_p_re.append(hr)
        st_p_im.append(hi)
        xs, hr, hi = _s5_layer_sample(xs, state_ssm_re[l], state_ssm_im[l], norm, w_in, prep, d, wg, wo)
        st_s_re.append(hr)
        st_s_im.append(hi)

    k_p, v_p, kb, vt, kbar_p = _kv_proj(xp, kv_norm[None], w_kv, cos_p, sin_p, n_seq, ROW_TILE, True)
    k_s, v_s = _kv_proj(xs, kv_norm[None], w_kv, cos_s, sin_s, 1, n_dec, False)
    nblk = seq // MOBA_BLOCK
    kbar_p = kbar_p.reshape(n_seq, nblk, N_HEADS, HEAD_DIM).transpose(0, 2, 1, 3)
    cache_k3 = cache_k.reshape(n_pool, PAGE_SIZE * N_HEADS, HEAD_DIM)
    cache_v3 = cache_v.reshape(n_pool, PAGE_SIZE * N_HEADS, HEAD_DIM)
    kbar_s = _page_means(cache_k3, page_table)

    for j in range(n_b):
        w_in, wo = b_w_in[j], b_w_out[j]
        final = j == n_b - 1
        q, bias, z = _q_proj(xp, b_norm[j][None], w_in, cos_p, sin_p, kbar_p, n_seq)
        q_s, z_s = _q_sample(xs, b_norm[j][None], w_in, cos_s, sin_s)
        picks = _pick_blocks(q_s, kbar_s, k_s)[:, :, :MOBA_TOPK].reshape(-1)
        o, o_s = _moba_layer(q, bias, kb, vt, q_s, k_s, v_s, cache_k3, cache_v3, page_table, picks)
        xp = _b_out(o.reshape(n_seq * seq, D_MODEL), z, xp, wo, final_norm[None], ROW_TILE, final)
        xs = _b_out(o_s.reshape(n_dec, D_MODEL), z_s, xs, wo, final_norm[None], n_dec, final)

    return (xp.reshape(n_seq, seq, D_MODEL), xs.reshape(n_dec, 1, D_MODEL),
            jnp.stack(st_p_re), jnp.stack(st_p_im), jnp.stack(st_s_re), jnp.stack(st_s_im),
            k_p.reshape(n_seq, seq, N_HEADS, HEAD_DIM), v_p.reshape(n_seq, seq, N_HEADS, HEAD_DIM),
            k_s.reshape(n_dec, 1, N_HEADS, HEAD_DIM), v_s.reshape(n_dec, 1, N_HEADS, HEAD_DIM))
```

```python
import functools
import math

import jax
import jax.numpy as jnp
from jax import lax
from jax.experimental import pallas as pl
from jax.experimental.pallas import tpu as pltpu

D_MODEL = 1024
SSM_GROUP = 16
SSM_GROUPS = D_MODEL // SSM_GROUP
SSM_STATE = 64
SSM_TBLK = 16
SSM_BLKW = SSM_TBLK * SSM_GROUP
HEAD_DIM = 128
N_HEADS = D_MODEL // HEAD_DIM
MOBA_BLOCK = 256
MOBA_TOPK = 3
PAGE_SIZE = 128
ROPE_THETA = 10000.0
RMS_EPS = 1e-6
NEG_INF = -1e30
LANES = 128
VMEM_LIMIT = 48 * 1024 * 1024
ROW_TILE = 512
MOBA_HEADS = 4
V_ROWS = HEAD_DIM + 16

F32 = jnp.float32
BF16 = jnp.bfloat16


def _params(*sem):
    return pltpu.CompilerParams(dimension_semantics=sem, vmem_limit_bytes=VMEM_LIMIT)


def _rmsnorm(x, g):
    r = lax.rsqrt(jnp.mean(x * x, axis=-1, keepdims=True) + RMS_EPS)
    return x * r * g


def _dot(a, b):
    return jnp.dot(a, b, preferred_element_type=F32)


def _dot_nt(a, b, precision=None):
    return lax.dot_general(a, b, (((1,), (1,)), ((), ())), precision=precision,
                           preferred_element_type=F32)


def _weight_spec(shape):
    return pl.BlockSpec(shape, lambda *_: (0,) * len(shape), pipeline_mode=pl.Buffered(1))


def _cast_once(*pairs):
    @pl.when(pl.program_id(0) == 0)
    def _():
        for w_ref, w_scr in pairs:
            w_scr[...] = w_ref[...].astype(BF16)


def _rope(x, cos2, sin2):
    return x * cos2 + pltpu.roll(x, HEAD_DIM // 2, 1) * sin2


def _silu(z):
    return z * jax.nn.sigmoid(z)


def _top3(s, valid, axis):
    n_lanes = s.shape[axis]
    lane = lax.broadcasted_iota(jnp.int32, s.shape, axis)
    sm = jnp.where(valid, s, NEG_INF)
    sel = jnp.zeros(s.shape, F32)
    picks = []
    for _ in range(MOBA_TOPK):
        m = jnp.max(sm, axis=axis, keepdims=True)
        idx = jnp.min(jnp.where(sm == m, lane, n_lanes), axis=axis, keepdims=True)
        hit = lane == idx
        sel = jnp.where(hit, jnp.where(valid, 1.0, sel), sel)
        sm = jnp.where(hit, NEG_INF, sm)
        picks.append(idx)
    return sel, picks


GROUPS_PER_TILE = LANES // SSM_GROUP
STEP_TILES = SSM_BLKW // LANES


def _slot_transpose(xs, slot):
    xs = list(xs)
    step = GROUPS_PER_TILE // 2
    while step:
        low = (slot & step) == 0
        for i in range(GROUPS_PER_TILE):
            if i & step == 0:
                a, b = xs[i], xs[i + step]
                xs[i] = jnp.where(low, a, pltpu.roll(b, step * SSM_GROUP, 1))
                xs[i + step] = jnp.where(low, pltpu.roll(a, LANES - step * SSM_GROUP, 1), b)
        step //= 2
    return xs


def _to_blocked(u, tiles_ref, ug_ref):
    nb = ug_ref.shape[0]
    slot = lax.broadcasted_iota(jnp.int32, (nb, LANES), 1) // SSM_GROUP
    for tile in range(D_MODEL // LANES):
        tiles_ref[tile] = u[:, tile * LANES:(tile + 1) * LANES]
    for tile in range(D_MODEL // LANES):
        for hh in range(STEP_TILES):
            steps = [tiles_ref[tile, pl.ds(hh * GROUPS_PER_TILE + rp, nb, stride=SSM_TBLK), :]
                     for rp in range(GROUPS_PER_TILE)]
            for gp, piece in enumerate(_slot_transpose(steps, slot)):
                col = (tile * GROUPS_PER_TILE + gp) * SSM_BLKW + hh * LANES
                ug_ref[:, col:col + LANES] = piece.astype(ug_ref.dtype)


def _from_blocked(yg_ref, tiles_ref):
    nb = yg_ref.shape[0]
    slot = lax.broadcasted_iota(jnp.int32, (nb, LANES), 1) // SSM_GROUP
    for tile in range(D_MODEL // LANES):
        for hh in range(STEP_TILES):
            cols = [(tile * GROUPS_PER_TILE + gp) * SSM_BLKW + hh * LANES for gp in range(GROUPS_PER_TILE)]
            for rp, piece in enumerate(_slot_transpose([yg_ref[:, c:c + LANES] for c in cols], slot)):
                tiles_ref[tile, pl.ds(hh * GROUPS_PER_TILE + rp, nb, stride=SSM_TBLK), :] = piece
    return jnp.concatenate([tiles_ref[tile] for tile in range(D_MODEL // LANES)], axis=1)


def _a_in_kernel(blocked, x_ref, g_ref, w_ref, u_ref, z_ref, *rest):
    w_scr = rest[-1]
    _cast_once((w_ref, w_scr))
    h = _rmsnorm(x_ref[...], g_ref[...]).astype(BF16)
    uz = _dot(h, w_scr[...])
    u_ref[...] = uz[:, :D_MODEL]
    z_ref[...] = uz[:, D_MODEL:]
    if blocked:
        ug_ref, tiles_ref = rest[:2]
        _to_blocked(uz[:, :D_MODEL], tiles_ref, ug_ref)


def _a_in(x, g, w, tm, blocked):
    t = x.shape[0]
    tile = pl.BlockSpec((tm, D_MODEL), lambda i: (i, 0))
    out_specs, out_shape = [tile, tile], [jax.ShapeDtypeStruct((t, D_MODEL), F32)] * 2
    if blocked:
        out_specs.append(pl.BlockSpec((tm // SSM_TBLK, SSM_GROUPS * SSM_BLKW), lambda i: (i, 0)))
        out_shape.append(jax.ShapeDtypeStruct((t // SSM_TBLK, SSM_GROUPS * SSM_BLKW), BF16))
    return pl.pallas_call(
        functools.partial(_a_in_kernel, blocked),
        grid=(t // tm,),
        in_specs=[tile, pl.BlockSpec((1, D_MODEL), lambda i: (0, 0)), _weight_spec((D_MODEL, 2 * D_MODEL))],
        out_specs=out_specs,
        out_shape=out_shape,
        scratch_shapes=([pltpu.VMEM((D_MODEL // LANES, tm, LANES), F32)] if blocked else [])
        + [pltpu.VMEM((D_MODEL, 2 * D_MODEL), BF16)],
        compiler_params=_params("arbitrary"),
        name="a_in",
    )(x, g, w)


def _ssm_scan_kernel(n_seq, ug_ref, m_ref, f_ref, e_ref, ap_ref, y_ref, st_ref):
    rows = ug_ref.shape[0]
    nblk = rows // n_seq
    ug = ug_ref[...]
    s = _dot(ug, f_ref[...])
    row = lax.broadcasted_iota(jnp.int32, (nblk, LANES), 0)
    hp = []
    for q in range(n_seq):
        re = s[q * nblk:(q + 1) * nblk, :LANES]
        im = s[q * nblk:(q + 1) * nblk, LANES:]
        step, k = 1, 0
        while step < nblk:
            ar = ap_ref[2 * k:2 * k + 1, :]
            ai = ap_ref[2 * k + 1:2 * k + 2, :]
            sr = jnp.where(row >= step, pltpu.roll(re, step, 0), 0.0)
            si = jnp.where(row >= step, pltpu.roll(im, step, 0), 0.0)
            re, im = re + ar * sr - ai * si, im + ar * si + ai * sr
            step, k = step * 2, k + 1
        st_ref[2 * q:2 * q + 1, :] = re[nblk - 1:nblk, :]
        st_ref[2 * q + 1:2 * q + 2, :] = im[nblk - 1:nblk, :]
        pr = jnp.where(row >= 1, pltpu.roll(re, 1, 0), 0.0)
        pi = jnp.where(row >= 1, pltpu.roll(im, 1, 0), 0.0)
        hp.append(jnp.concatenate([pr, pi], axis=1))
    hprev = jnp.concatenate(hp, axis=0).astype(BF16)
    carry = _dot(hprev, e_ref[...])
    y_ref[:, :SSM_BLKW] = _dot(ug[:, :SSM_BLKW], m_ref[0]) + carry[:, :SSM_BLKW]
    y_ref[:, SSM_BLKW:] = _dot(ug[:, SSM_BLKW:], m_ref[1]) + carry[:, SSM_BLKW:]


def _ssm_scan(ug, prep, n_seq):
    rows = ug.shape[0]
    n_pairs = SSM_GROUPS // 2
    n_pow = prep["apow"].shape[1]
    return pl.pallas_call(
        functools.partial(_ssm_scan_kernel, n_seq),
        grid=(n_pairs,),
        in_specs=[pl.BlockSpec((rows, 2 * SSM_BLKW), lambda p: (0, p)),
                  pl.BlockSpec((2, SSM_BLKW, SSM_BLKW), lambda p: (p, 0, 0)),
                  pl.BlockSpec((None, 2 * SSM_BLKW, 2 * LANES), lambda p: (p, 0, 0)),
                  pl.BlockSpec((None, 2 * LANES, 2 * SSM_BLKW), lambda p: (p, 0, 0)),
                  pl.BlockSpec((None, n_pow, LANES), lambda p: (p, 0, 0))],
        out_specs=[pl.BlockSpec((rows, 2 * SSM_BLKW), lambda p: (0, p)),
                   pl.BlockSpec((None, 2 * n_seq, LANES), lambda p: (p, 0, 0))],
        out_shape=[jax.ShapeDtypeStruct((rows, SSM_GROUPS * SSM_BLKW), F32),
                   jax.ShapeDtypeStruct((n_pairs, 2 * n_seq, LANES), F32)],
        compiler_params=_params("parallel"),
        name="ssm_scan",
    )(ug, prep["m"], prep["f"], prep["e"], prep["apow"])


def _ssm_step_kernel(u_ref, hr_ref, hi_ref, bb_ref, ar_ref, ai_ref, cs_ref, y_ref, or_ref, oi_ref):
    bu = lax.dot_general(u_ref[...], bb_ref[...], (((2,), (1,)), ((0,), (0,))),
                         preferred_element_type=F32)
    h0r, h0i = hr_ref[...], hi_ref[...]
    ar, ai = ar_ref[...], ai_ref[...]
    hr = bu[:, :, :SSM_STATE] + ar * h0r - ai * h0i
    hi = bu[:, :, SSM_STATE:] + ar * h0i + ai * h0r
    or_ref[...] = hr
    oi_ref[...] = hi
    y_ref[...] = lax.dot_general(jnp.concatenate([hr, hi], axis=2), cs_ref[...],
                                 (((2,), (1,)), ((0,), (0,))), preferred_element_type=F32)


def _ssm_step(u_g, h0r, h0i, prep):
    g, n, _ = u_g.shape
    return pl.pallas_call(
        _ssm_step_kernel,
        out_shape=[jax.ShapeDtypeStruct((g, n, SSM_GROUP), F32),
                   jax.ShapeDtypeStruct((g, n, SSM_STATE), F32),
                   jax.ShapeDtypeStruct((g, n, SSM_STATE), F32)],
        compiler_params=pltpu.CompilerParams(vmem_limit_bytes=VMEM_LIMIT),
        name="ssm_step",
    )(u_g, h0r, h0i, prep["bb"], prep["a_re"], prep["a_im"], prep["cs"])


def _a_out_kernel(blocked, ys_ref, u_ref, z_ref, x_ref, d_ref, wg_ref, wo_ref, o_ref, *scratch):
    wg_scr, wo_scr = scratch[-2:]
    _cast_once((wg_ref, wg_scr), (wo_ref, wo_scr))
    ys = _from_blocked(ys_ref, scratch[0]) if blocked else ys_ref[...]
    y = ys + d_ref[...] * u_ref[...]
    g = jax.nn.gelu(y)
    y2 = g * jax.nn.sigmoid(_dot(g.astype(BF16), wg_scr[...]))
    v = (y2 * _silu(z_ref[...])).astype(BF16)
    o_ref[...] = x_ref[...] + _dot(v, wo_scr[...])


def _a_out(ys, u, z, x, d, wg, wo, tm, blocked):
    t = x.shape[0]
    tile = pl.BlockSpec((tm, D_MODEL), lambda i: (i, 0))
    full = _weight_spec((D_MODEL, D_MODEL))
    ys_spec = pl.BlockSpec((tm // SSM_TBLK, SSM_GROUPS * SSM_BLKW), lambda i: (i, 0)) if blocked else tile
    return pl.pallas_call(
        functools.partial(_a_out_kernel, blocked),
        grid=(t // tm,),
        in_specs=[ys_spec, tile, tile, tile, pl.BlockSpec((1, D_MODEL), lambda i: (0, 0)), full, full],
        out_specs=tile,
        out_shape=jax.ShapeDtypeStruct((t, D_MODEL), F32),
        scratch_shapes=([pltpu.VMEM((D_MODEL // LANES, tm, LANES), F32)] if blocked else [])
        + [pltpu.VMEM((D_MODEL, D_MODEL), BF16)] * 2,
        compiler_params=_params("arbitrary"),
        name="a_out",
    )(ys, u, z, x, d, wg, wo)


def _ssm_prep_kernel(n_steps, lrc_ref, lic_ref, lrr_ref, lir_ref, ldt_ref, b_re_ref, b_im_ref, bt_re_ref, bt_im_ref,
                     ct_re_ref, ct_im_ref, w_re_ref, w_im_ref, ca_re_ref, nca_im_ref, k_ref, ap_ref, ab_ref,
                     bb_re_ref, bb_im_ref, cs_ref):
    hi = lax.Precision.HIGHEST
    dt = jnp.exp(ldt_ref[...])

    def discretise(lr, li):
        x, th = lr * dt, li * dt
        mag = jnp.exp(x)
        a_re, a_im = mag * jnp.cos(th), mag * jnp.sin(th)
        den = lr * lr + li * li
        f_re = ((a_re - 1.0) * lr + a_im * li) / den
        f_im = (a_im * lr - (a_re - 1.0) * li) / den
        return x, th, a_re, a_im, f_re, f_im

    xc, thc, _, _, fc_re, fc_im = discretise(lrc_ref[...], lic_ref[...])
    lag = lax.broadcasted_iota(jnp.int32, (1, LANES), 1).astype(F32)
    mag = jnp.exp(xc * lag)
    pw_re, pw_im = mag * jnp.cos(thc * lag), mag * jnp.sin(thc * lag)
    b_re, b_im = b_re_ref[...], b_im_ref[...]
    bb_re = fc_re * b_re - fc_im * b_im
    bb_im = fc_re * b_im + fc_im * b_re
    bb_re_ref[...] = bb_re
    bb_im_ref[...] = bb_im
    ct_re, ct_im = ct_re_ref[...], ct_im_ref[...]
    cs_ref[...] = jnp.concatenate([ct_re, -ct_im], axis=0)
    xr, thr, ar_re, ar_im, fr_re, fr_im = discretise(lrr_ref[...], lir_ref[...])
    ab_ref[0:1, :] = ar_re
    ab_ref[1:2, :] = ar_im
    bbt_re = fr_re * bt_re_ref[...] - fr_im * bt_im_ref[...]
    bbt_im = fr_re * bt_im_ref[...] + fr_im * bt_re_ref[...]
    for k in range(n_steps):
        n = float(SSM_TBLK * 2 ** k)
        m = jnp.exp(xr * n)
        ap_ref[2 * k:2 * k + 1, :] = m * jnp.cos(thr * n)
        ap_ref[2 * k + 1:2 * k + 2, :] = m * jnp.sin(thr * n)
    for t in range(SSM_TBLK + 1):
        pr, pi = pw_re[:, t:t + 1], pw_im[:, t:t + 1]
        ca_re = pr * ct_re - pi * ct_im
        nca_im = -(pr * ct_im + pi * ct_re)
        ca_re_ref[t] = ca_re
        nca_im_ref[t] = nca_im
        if t < SSM_TBLK:
            w_re_ref[t] = pr * bb_re - pi * bb_im
            w_im_ref[t] = pr * bb_im + pi * bb_re
            k_ref[t] = (jnp.dot(bbt_re, ca_re, precision=hi, preferred_element_type=F32)
                        + jnp.dot(bbt_im, nca_im, precision=hi, preferred_element_type=F32))


def _ssm_prep(lam_re, lam_im, log_dt, b_re, b_im, c_re, c_im, nblk):
    g, p = lam_re.shape
    n_steps = max(1, math.ceil(math.log2(nblk)))
    col = lambda x: x.reshape(g, p, 1)
    row = lambda x: x.reshape(g, 1, p)
    tr = lambda x: x.transpose(0, 2, 1)

    def spec(*shape):
        return pl.BlockSpec((None,) + shape, lambda i: (i,) + (0,) * len(shape))

    shapes = [(SSM_TBLK, p, SSM_GROUP)] * 2 + [(SSM_TBLK + 1, p, SSM_GROUP)] * 2 + \
             [(SSM_TBLK, SSM_GROUP, SSM_GROUP), (2 * n_steps, p), (2, p), (p, SSM_GROUP), (p, SSM_GROUP),
              (2 * p, SSM_GROUP)]
    w_re, w_im, ca_re, nca_im, kern, ap, ab, bb_re, bb_im, cs = pl.pallas_call(
        functools.partial(_ssm_prep_kernel, n_steps),
        grid=(g,),
        in_specs=[spec(p, 1), spec(p, 1), spec(1, p), spec(1, p), spec(1, 1),
                  spec(p, SSM_GROUP), spec(p, SSM_GROUP), spec(SSM_GROUP, p), spec(SSM_GROUP, p),
                  spec(p, SSM_GROUP), spec(p, SSM_GROUP)],
        out_specs=[spec(*sh) for sh in shapes],
        out_shape=[jax.ShapeDtypeStruct((g,) + sh, F32) for sh in shapes],
        compiler_params=_params("parallel"),
        name="ssm_prep",
    )(col(lam_re), col(lam_im), row(lam_re), row(lam_im), log_dt.reshape(g, 1, 1),
      b_re, b_im, tr(b_re), tr(b_im), tr(c_re), tr(c_im))

    f_re_m = w_re[:, ::-1].transpose(0, 1, 3, 2).reshape(g, SSM_BLKW, p)
    f_im_m = w_im[:, ::-1].transpose(0, 1, 3, 2).reshape(g, SSM_BLKW, p)
    lag = jnp.arange(SSM_TBLK)[None, :] - jnp.arange(SSM_TBLK)[:, None]
    m = jnp.where((lag >= 0)[None, :, :, None, None], kern[:, jnp.clip(lag, 0, SSM_TBLK - 1)], 0.0)
    m = m.transpose(0, 1, 3, 2, 4).reshape(g, SSM_BLKW, SSM_BLKW)
    e_re = ca_re[:, 1:].transpose(0, 2, 1, 3).reshape(g, p, SSM_BLKW)
    e_im = nca_im[:, 1:].transpose(0, 2, 1, 3).reshape(g, p, SSM_BLKW)

    n_pairs = g // 2
    zf = jnp.zeros((n_pairs, SSM_BLKW, p), F32)
    fp = jnp.concatenate([
        jnp.concatenate([f_re_m[0::2], zf, f_im_m[0::2], zf], axis=2),
        jnp.concatenate([zf, f_re_m[1::2], zf, f_im_m[1::2]], axis=2)], axis=1)
    ze = jnp.zeros((n_pairs, p, SSM_BLKW), F32)
    ep = jnp.concatenate([
        jnp.concatenate([e_re[0::2], ze], axis=2),
        jnp.concatenate([ze, e_re[1::2]], axis=2),
        jnp.concatenate([e_im[0::2], ze], axis=2),
        jnp.concatenate([ze, e_im[1::2]], axis=2)], axis=1)
    ap = ap.reshape(n_pairs, 2, 2 * n_steps, p).transpose(0, 2, 1, 3).reshape(n_pairs, 2 * n_steps, 2 * p)
    return {
        "m": m.astype(BF16), "f": fp.astype(BF16), "e": ep.astype(BF16), "apow": ap,
        "bb": jnp.concatenate([tr(bb_re), tr(bb_im)], axis=2),
        "a_re": ab[:, 0:1, :], "a_im": ab[:, 1:2, :],
        "cs": cs,
    }


def _s5_layer_prompt(x, n_seq, norm, w_in, prep, d, wg, wo):
    u, z, ug = _a_in(x, norm, w_in, ROW_TILE, True)
    yg, st = _ssm_scan(ug, prep, n_seq)
    st = st.reshape(SSM_GROUPS // 2, n_seq, 2, 2, SSM_STATE)
    st = st.transpose(2, 1, 0, 3, 4).reshape(2, n_seq, SSM_GROUPS, SSM_STATE)
    return _a_out(yg, u, z, x, d, wg, wo, ROW_TILE, True), st[0], st[1]


def _s5_layer_sample(x, h0r, h0i, norm, w_in, prep, d, wg, wo):
    n = x.shape[0]
    u, z = _a_in(x, norm, w_in, n, False)
    u_g = u.reshape(n, SSM_GROUPS, SSM_GROUP).transpose(1, 0, 2)
    y_g, hr, hi = _ssm_step(u_g, h0r.transpose(1, 0, 2), h0i.transpose(1, 0, 2), prep)
    ys = y_g.transpose(1, 0, 2).reshape(n, D_MODEL)
    return _a_out(ys, u, z, x, d, wg, wo, n, False), hr.transpose(1, 0, 2), hi.transpose(1, 0, 2)


def _kv_kernel(prompt, x_ref, g_ref, w_ref, cos_ref, sin_ref, k_ref, v_ref, *rest):
    w_scr = rest[-1]
    _cast_once((w_ref, w_scr))
    h = _rmsnorm(x_ref[...], g_ref[...]).astype(BF16)
    kv = _dot(h, w_scr[...])
    cos2, sin2 = cos_ref[...], sin_ref[...]
    tm = kv.shape[0]
    for hd in range(N_HEADS):
        sl = slice(hd * HEAD_DIM, (hd + 1) * HEAD_DIM)
        k = _rope(kv[:, sl], cos2, sin2)
        v = kv[:, D_MODEL + hd * HEAD_DIM:D_MODEL + (hd + 1) * HEAD_DIM]
        k_ref[:, sl] = k
        v_ref[:, sl] = v
        if prompt:
            kb_ref, vt_ref, mean_ref = rest[:3]
            kb_ref[hd] = k.astype(BF16)
            for b in range(tm // MOBA_BLOCK):
                rows = slice(b * MOBA_BLOCK, (b + 1) * MOBA_BLOCK)
                mean_ref[b, :, sl] = jnp.sum(k[rows], axis=0, keepdims=True) * (1.0 / MOBA_BLOCK)
                vt_ref[hd, b, :HEAD_DIM, :] = v[rows].T.astype(BF16)
                vt_ref[hd, b, HEAD_DIM:, :] = jnp.ones((V_ROWS - HEAD_DIM, MOBA_BLOCK), BF16)


def _kv_proj(x, g, w, cos2, sin2, n_seq, tm, prompt):
    t = x.shape[0]
    per_seq = t // n_seq // tm
    tile = pl.BlockSpec((tm, D_MODEL), lambda i: (i, 0))
    rope_tile = pl.BlockSpec((tm, HEAD_DIM), lambda i: (i % per_seq, 0))
    out_specs = [tile, tile]
    out_shape = [jax.ShapeDtypeStruct((t, D_MODEL), F32)] * 2
    if prompt:
        nb = tm // MOBA_BLOCK
        out_specs += [
            pl.BlockSpec((None, N_HEADS, tm, HEAD_DIM), lambda i: (i // per_seq, 0, i % per_seq, 0)),
            pl.BlockSpec((None, N_HEADS, nb, V_ROWS, MOBA_BLOCK), lambda i: (i // per_seq, 0, i % per_seq, 0, 0)),
            pl.BlockSpec((nb, 1, D_MODEL), lambda i: (i, 0, 0))]
        out_shape += [
            jax.ShapeDtypeStruct((n_seq, N_HEADS, t // n_seq, HEAD_DIM), BF16),
            jax.ShapeDtypeStruct((n_seq, N_HEADS, t // n_seq // MOBA_BLOCK, V_ROWS, MOBA_BLOCK), BF16),
            jax.ShapeDtypeStruct((t // MOBA_BLOCK, 1, D_MODEL), F32)]
    return pl.pallas_call(
        functools.partial(_kv_kernel, prompt),
        grid=(t // tm,),
        in_specs=[tile, pl.BlockSpec((1, D_MODEL), lambda i: (0, 0)), _weight_spec((D_MODEL, 2 * D_MODEL)),
                  rope_tile, rope_tile],
        out_specs=out_specs,
        out_shape=out_shape,
        scratch_shapes=[pltpu.VMEM((D_MODEL, 2 * D_MODEL), BF16)],
        compiler_params=_params("arbitrary"),
        name="kv_proj",
    )(x, g, w, cos2, sin2)


def _q_kernel(blocks_per_seq, x_ref, g_ref, w_ref, cos_ref, sin_ref, kbar_ref, q_ref, bias_ref, z_ref, w_scr):
    _cast_once((w_ref, w_scr))
    tm = x_ref.shape[0]
    per_tile = tm // MOBA_BLOCK
    first = (pl.program_id(0) * per_tile) % blocks_per_seq
    h = _rmsnorm(x_ref[...], g_ref[...]).astype(BF16)
    qz = _dot(h, w_scr[...])
    z_ref[...] = qz[:, D_MODEL:]
    cos2, sin2 = cos_ref[...], sin_ref[...]
    blk = lax.broadcasted_iota(jnp.int32, (blocks_per_seq, MOBA_BLOCK), 0)
    for hd in range(N_HEADS):
        q = _rope(qz[:, hd * HEAD_DIM:(hd + 1) * HEAD_DIM], cos2, sin2)
        q_ref[hd] = (q * (HEAD_DIM ** -0.5)).astype(BF16)
        for b in range(per_tile):
            rows = slice(b * MOBA_BLOCK, (b + 1) * MOBA_BLOCK)
            s = _dot_nt(kbar_ref[hd], q[rows], precision=lax.Precision.HIGHEST)
            sel, _ = _top3(s, blk < first + b, 0)
            bias_ref[hd, b] = jnp.where(sel > 0.5, 0.0, NEG_INF)


def _q_proj(x, g, w, cos2, sin2, kbar, n_seq):
    t = x.shape[0]
    tm = ROW_TILE
    bps = t // n_seq // MOBA_BLOCK
    per_tile = tm // MOBA_BLOCK
    tps = bps // per_tile
    tile = pl.BlockSpec((tm, D_MODEL), lambda i: (i, 0))
    rope_tile = pl.BlockSpec((tm, HEAD_DIM), lambda i: (i % tps, 0))
    return pl.pallas_call(
        functools.partial(_q_kernel, bps),
        grid=(t // tm,),
        in_specs=[tile, pl.BlockSpec((1, D_MODEL), lambda i: (0, 0)), _weight_spec((D_MODEL, 2 * D_MODEL)),
                  rope_tile, rope_tile,
                  pl.BlockSpec((None, N_HEADS, bps, HEAD_DIM), lambda i: (i // tps, 0, 0, 0))],
        out_specs=[pl.BlockSpec((None, N_HEADS, tm, HEAD_DIM), lambda i: (i // tps, 0, i % tps, 0)),
                   pl.BlockSpec((None, N_HEADS, per_tile, bps, MOBA_BLOCK), lambda i: (i // tps, 0, i % tps, 0, 0)),
                   tile],
        out_shape=[jax.ShapeDtypeStruct((n_seq, N_HEADS, t // n_seq, HEAD_DIM), BF16),
                   jax.ShapeDtypeStruct((n_seq, N_HEADS, bps, bps, MOBA_BLOCK), F32),
                   jax.ShapeDtypeStruct((t, D_MODEL), F32)],
        scratch_shapes=[pltpu.VMEM((D_MODEL, 2 * D_MODEL), BF16)],
        compiler_params=_params("arbitrary"),
        name="q_proj",
    )(x, g, w, cos2, sin2, kbar)


def _moba_layer_kernel(items, pk_ref, pt_ref, q_ref, bias_ref, k_ref, vt_ref, qs_ref, kn_ref, vn_ref, *rest):
    n_pages = items * N_FETCH
    k_pages, v_pages = rest[:n_pages], rest[n_pages:2 * n_pages]
    o_ref, os_ref, s_a, s_b, acc_scr = rest[2 * n_pages:]
    step = (pl.program_id(0) * pl.num_programs(1) + pl.program_id(1)) * pl.num_programs(2) + pl.program_id(2)
    head0 = (step * items) % N_HEADS
    for a in range(items):
        pages = slice(a * N_FETCH, (a + 1) * N_FETCH)
        os_ref[a] = _decode_attend(head0 + a, qs_ref[a], kn_ref[a], vn_ref[a], k_pages[pages], v_pages[pages])
    _moba_tile(q_ref, bias_ref, k_ref, vt_ref, o_ref, s_a, s_b, acc_scr)


def _moba_tile(q_ref, bias_ref, k_ref, vt_ref, o_ref, s_a, s_b, acc_scr):
    i = pl.program_id(2)
    nblk = bias_ref.shape[1]
    row = lax.broadcasted_iota(jnp.int32, (MOBA_BLOCK, LANES), 0)
    col = lax.broadcasted_iota(jnp.int32, (MOBA_BLOCK, LANES), 1)
    chains = [(h, c) for h in range(MOBA_HEADS) for c in range(MOBA_BLOCK // LANES)]
    nc = len(chains)
    trips = (i + 1) // 2
    last_trip = nblk // 2 - 1

    def scores(h, c, blk):
        off = pl.multiple_of(blk * MOBA_BLOCK, MOBA_BLOCK)
        return _dot_nt(k_ref[h, pl.ds(off, MOBA_BLOCK), :], q_ref[h, c * LANES:(c + 1) * LANES, :])

    def stage_scores(t, buf):
        mbs = []
        for n, (h, c) in enumerate(chains):
            for u in range(2):
                s = scores(h, c, 2 * t + u)
                buf[u * nc + n] = s
                mbs.append(jnp.max(s, axis=0, keepdims=True))
        return tuple(mbs)

    def consume(t, buf, ms, mbs):
        out = []
        for n, (h, c) in enumerate(chains):
            b = [bias_ref[h, pl.ds(2 * t + u, 1), :][:, c * LANES:(c + 1) * LANES] for u in range(2)]
            m_new = jnp.maximum(ms[n], jnp.maximum(mbs[2 * n] + b[0], mbs[2 * n + 1] + b[1]))
            alpha = jnp.exp(ms[n] - m_new)
            p = [jnp.exp(buf[u * nc + n] - (m_new - b[u])).astype(BF16) for u in range(2)]
            acc_scr[n] = alpha * acc_scr[n] + (_dot(vt_ref[h, 2 * t], p[0]) + _dot(vt_ref[h, 2 * t + 1], p[1]))
            out.append(m_new)
        return tuple(out)

    for n, (h, c) in enumerate(chains):
        s_b[n] = scores(h, c, i)
    mbs_a = stage_scores(0, s_a)
    ms = []
    for n, (h, c) in enumerate(chains):
        s = jnp.where(row <= col + c * LANES, s_b[n], NEG_INF)
        m = jnp.max(s, axis=0, keepdims=True)
        acc_scr[n] = _dot(vt_ref[h, i], jnp.exp(s - m).astype(BF16))
        ms.append(m)

    def body(tt, carry):
        ms, mbs_a = carry
        t0 = 2 * tt
        mbs_b = stage_scores(t0 + 1, s_b)
        ms = consume(t0, s_a, ms, mbs_a)
        mbs_a = stage_scores(jnp.minimum(t0 + 2, last_trip), s_a)
        ms = consume(t0 + 1, s_b, ms, mbs_b)
        return ms, mbs_a

    lax.fori_loop(0, (trips + 1) // 2, body, (tuple(ms), mbs_a))
    for n, (h, c) in enumerate(chains):
        acc = acc_scr[n]
        o = acc[:HEAD_DIM, :] / acc[HEAD_DIM:HEAD_DIM + 1, :]
        o_ref[c * LANES:(c + 1) * LANES, h * HEAD_DIM:(h + 1) * HEAD_DIM] = o.T.astype(o_ref.dtype)


def _moba_layer(q, bias, kb, vt, q_s, k_new, v_new, cache_k3, cache_v3, page_table, picks):
    n_seq, _, seq, _ = q.shape
    n_dec, n_pages = page_table.shape
    nblk = seq // MOBA_BLOCK
    hg = MOBA_HEADS
    n_hg = N_HEADS // hg
    n_chains = hg * (MOBA_BLOCK // LANES)
    steps = n_seq * n_hg * nblk
    items = n_dec * N_HEADS // steps
    assert items * steps == n_dec * N_HEADS and N_HEADS % items == 0
    per_block = MOBA_BLOCK // PAGE_SIZE

    def first_item(n, h, i):
        return ((n * n_hg + h) * nblk + i) * items

    def dec_index(n, h, i, pk, pt):
        e = first_item(n, h, i)
        return (e // N_HEADS, (e % N_HEADS) // items, 0, 0)

    def page_spec(a, f):
        def index(n, h, i, pk, pt):
            e = first_item(n, h, i) + a
            blk = pk[e * MOBA_TOPK + f // per_block]
            return (pt[(e // N_HEADS) * n_pages + blk * per_block + f % per_block], 0, 0)
        return pl.BlockSpec((None, PAGE_SIZE * N_HEADS, HEAD_DIM), index)

    dec_rows = pl.BlockSpec((None, items, 1, HEAD_DIM), dec_index)
    page_specs = [page_spec(a, f) for a in range(items) for f in range(N_FETCH)]
    resident = pl.Buffered(1)
    dec_shape = (n_dec, N_HEADS, 1, HEAD_DIM)
    return pl.pallas_call(
        functools.partial(_moba_layer_kernel, items),
        grid_spec=pltpu.PrefetchScalarGridSpec(
            num_scalar_prefetch=2,
            grid=(n_seq, n_hg, nblk),
            in_specs=[pl.BlockSpec((None, hg, MOBA_BLOCK, HEAD_DIM), lambda n, h, i, pk, pt: (n, h, i, 0)),
                      pl.BlockSpec((None, hg, None, nblk, MOBA_BLOCK), lambda n, h, i, pk, pt: (n, h, i, 0, 0)),
                      pl.BlockSpec((None, hg, seq, HEAD_DIM), lambda n, h, i, pk, pt: (n, h, 0, 0),
                                   pipeline_mode=resident),
                      pl.BlockSpec((None, hg, nblk, V_ROWS, MOBA_BLOCK), lambda n, h, i, pk, pt: (n, h, 0, 0, 0),
                                   pipeline_mode=resident),
                      dec_rows, dec_rows, dec_rows] + page_specs + page_specs,
            out_specs=[pl.BlockSpec((None, MOBA_BLOCK, hg * HEAD_DIM), lambda n, h, i, pk, pt: (n, i, h)),
                       dec_rows],
            scratch_shapes=[pltpu.VMEM((2 * n_chains, MOBA_BLOCK, LANES), F32),
                            pltpu.VMEM((2 * n_chains, MOBA_BLOCK, LANES), F32),
                            pltpu.VMEM((n_chains, V_ROWS, LANES), F32)]),
        out_shape=[jax.ShapeDtypeStruct((n_seq, seq, D_MODEL), BF16),
                   jax.ShapeDtypeStruct(dec_shape, F32)],
        compiler_params=_params("arbitrary", "arbitrary", "arbitrary"),
        name="moba_layer",
    )(picks, page_table.reshape(-1), q, bias, kb, vt,
      q_s.reshape(dec_shape), k_new.reshape(dec_shape), v_new.reshape(dec_shape),
      *([cache_k3] * len(page_specs)), *([cache_v3] * len(page_specs)))


def _b_out_kernel(final, o_ref, z_ref, x_ref, w_ref, g_ref, y_ref, w_scr):
    _cast_once((w_ref, w_scr))
    v = (o_ref[...].astype(F32) * _silu(z_ref[...])).astype(BF16)
    x = x_ref[...] + _dot(v, w_scr[...])
    y_ref[...] = _rmsnorm(x, g_ref[...]) if final else x


def _b_out(o, z, x, w, g, tm, final):
    t = x.shape[0]
    tile = pl.BlockSpec((tm, D_MODEL), lambda i: (i, 0))
    return pl.pallas_call(
        functools.partial(_b_out_kernel, final),
        grid=(t // tm,),
        in_specs=[tile, tile, tile, _weight_spec((D_MODEL, D_MODEL)), pl.BlockSpec((1, D_MODEL), lambda i: (0, 0))],
        out_specs=tile,
        out_shape=jax.ShapeDtypeStruct((t, D_MODEL), F32),
        scratch_shapes=[pltpu.VMEM((D_MODEL, D_MODEL), BF16)],
        compiler_params=_params("arbitrary"),
        name="b_out",
    )(o, z, x, w, g)


PAGES_PER_STEP = 16


def _page_mean_kernel(*refs):
    pages, out_ref = refs[1:1 + PAGES_PER_STEP], refs[-1]
    per_block = MOBA_BLOCK // PAGE_SIZE
    for b in range(PAGES_PER_STEP // per_block):
        tot = None
        for r in range(per_block):
            page = pages[per_block * b + r][...].reshape(PAGE_SIZE, N_HEADS, HEAD_DIM)
            part = jnp.sum(page, axis=0)
            tot = part if tot is None else tot + part
        tot = tot * (1.0 / MOBA_BLOCK)
        for hd in range(N_HEADS):
            out_ref[hd, b:b + 1, :] = tot[hd:hd + 1, :]


def _page_means(cache_k2, page_table):
    n, n_pages = page_table.shape
    steps = n_pages // PAGES_PER_STEP
    nb = PAGES_PER_STEP * PAGE_SIZE // MOBA_BLOCK

    def page_spec(r):
        return pl.BlockSpec((None, PAGE_SIZE * N_HEADS, HEAD_DIM),
                            lambda s, j, pt, r=r: (pt[s * n_pages + j * PAGES_PER_STEP + r], 0, 0))

    return pl.pallas_call(
        _page_mean_kernel,
        grid_spec=pltpu.PrefetchScalarGridSpec(
            num_scalar_prefetch=1,
            grid=(n, steps),
            in_specs=[page_spec(r) for r in range(PAGES_PER_STEP)],
            out_specs=pl.BlockSpec((None, N_HEADS, nb, HEAD_DIM), lambda s, j, pt: (s, 0, j, 0))),
        out_shape=jax.ShapeDtypeStruct((n, N_HEADS, n_pages * PAGE_SIZE // MOBA_BLOCK, HEAD_DIM), F32),
        compiler_params=_params("parallel", "arbitrary"),
        name="page_means",
    )(page_table.reshape(-1), *([cache_k2] * PAGES_PER_STEP))


def _q_sample_kernel(x_ref, g_ref, w_ref, cos_ref, sin_ref, q_ref, z_ref):
    h = _rmsnorm(x_ref[...], g_ref[...]).astype(BF16)
    qz = _dot(h, w_ref[...].astype(BF16))
    z_ref[...] = qz[:, D_MODEL:]
    cos2, sin2 = cos_ref[...], sin_ref[...]
    for hd in range(N_HEADS):
        sl = slice(hd * HEAD_DIM, (hd + 1) * HEAD_DIM)
        q_ref[:, sl] = _rope(qz[:, sl], cos2, sin2)


def _q_sample(x, g, w, cos2, sin2):
    n = x.shape[0]
    return pl.pallas_call(
        _q_sample_kernel,
        out_shape=[jax.ShapeDtypeStruct((n, D_MODEL), F32)] * 2,
        compiler_params=pltpu.CompilerParams(vmem_limit_bytes=VMEM_LIMIT),
        name="q_sample",
    )(x, g, w, cos2, sin2)


def _pick_kernel(q_ref, kbar_ref, knew_ref, idx_ref):
    n_seq, _, n_past, _ = kbar_ref.shape
    lane_o = lax.broadcasted_iota(jnp.int32, (8, LANES), 1)

    def one_sequence(sq, carry):
        q_row, knew_row = q_ref[pl.ds(sq, 1), :], knew_ref[pl.ds(sq, 1), :]
        scores = None
        for hd in range(N_HEADS):
            sl = slice(hd * HEAD_DIM, (hd + 1) * HEAD_DIM)
            q8 = jnp.broadcast_to(q_row[:, sl], (8, HEAD_DIM))
            own_mean = jnp.broadcast_to(knew_row[:, sl] * (1.0 / MOBA_BLOCK), (8, HEAD_DIM))
            means = jnp.concatenate([kbar_ref[sq, hd], own_mean], axis=0)
            s = _dot_nt(q8, means, precision=lax.Precision.HIGHEST)
            head_row = lax.broadcasted_iota(jnp.int32, s.shape, 0) == hd
            scores = s if scores is None else jnp.where(head_row, s, scores)
        lane = lax.broadcasted_iota(jnp.int32, scores.shape, 1)
        _, picks = _top3(scores, lane < n_past, 1)
        out = jnp.zeros((8, LANES), jnp.int32)
        for r, idx in enumerate(picks):
            out = jnp.where(lane_o == r, idx, out)
        idx_ref[sq] = out
        return carry

    lax.fori_loop(0, n_seq, one_sequence, 0)


def _pick_blocks(q, kbar, knew):
    n = kbar.shape[0]
    return pl.pallas_call(
        _pick_kernel,
        out_shape=jax.ShapeDtypeStruct((n, 8, LANES), jnp.int32),
        compiler_params=pltpu.CompilerParams(vmem_limit_bytes=VMEM_LIMIT),
        name="pick_blocks",
    )(q, kbar, knew)


N_FETCH = MOBA_TOPK * (MOBA_BLOCK // PAGE_SIZE)


def _decode_attend(head, q, k_new, v_new, k_pages, v_pages):
    scale = HEAD_DIM ** -0.5
    q8 = jnp.broadcast_to(q, (8, HEAD_DIM)).astype(BF16)
    head_rows = pl.ds(head, PAGE_SIZE, stride=N_HEADS)
    kc = jnp.concatenate([r[head_rows, :].astype(BF16) for r in k_pages], axis=0)
    vc = jnp.concatenate([r[head_rows, :].astype(BF16) for r in v_pages], axis=0)
    s = _dot_nt(q8, kc)[0:1, :] * scale
    s_new = jnp.sum(q * k_new, axis=1, keepdims=True) * scale
    m = jnp.maximum(jnp.max(s, axis=1, keepdims=True), s_new)
    p = jnp.exp(s - m)
    p_new = jnp.exp(s_new - m)
    l = jnp.sum(p, axis=1, keepdims=True) + p_new
    pv = _dot(jnp.broadcast_to(p, (8, p.shape[1])).astype(BF16), vc)[0:1, :]
    return (pv + p_new * v_new) / l


def _rope_tables(pos):
    half = HEAD_DIM // 2
    inv = ROPE_THETA ** (-jnp.arange(half, dtype=F32) / half)
    ang = pos.astype(F32)[:, None] * inv[None, :]
    cos, sin = jnp.cos(ang), jnp.sin(ang)
    return jnp.concatenate([cos, cos], axis=1), jnp.concatenate([-sin, sin], axis=1)


def kernel(x_prompt, x_sample, state_ssm_re, state_ssm_im, cache_k, cache_v, page_table, a_norm, a_w_in, a_lam_re, a_lam_im, a_log_dt, a_b_re, a_b_im, a_c_re, a_c_im, a_d, a_w_glu, a_w_out, kv_norm, w_kv, b_norm, b_w_in, b_w_out, final_norm):
    n_seq, seq, _ = x_prompt.shape
    n_dec = x_sample.shape[0]
    n_pool = cache_k.shape[0]
    past_len = page_table.shape[1] * PAGE_SIZE
    assert x_sample.shape[1] == 1 and seq % ROW_TILE == 0 and ROW_TILE % MOBA_BLOCK == 0
    assert past_len % (PAGES_PER_STEP * PAGE_SIZE) == 0 and past_len // MOBA_BLOCK >= MOBA_TOPK
    assert seq // MOBA_BLOCK <= LANES and (seq // MOBA_BLOCK) % 8 == 0
    n_a, n_b = a_norm.shape[0], b_norm.shape[0]

    xp = x_prompt.reshape(n_seq * seq, D_MODEL)
    xs = x_sample.reshape(n_dec, D_MODEL)
    cos_p, sin_p = _rope_tables(jnp.arange(seq, dtype=jnp.int32))
    cos_s, sin_s = _rope_tables(jnp.full((n_dec,), past_len, jnp.int32))

    st_p_re, st_p_im, st_s_re, st_s_im = [], [], [], []
    for l in range(n_a):
        prep = _ssm_prep(a_lam_re[l], a_lam_im[l], a_log_dt[l], a_b_re[l], a_b_im[l], a_c_re[l], a_c_im[l],
                         seq // SSM_TBLK)
        norm, d = a_norm[l][None], a_d[l][None]
        w_in, wg, wo = a_w_in[l], a_w_glu[l], a_w_out[l]
        xp, hr, hi = _s5_layer_prompt(xp, n_seq, norm, w_in, prep, d, wg, wo)
        st_p_re.append(hr)
        st_p_im.append(hi)
        xs, hr, hi = _s5_layer_sample(xs, state_ssm_re[l], state_ssm_im[l], norm, w_in, prep, d, wg, wo)
        st_s_re.append(hr)
        st_s_im.append(hi)

    k_p, v_p, kb, vt, kbar_p = _kv_proj(xp, kv_norm[None], w_kv, cos_p, sin_p, n_seq, ROW_TILE, True)
    k_s, v_s = _kv_proj(xs, kv_norm[None], w_kv, cos_s, sin_s, 1, n_dec, False)
    nblk = seq // MOBA_BLOCK
    kbar_p = kbar_p.reshape(n_seq, nblk, N_HEADS, HEAD_DIM).transpose(0, 2, 1, 3)
    cache_k3 = cache_k.reshape(n_pool, PAGE_SIZE * N_HEADS, HEAD_DIM)
    cache_v3 = cache_v.reshape(n_pool, PAGE_SIZE * N_HEADS, HEAD_DIM)
    kbar_s = _page_means(cache_k3, page_table)

    for j in range(n_b):
        w_in, wo = b_w_in[j], b_w_out[j]
        final = j == n_b - 1
        q, bias, z = _q_proj(xp, b_norm[j][None], w_in, cos_p, sin_p, kbar_p, n_seq)
        q_s, z_s = _q_sample(xs, b_norm[j][None], w_in, cos_s, sin_s)
        picks = _pick_blocks(q_s, kbar_s, k_s)[:, :, :MOBA_TOPK].reshape(-1)
        o, o_s = _moba_layer(q, bias, kb, vt, q_s, k_s, v_s, cache_k3, cache_v3, page_table, picks)
        xp = _b_out(o.reshape(n_seq * seq, D_MODEL), z, xp, wo, final_norm[None], ROW_TILE, final)
        xs = _b_out(o_s.reshape(n_dec, D_MODEL), z_s, xs, wo, final_norm[None], n_dec, final)

    return (xp.reshape(n_seq, seq, D_MODEL), xs.reshape(n_dec, 1, D_MODEL),
            jnp.stack(st_p_re), jnp.stack(st_p_im), jnp.stack(st_s_re), jnp.stack(st_s_im),
            k_p.reshape(n_seq, seq, N_HEADS, HEAD_DIM), v_p.reshape(n_seq, seq, N_HEADS, HEAD_DIM),
            k_s.reshape(n_dec, 1, N_HEADS, HEAD_DIM), v_s.reshape(n_dec, 1, N_HEADS, HEAD_DIM))
```

```python
import functools
import math

import jax
import jax.numpy as jnp
from jax import lax
from jax.experimental import pallas as pl
from jax.experimental.pallas import tpu as pltpu

D_MODEL = 1024
SSM_GROUP = 16
SSM_GROUPS = D_MODEL // SSM_GROUP
SSM_STATE = 64
SSM_TBLK = 16
SSM_BLKW = SSM_TBLK * SSM_GROUP
HEAD_DIM = 128
N_HEADS = D_MODEL // HEAD_DIM
MOBA_BLOCK = 256
MOBA_TOPK = 3
PAGE_SIZE = 128
ROPE_THETA = 10000.0
RMS_EPS = 1e-6
NEG_INF = -1e30
LANES = 128
VMEM_LIMIT = 48 * 1024 * 1024
ROW_TILE = 512
MOBA_HEADS = 4
V_ROWS = HEAD_DIM + 16

F32 = jnp.float32
BF16 = jnp.bfloat16


def _params(*sem):
    return pltpu.CompilerParams(dimension_semantics=sem, vmem_limit_bytes=VMEM_LIMIT)


def _rmsnorm(x, g):
    r = lax.rsqrt(jnp.mean(x * x, axis=-1, keepdims=True) + RMS_EPS)
    return x * r * g


def _dot(a, b):
    return jnp.dot(a, b, preferred_element_type=F32)


def _dot_nt(a, b, precision=None):
    return lax.dot_general(a, b, (((1,), (1,)), ((), ())), precision=precision,
                           preferred_element_type=F32)


def _weight_spec(shape):
    return pl.BlockSpec(shape, lambda *_: (0,) * len(shape), pipeline_mode=pl.Buffered(1))


def _cast_once(*pairs):
    @pl.when(pl.program_id(0) == 0)
    def _():
        for w_ref, w_scr in pairs:
            w_scr[...] = w_ref[...].astype(BF16)


def _rope(x, cos2, sin2):
    return x * cos2 + pltpu.roll(x, HEAD_DIM // 2, 1) * sin2


def _silu(z):
    return z * jax.nn.sigmoid(z)


def _top3(s, valid, axis):
    n_lanes = s.shape[axis]
    lane = lax.broadcasted_iota(jnp.int32, s.shape, axis)
    sm = jnp.where(valid, s, NEG_INF)
    sel = jnp.zeros(s.shape, F32)
    picks = []
    for _ in range(MOBA_TOPK):
        m = jnp.max(sm, axis=axis, keepdims=True)
        idx = jnp.min(jnp.where(sm == m, lane, n_lanes), axis=axis, keepdims=True)
        hit = lane == idx
        sel = jnp.where(hit, jnp.where(valid, 1.0, sel), sel)
        sm = jnp.where(hit, NEG_INF, sm)
        picks.append(idx)
    return sel, picks


GROUPS_PER_TILE = LANES // SSM_GROUP
STEP_TILES = SSM_BLKW // LANES


def _slot_transpose(xs, slot):
    xs = list(xs)
    step = GROUPS_PER_TILE // 2
    while step:
        low = (slot & step) == 0
        for i in range(GROUPS_PER_TILE):
            if i & step == 0:
                a, b = xs[i], xs[i + step]
                xs[i] = jnp.where(low, a, pltpu.roll(b, step * SSM_GROUP, 1))
                xs[i + step] = jnp.where(low, pltpu.roll(a, LANES - step * SSM_GROUP, 1), b)
        step //= 2
    return xs


def _to_blocked(u, tiles_ref, ug_ref):
    nb = ug_ref.shape[0]
    slot = lax.broadcasted_iota(jnp.int32, (nb, LANES), 1) // SSM_GROUP
    for tile in range(D_MODEL // LANES):
        tiles_ref[tile] = u[:, tile * LANES:(tile + 1) * LANES]
    for tile in range(D_MODEL // LANES):
        for hh in range(STEP_TILES):
            steps = [tiles_ref[tile, pl.ds(hh * GROUPS_PER_TILE + rp, nb, stride=SSM_TBLK), :]
                     for rp in range(GROUPS_PER_TILE)]
            for gp, piece in enumerate(_slot_transpose(steps, slot)):
                col = (tile * GROUPS_PER_TILE + gp) * SSM_BLKW + hh * LANES
                ug_ref[:, col:col + LANES] = piece.astype(ug_ref.dtype)


def _from_blocked(yg_ref, tiles_ref):
    nb = yg_ref.shape[0]
    slot = lax.broadcasted_iota(jnp.int32, (nb, LANES), 1) // SSM_GROUP
    for tile in range(D_MODEL // LANES):
        for hh in range(STEP_TILES):
            cols = [(tile * GROUPS_PER_TILE + gp) * SSM_BLKW + hh * LANES for gp in range(GROUPS_PER_TILE)]
            for rp, piece in enumerate(_slot_transpose([yg_ref[:, c:c + LANES] for c in cols], slot)):
                tiles_ref[tile, pl.ds(hh * GROUPS_PER_TILE + rp, nb, stride=SSM_TBLK), :] = piece
    return jnp.concatenate([tiles_ref[tile] for tile in range(D_MODEL // LANES)], axis=1)


def _a_in_kernel(blocked, x_ref, g_ref, w_ref, u_ref, z_ref, *rest):
    w_scr = rest[-1]
    _cast_once((w_ref, w_scr))
    h = _rmsnorm(x_ref[...], g_ref[...]).astype(BF16)
    uz = _dot(h, w_scr[...])
    u_ref[...] = uz[:, :D_MODEL]
    z_ref[...] = uz[:, D_MODEL:]
    if blocked:
        ug_ref, tiles_ref = rest[:2]
        _to_blocked(uz[:, :D_MODEL], tiles_ref, ug_ref)


def _a_in(x, g, w, tm, blocked):
    t = x.shape[0]
    tile = pl.BlockSpec((tm, D_MODEL), lambda i: (i, 0))
    out_specs, out_shape = [tile, tile], [jax.ShapeDtypeStruct((t, D_MODEL), F32)] * 2
    if blocked:
        out_specs.append(pl.BlockSpec((tm // SSM_TBLK, SSM_GROUPS * SSM_BLKW), lambda i: (i, 0)))
        out_shape.append(jax.ShapeDtypeStruct((t // SSM_TBLK, SSM_GROUPS * SSM_BLKW), BF16))
    return pl.pallas_call(
        functools.partial(_a_in_kernel, blocked),
        grid=(t // tm,),
        in_specs=[tile, pl.BlockSpec((1, D_MODEL), lambda i: (0, 0)), _weight_spec((D_MODEL, 2 * D_MODEL))],
        out_specs=out_specs,
        out_shape=out_shape,
        scratch_shapes=([pltpu.VMEM((D_MODEL // LANES, tm, LANES), F32)] if blocked else [])
        + [pltpu.VMEM((D_MODEL, 2 * D_MODEL), BF16)],
        compiler_params=_params("arbitrary"),
        name="a_in",
    )(x, g, w)


def _ssm_scan_kernel(n_seq, ug_ref, m_ref, f_ref, e_ref, ap_ref, y_ref, st_ref):
    rows = ug_ref.shape[0]
    nblk = rows // n_seq
    ug = ug_ref[...]
    s = _dot(ug, f_ref[...])
    row = lax.broadcasted_iota(jnp.int32, (nblk, LANES), 0)
    hp = []
    for q in range(n_seq):
        re = s[q * nblk:(q + 1) * nblk, :LANES]
        im = s[q * nblk:(q + 1) * nblk, LANES:]
        step, k = 1, 0
        while step < nblk:
            ar = ap_ref[2 * k:2 * k + 1, :]
            ai = ap_ref[2 * k + 1:2 * k + 2, :]
            sr = jnp.where(row >= step, pltpu.roll(re, step, 0), 0.0)
            si = jnp.where(row >= step, pltpu.roll(im, step, 0), 0.0)
            re, im = re + ar * sr - ai * si, im + ar * si + ai * sr
            step, k = step * 2, k + 1
        st_ref[2 * q:2 * q + 1, :] = re[nblk - 1:nblk, :]
        st_ref[2 * q + 1:2 * q + 2, :] = im[nblk - 1:nblk, :]
        pr = jnp.where(row >= 1, pltpu.roll(re, 1, 0), 0.0)
        pi = jnp.where(row >= 1, pltpu.roll(im, 1, 0), 0.0)
        hp.append(jnp.concatenate([pr, pi], axis=1))
    hprev = jnp.concatenate(hp, axis=0).astype(BF16)
    carry = _dot(hprev, e_ref[...])
    y_ref[:, :SSM_BLKW] = _dot(ug[:, :SSM_BLKW], m_ref[0]) + carry[:, :SSM_BLKW]
    y_ref[:, SSM_BLKW:] = _dot(ug[:, SSM_BLKW:], m_ref[1]) + carry[:, SSM_BLKW:]


def _ssm_scan(ug, prep, n_seq):
    rows = ug.shape[0]
    n_pairs = SSM_GROUPS // 2
    n_pow = prep["apow"].shape[1]
    return pl.pallas_call(
        functools.partial(_ssm_scan_kernel, n_seq),
        grid=(n_pairs,),
        in_specs=[pl.BlockSpec((rows, 2 * SSM_BLKW), lambda p: (0, p)),
                  pl.BlockSpec((2, SSM_BLKW, SSM_BLKW), lambda p: (p, 0, 0)),
                  pl.BlockSpec((None, 2 * SSM_BLKW, 2 * LANES), lambda p: (p, 0, 0)),
                  pl.BlockSpec((None, 2 * LANES, 2 * SSM_BLKW), lambda p: (p, 0, 0)),
                  pl.BlockSpec((None, n_pow, LANES), lambda p: (p, 0, 0))],
        out_specs=[pl.BlockSpec((rows, 2 * SSM_BLKW), lambda p: (0, p)),
                   pl.BlockSpec((None, 2 * n_seq, LANES), lambda p: (p, 0, 0))],
        out_shape=[jax.ShapeDtypeStruct((rows, SSM_GROUPS * SSM_BLKW), F32),
                   jax.ShapeDtypeStruct((n_pairs, 2 * n_seq, LANES), F32)],
        compiler_params=_params("parallel"),
        name="ssm_scan",
    )(ug, prep["m"], prep["f"], prep["e"], prep["apow"])


def _ssm_step_kernel(u_ref, hr_ref, hi_ref, bb_ref, ar_ref, ai_ref, cs_ref, y_ref, or_ref, oi_ref):
    bu = lax.dot_general(u_ref[...], bb_ref[...], (((2,), (1,)), ((0,), (0,))),
                         preferred_element_type=F32)
    h0r, h0i = hr_ref[...], hi_ref[...]
    ar, ai = ar_ref[...], ai_ref[...]
    hr = bu[:, :, :SSM_STATE] + ar * h0r - ai * h0i
    hi = bu[:, :, SSM_STATE:] + ar * h0i + ai * h0r
    or_ref[...] = hr
    oi_ref[...] = hi
    y_ref[...] = lax.dot_general(jnp.concatenate([hr, hi], axis=2), cs_ref[...],
                                 (((2,), (1,)), ((0,), (0,))), preferred_element_type=F32)


def _ssm_step(u_g, h0r, h0i, prep):
    g, n, _ = u_g.shape
    return pl.pallas_call(
        _ssm_step_kernel,
        out_shape=[jax.ShapeDtypeStruct((g, n, SSM_GROUP), F32),
                   jax.ShapeDtypeStruct((g, n, SSM_STATE), F32),
                   jax.ShapeDtypeStruct((g, n, SSM_STATE), F32)],
        compiler_params=pltpu.CompilerParams(vmem_limit_bytes=VMEM_LIMIT),
        name="ssm_step",
    )(u_g, h0r, h0i, prep["bb"], prep["a_re"], prep["a_im"], prep["cs"])


def _a_out_kernel(blocked, ys_ref, u_ref, z_ref, x_ref, d_ref, wg_ref, wo_ref, o_ref, *scratch):
    wg_scr, wo_scr = scratch[-2:]
    _cast_once((wg_ref, wg_scr), (wo_ref, wo_scr))
    ys = _from_blocked(ys_ref, scratch[0]) if blocked else ys_ref[...]
    y = ys + d_ref[...] * u_ref[...]
    g = jax.nn.gelu(y)
    y2 = g * jax.nn.sigmoid(_dot(g.astype(BF16), wg_scr[...]))
    v = (y2 * _silu(z_ref[...])).astype(BF16)
    o_ref[...] = x_ref[...] + _dot(v, wo_scr[...])


def _a_out(ys, u, z, x, d, wg, wo, tm, blocked):
    t = x.shape[0]
    tile = pl.BlockSpec((tm, D_MODEL), lambda i: (i, 0))
    full = _weight_spec((D_MODEL, D_MODEL))
    ys_spec = pl.BlockSpec((tm // SSM_TBLK, SSM_GROUPS * SSM_BLKW), lambda i: (i, 0)) if blocked else tile
    return pl.pallas_call(
        functools.partial(_a_out_kernel, blocked),
        grid=(t // tm,),
        in_specs=[ys_spec, tile, tile, tile, pl.BlockSpec((1, D_MODEL), lambda i: (0, 0)), full, full],
        out_specs=tile,
        out_shape=jax.ShapeDtypeStruct((t, D_MODEL), F32),
        scratch_shapes=([pltpu.VMEM((D_MODEL // LANES, tm, LANES), F32)] if blocked else [])
        + [pltpu.VMEM((D_MODEL, D_MODEL), BF16)] * 2,
        compiler_params=_params("arbitrary"),
        name="a_out",
    )(ys, u, z, x, d, wg, wo)


def _ssm_prep_kernel(n_steps, lrc_ref, lic_ref, lrr_ref, lir_ref, ldt_ref, bt_re_ref, bt_im_ref, ct_re_ref, ct_im_ref,
                     m_ref, f_ref, e_ref, ap_ref, ab_ref, bb_ref, cs_ref, m32, f32s, e32):
    hi = lax.Precision.HIGHEST
    p = SSM_STATE
    m32[...] = jnp.zeros(m32.shape, F32)
    f32s[...] = jnp.zeros(f32s.shape, F32)
    e32[...] = jnp.zeros(e32.shape, F32)
    for gi in range(2):
        dt = jnp.exp(ldt_ref[gi])

        def discretise(lr, li):
            mag = jnp.exp(lr * dt)
            a_re, a_im = mag * jnp.cos(li * dt), mag * jnp.sin(li * dt)
            den = lr * lr + li * li
            f_re = ((a_re - 1.0) * lr + a_im * li) / den
            f_im = (a_im * lr - (a_re - 1.0) * li) / den
            return a_re, a_im, f_re, f_im

        ac_re, ac_im, _, _ = discretise(lrc_ref[gi], lic_ref[gi])
        lrr, lir = lrr_ref[gi], lir_ref[gi]
        ar_re, ar_im, fr_re, fr_im = discretise(lrr, lir)
        ab_ref[gi, 0:1, :] = ar_re
        ab_ref[gi, 1:2, :] = ar_im
        ct_re, ct_im = ct_re_ref[gi], ct_im_ref[gi]
        cs_ref[gi] = jnp.concatenate([ct_re, -ct_im], axis=0)
        bbt_re = fr_re * bt_re_ref[gi] - fr_im * bt_im_ref[gi]
        bbt_im = fr_re * bt_im_ref[gi] + fr_im * bt_re_ref[gi]
        bb_ref[gi, :, :p] = bbt_re
        bb_ref[gi, :, p:] = bbt_im
        for k in range(n_steps):
            n = float(SSM_TBLK * 2 ** k)
            mag = jnp.exp(lrr * dt * n)
            ap_ref[2 * k:2 * k + 1, gi * p:(gi + 1) * p] = mag * jnp.cos(lir * dt * n)
            ap_ref[2 * k + 1:2 * k + 2, gi * p:(gi + 1) * p] = mag * jnp.sin(lir * dt * n)
        pc_re, pc_im = jnp.ones_like(ac_re), jnp.zeros_like(ac_im)
        pr_re, pr_im = jnp.ones_like(ar_re), jnp.zeros_like(ar_im)
        for t in range(SSM_TBLK + 1):
            if t:
                pc_re, pc_im = pc_re * ac_re - pc_im * ac_im, pc_re * ac_im + pc_im * ac_re
                pr_re, pr_im = pr_re * ar_re - pr_im * ar_im, pr_re * ar_im + pr_im * ar_re
            ca_re = pc_re * ct_re - pc_im * ct_im
            nca_im = -(pc_re * ct_im + pc_im * ct_re)
            if t:
                cols = slice(gi * SSM_BLKW + (t - 1) * SSM_GROUP, gi * SSM_BLKW + t * SSM_GROUP)
                e32[gi * p:(gi + 1) * p, cols] = ca_re
                e32[(2 + gi) * p:(3 + gi) * p, cols] = nca_im
            if t < SSM_TBLK:
                r = SSM_TBLK - 1 - t
                rows = slice(gi * SSM_BLKW + r * SSM_GROUP, gi * SSM_BLKW + (r + 1) * SSM_GROUP)
                f32s[rows, gi * p:(gi + 1) * p] = pr_re * bbt_re - pr_im * bbt_im
                f32s[rows, (2 + gi) * p:(3 + gi) * p] = pr_re * bbt_im + pr_im * bbt_re
                kt = (jnp.dot(bbt_re, ca_re, precision=hi, preferred_element_type=F32)
                      + jnp.dot(bbt_im, nca_im, precision=hi, preferred_element_type=F32))
                for r_in in range(SSM_TBLK - t):
                    r_out = r_in + t
                    m32[gi, r_in * SSM_GROUP:(r_in + 1) * SSM_GROUP, r_out * SSM_GROUP:(r_out + 1) * SSM_GROUP] = kt
    m_ref[...] = m32[...].astype(BF16)
    f_ref[...] = f32s[...].astype(BF16)
    e_ref[...] = e32[...].astype(BF16)


def _ssm_prep(lam_re, lam_im, log_dt, b_re, b_im, c_re, c_im, nblk):
    g, p = lam_re.shape
    n_pairs = g // 2
    n_steps = max(1, math.ceil(math.log2(nblk)))
    col = lambda x: x.reshape(g, p, 1)
    row = lambda x: x.reshape(g, 1, p)
    tr = lambda x: x.transpose(0, 2, 1)

    def two(*shape):
        return pl.BlockSpec((2,) + shape, lambda i: (i,) + (0,) * len(shape))

    def one(*shape):
        return pl.BlockSpec((None,) + shape, lambda i: (i,) + (0,) * len(shape))

    m, fp, ep, ap, ab, bb, cs = pl.pallas_call(
        functools.partial(_ssm_prep_kernel, n_steps),
        grid=(n_pairs,),
        in_specs=[two(p, 1), two(p, 1), two(1, p), two(1, p), two(1, 1),
                  two(SSM_GROUP, p), two(SSM_GROUP, p), two(p, SSM_GROUP), two(p, SSM_GROUP)],
        out_specs=[two(SSM_BLKW, SSM_BLKW), one(2 * SSM_BLKW, 4 * p), one(4 * p, 2 * SSM_BLKW),
                   one(2 * n_steps, 2 * p), two(2, p), two(SSM_GROUP, 2 * p), two(2 * p, SSM_GROUP)],
        out_shape=[jax.ShapeDtypeStruct((g, SSM_BLKW, SSM_BLKW), BF16),
                   jax.ShapeDtypeStruct((n_pairs, 2 * SSM_BLKW, 4 * p), BF16),
                   jax.ShapeDtypeStruct((n_pairs, 4 * p, 2 * SSM_BLKW), BF16),
                   jax.ShapeDtypeStruct((n_pairs, 2 * n_steps, 2 * p), F32),
                   jax.ShapeDtypeStruct((g, 2, p), F32),
                   jax.ShapeDtypeStruct((g, SSM_GROUP, 2 * p), F32),
                   jax.ShapeDtypeStruct((g, 2 * p, SSM_GROUP), F32)],
        scratch_shapes=[pltpu.VMEM((2, SSM_BLKW, SSM_BLKW), F32), pltpu.VMEM((2 * SSM_BLKW, 4 * p), F32),
                        pltpu.VMEM((4 * p, 2 * SSM_BLKW), F32)],
        compiler_params=_params("parallel"),
        name="ssm_prep",
    )(col(lam_re), col(lam_im), row(lam_re), row(lam_im), log_dt.reshape(g, 1, 1),
      tr(b_re), tr(b_im), tr(c_re), tr(c_im))
    return {"m": m, "f": fp, "e": ep, "apow": ap, "bb": bb,
            "a_re": ab[:, 0:1, :], "a_im": ab[:, 1:2, :], "cs": cs}


def _s5_layer_prompt(x, n_seq, norm, w_in, prep, d, wg, wo):
    u, z, ug = _a_in(x, norm, w_in, ROW_TILE, True)
    yg, st = _ssm_scan(ug, prep, n_seq)
    st = st.reshape(SSM_GROUPS // 2, n_seq, 2, 2, SSM_STATE)
    st = st.transpose(2, 1, 0, 3, 4).reshape(2, n_seq, SSM_GROUPS, SSM_STATE)
    return _a_out(yg, u, z, x, d, wg, wo, ROW_TILE, True), st[0], st[1]


def _s5_layer_sample(x, h0r, h0i, norm, w_in, prep, d, wg, wo):
    n = x.shape[0]
    u, z = _a_in(x, norm, w_in, n, False)
    u_g = u.reshape(n, SSM_GROUPS, SSM_GROUP).transpose(1, 0, 2)
    y_g, hr, hi = _ssm_step(u_g, h0r.transpose(1, 0, 2), h0i.transpose(1, 0, 2), prep)
    ys = y_g.transpose(1, 0, 2).reshape(n, D_MODEL)
    return _a_out(ys, u, z, x, d, wg, wo, n, False), hr.transpose(1, 0, 2), hi.transpose(1, 0, 2)


def _kv_kernel(prompt, x_ref, g_ref, w_ref, cos_ref, sin_ref, k_ref, v_ref, *rest):
    w_scr = rest[-1]
    _cast_once((w_ref, w_scr))
    h = _rmsnorm(x_ref[...], g_ref[...]).astype(BF16)
    kv = _dot(h, w_scr[...])
    cos2, sin2 = cos_ref[...], sin_ref[...]
    tm = kv.shape[0]
    for hd in range(N_HEADS):
        sl = slice(hd * HEAD_DIM, (hd + 1) * HEAD_DIM)
        k = _rope(kv[:, sl], cos2, sin2)
        v = kv[:, D_MODEL + hd * HEAD_DIM:D_MODEL + (hd + 1) * HEAD_DIM]
        k_ref[:, sl] = k
        v_ref[:, sl] = v
        if prompt:
            kb_ref, vt_ref, mean_ref = rest[:3]
            kb_ref[hd] = k.astype(BF16)
            for b in range(tm // MOBA_BLOCK):
                rows = slice(b * MOBA_BLOCK, (b + 1) * MOBA_BLOCK)
                mean_ref[b, :, sl] = jnp.sum(k[rows], axis=0, keepdims=True) * (1.0 / MOBA_BLOCK)
                vt_ref[hd, b, :HEAD_DIM, :] = v[rows].T.astype(BF16)
                vt_ref[hd, b, HEAD_DIM:, :] = jnp.ones((V_ROWS - HEAD_DIM, MOBA_BLOCK), BF16)


def _kv_proj(x, g, w, cos2, sin2, n_seq, tm, prompt):
    t = x.shape[0]
    per_seq = t // n_seq // tm
    tile = pl.BlockSpec((tm, D_MODEL), lambda i: (i, 0))
    rope_tile = pl.BlockSpec((tm, HEAD_DIM), lambda i: (i % per_seq, 0))
    out_specs = [tile, tile]
    out_shape = [jax.ShapeDtypeStruct((t, D_MODEL), F32)] * 2
    if prompt:
        nb = tm // MOBA_BLOCK
        out_specs += [
            pl.BlockSpec((None, N_HEADS, tm, HEAD_DIM), lambda i: (i // per_seq, 0, i % per_seq, 0)),
            pl.BlockSpec((None, N_HEADS, nb, V_ROWS, MOBA_BLOCK), lambda i: (i // per_seq, 0, i % per_seq, 0, 0)),
            pl.BlockSpec((nb, 1, D_MODEL), lambda i: (i, 0, 0))]
        out_shape += [
            jax.ShapeDtypeStruct((n_seq, N_HEADS, t // n_seq, HEAD_DIM), BF16),
            jax.ShapeDtypeStruct((n_seq, N_HEADS, t // n_seq // MOBA_BLOCK, V_ROWS, MOBA_BLOCK), BF16),
            jax.ShapeDtypeStruct((t // MOBA_BLOCK, 1, D_MODEL), F32)]
    return pl.pallas_call(
        functools.partial(_kv_kernel, prompt),
        grid=(t // tm,),
        in_specs=[tile, pl.BlockSpec((1, D_MODEL), lambda i: (0, 0)), _weight_spec((D_MODEL, 2 * D_MODEL)),
                  rope_tile, rope_tile],
        out_specs=out_specs,
        out_shape=out_shape,
        scratch_shapes=[pltpu.VMEM((D_MODEL, 2 * D_MODEL), BF16)],
        compiler_params=_params("arbitrary"),
        name="kv_proj",
    )(x, g, w, cos2, sin2)


def _q_kernel(blocks_per_seq, x_ref, g_ref, w_ref, cos_ref, sin_ref, kbar_ref, q_ref, bias_ref, z_ref, w_scr):
    _cast_once((w_ref, w_scr))
    tm = x_ref.shape[0]
    per_tile = tm // MOBA_BLOCK
    first = (pl.program_id(0) * per_tile) % blocks_per_seq
    h = _rmsnorm(x_ref[...], g_ref[...]).astype(BF16)
    qz = _dot(h, w_scr[...])
    z_ref[...] = qz[:, D_MODEL:]
    cos2, sin2 = cos_ref[...], sin_ref[...]
    blk = lax.broadcasted_iota(jnp.int32, (blocks_per_seq, MOBA_BLOCK), 0)
    for hd in range(N_HEADS):
        q = _rope(qz[:, hd * HEAD_DIM:(hd + 1) * HEAD_DIM], cos2, sin2)
        q_ref[hd] = (q * (HEAD_DIM ** -0.5)).astype(BF16)
        for b in range(per_tile):
            rows = slice(b * MOBA_BLOCK, (b + 1) * MOBA_BLOCK)
            s = _dot_nt(kbar_ref[hd], q[rows], precision=lax.Precision.HIGHEST)
            sel, _ = _top3(s, blk < first + b, 0)
            bias_ref[hd, b] = jnp.where(sel > 0.5, 0.0, NEG_INF)


def _q_proj(x, g, w, cos2, sin2, kbar, n_seq):
    t = x.shape[0]
    tm = ROW_TILE
    bps = t // n_seq // MOBA_BLOCK
    per_tile = tm // MOBA_BLOCK
    tps = bps // per_tile
    tile = pl.BlockSpec((tm, D_MODEL), lambda i: (i, 0))
    rope_tile = pl.BlockSpec((tm, HEAD_DIM), lambda i: (i % tps, 0))
    return pl.pallas_call(
        functools.partial(_q_kernel, bps),
        grid=(t // tm,),
        in_specs=[tile, pl.BlockSpec((1, D_MODEL), lambda i: (0, 0)), _weight_spec((D_MODEL, 2 * D_MODEL)),
                  rope_tile, rope_tile,
                  pl.BlockSpec((None, N_HEADS, bps, HEAD_DIM), lambda i: (i // tps, 0, 0, 0))],
        out_specs=[pl.BlockSpec((None, N_HEADS, tm, HEAD_DIM), lambda i: (i // tps, 0, i % tps, 0)),
                   pl.BlockSpec((None, N_HEADS, per_tile, bps, MOBA_BLOCK), lambda i: (i // tps, 0, i % tps, 0, 0)),
                   tile],
        out_shape=[jax.ShapeDtypeStruct((n_seq, N_HEADS, t // n_seq, HEAD_DIM), BF16),
                   jax.ShapeDtypeStruct((n_seq, N_HEADS, bps, bps, MOBA_BLOCK), F32),
                   jax.ShapeDtypeStruct((t, D_MODEL), F32)],
        scratch_shapes=[pltpu.VMEM((D_MODEL, 2 * D_MODEL), BF16)],
        compiler_params=_params("arbitrary"),
        name="q_proj",
    )(x, g, w, cos2, sin2, kbar)


def _moba_layer_kernel(items, pk_ref, pt_ref, q_ref, bias_ref, k_ref, vt_ref, qs_ref, kn_ref, vn_ref, *rest):
    n_pages = items * N_FETCH
    k_pages, v_pages = rest[:n_pages], rest[n_pages:2 * n_pages]
    o_ref, os_ref, s_a, s_b, acc_scr = rest[2 * n_pages:]
    step = (pl.program_id(0) * pl.num_programs(1) + pl.program_id(1)) * pl.num_programs(2) + pl.program_id(2)
    head0 = (step * items) % N_HEADS
    for a in range(items):
        pages = slice(a * N_FETCH, (a + 1) * N_FETCH)
        os_ref[a] = _decode_attend(head0 + a, qs_ref[a], kn_ref[a], vn_ref[a], k_pages[pages], v_pages[pages])
    _moba_tile(q_ref, bias_ref, k_ref, vt_ref, o_ref, s_a, s_b, acc_scr)


def _moba_tile(q_ref, bias_ref, k_ref, vt_ref, o_ref, s_a, s_b, acc_scr):
    i = pl.program_id(2)
    nblk = bias_ref.shape[1]
    row = lax.broadcasted_iota(jnp.int32, (MOBA_BLOCK, LANES), 0)
    col = lax.broadcasted_iota(jnp.int32, (MOBA_BLOCK, LANES), 1)
    chains = [(h, c) for h in range(MOBA_HEADS) for c in range(MOBA_BLOCK // LANES)]
    nc = len(chains)
    trips = (i + 1) // 2
    last_trip = nblk // 2 - 1

    def scores(h, c, blk):
        off = pl.multiple_of(blk * MOBA_BLOCK, MOBA_BLOCK)
        return _dot_nt(k_ref[h, pl.ds(off, MOBA_BLOCK), :], q_ref[h, c * LANES:(c + 1) * LANES, :])

    def stage_scores(t, buf):
        mbs = []
        for n, (h, c) in enumerate(chains):
            for u in range(2):
                s = scores(h, c, 2 * t + u)
                buf[u * nc + n] = s
                mbs.append(jnp.max(s, axis=0, keepdims=True))
        return tuple(mbs)

    def consume(t, buf, ms, mbs):
        out = []
        for n, (h, c) in enumerate(chains):
            b = [bias_ref[h, pl.ds(2 * t + u, 1), :][:, c * LANES:(c + 1) * LANES] for u in range(2)]
            m_new = jnp.maximum(ms[n], jnp.maximum(mbs[2 * n] + b[0], mbs[2 * n + 1] + b[1]))
            alpha = jnp.exp(ms[n] - m_new)
            p = [jnp.exp(buf[u * nc + n] - (m_new - b[u])).astype(BF16) for u in range(2)]
            acc_scr[n] = alpha * acc_scr[n] + (_dot(vt_ref[h, 2 * t], p[0]) + _dot(vt_ref[h, 2 * t + 1], p[1]))
            out.append(m_new)
        return tuple(out)

    for n, (h, c) in enumerate(chains):
        s_b[n] = scores(h, c, i)
    mbs_a = stage_scores(0, s_a)
    ms = []
    for n, (h, c) in enumerate(chains):
        s = jnp.where(row <= col + c * LANES, s_b[n], NEG_INF)
        m = jnp.max(s, axis=0, keepdims=True)
        acc_scr[n] = _dot(vt_ref[h, i], jnp.exp(s - m).astype(BF16))
        ms.append(m)

    def body(tt, carry):
        ms, mbs_a = carry
        t0 = 2 * tt
        mbs_b = stage_scores(t0 + 1, s_b)
        ms = consume(t0, s_a, ms, mbs_a)
        mbs_a = stage_scores(jnp.minimum(t0 + 2, last_trip), s_a)
        ms = consume(t0 + 1, s_b, ms, mbs_b)
        return ms, mbs_a

    lax.fori_loop(0, (trips + 1) // 2, body, (tuple(ms), mbs_a))
    for n, (h, c) in enumerate(chains):
        acc = acc_scr[n]
        o = acc[:HEAD_DIM, :] / acc[HEAD_DIM:HEAD_DIM + 1, :]
        o_ref[c * LANES:(c + 1) * LANES, h * HEAD_DIM:(h + 1) * HEAD_DIM] = o.T.astype(o_ref.dtype)


def _moba_layer(q, bias, kb, vt, q_s, k_new, v_new, cache_k3, cache_v3, page_table, picks):
    n_seq, _, seq, _ = q.shape
    n_dec, n_pages = page_table.shape
    nblk = seq // MOBA_BLOCK
    hg = MOBA_HEADS
    n_hg = N_HEADS // hg
    n_chains = hg * (MOBA_BLOCK // LANES)
    steps = n_seq * n_hg * nblk
    items = n_dec * N_HEADS // steps
    assert items * steps == n_dec * N_HEADS and N_HEADS % items == 0
    per_block = MOBA_BLOCK // PAGE_SIZE

    def first_item(n, h, i):
        return ((n * n_hg + h) * nblk + i) * items

    def dec_index(n, h, i, pk, pt):
        e = first_item(n, h, i)
        return (e // N_HEADS, (e % N_HEADS) // items, 0, 0)

    def page_spec(a, f):
        def index(n, h, i, pk, pt):
            e = first_item(n, h, i) + a
            blk = pk[e * MOBA_TOPK + f // per_block]
            return (pt[(e // N_HEADS) * n_pages + blk * per_block + f % per_block], 0, 0)
        return pl.BlockSpec((None, PAGE_SIZE * N_HEADS, HEAD_DIM), index)

    dec_rows = pl.BlockSpec((None, items, 1, HEAD_DIM), dec_index)
    page_specs = [page_spec(a, f) for a in range(items) for f in range(N_FETCH)]
    resident = pl.Buffered(1)
    dec_shape = (n_dec, N_HEADS, 1, HEAD_DIM)
    return pl.pallas_call(
        functools.partial(_moba_layer_kernel, items),
        grid_spec=pltpu.PrefetchScalarGridSpec(
            num_scalar_prefetch=2,
            grid=(n_seq, n_hg, nblk),
            in_specs=[pl.BlockSpec((None, hg, MOBA_BLOCK, HEAD_DIM), lambda n, h, i, pk, pt: (n, h, i, 0)),
                      pl.BlockSpec((None, hg, None, nblk, MOBA_BLOCK), lambda n, h, i, pk, pt: (n, h, i, 0, 0)),
                      pl.BlockSpec((None, hg, seq, HEAD_DIM), lambda n, h, i, pk, pt: (n, h, 0, 0),
                                   pipeline_mode=resident),
                      pl.BlockSpec((None, hg, nblk, V_ROWS, MOBA_BLOCK), lambda n, h, i, pk, pt: (n, h, 0, 0, 0),
                                   pipeline_mode=resident),
                      dec_rows, dec_rows, dec_rows] + page_specs + page_specs,
            out_specs=[pl.BlockSpec((None, MOBA_BLOCK, hg * HEAD_DIM), lambda n, h, i, pk, pt: (n, i, h)),
                       dec_rows],
            scratch_shapes=[pltpu.VMEM((2 * n_chains, MOBA_BLOCK, LANES), F32),
                            pltpu.VMEM((2 * n_chains, MOBA_BLOCK, LANES), F32),
                            pltpu.VMEM((n_chains, V_ROWS, LANES), F32)]),
        out_shape=[jax.ShapeDtypeStruct((n_seq, seq, D_MODEL), BF16),
                   jax.ShapeDtypeStruct(dec_shape, F32)],
        compiler_params=_params("arbitrary", "arbitrary", "arbitrary"),
        name="moba_layer",
    )(picks, page_table.reshape(-1), q, bias, kb, vt,
      q_s.reshape(dec_shape), k_new.reshape(dec_shape), v_new.reshape(dec_shape),
      *([cache_k3] * len(page_specs)), *([cache_v3] * len(page_specs)))


def _b_out_kernel(final, o_ref, z_ref, x_ref, w_ref, g_ref, y_ref, w_scr):
    _cast_once((w_ref, w_scr))
    v = (o_ref[...].astype(F32) * _silu(z_ref[...])).astype(BF16)
    x = x_ref[...] + _dot(v, w_scr[...])
    y_ref[...] = _rmsnorm(x, g_ref[...]) if final else x


def _b_out(o, z, x, w, g, tm, final):
    t = x.shape[0]
    tile = pl.BlockSpec((tm, D_MODEL), lambda i: (i, 0))
    return pl.pallas_call(
        functools.partial(_b_out_kernel, final),
        grid=(t // tm,),
        in_specs=[tile, tile, tile, _weight_spec((D_MODEL, D_MODEL)), pl.BlockSpec((1, D_MODEL), lambda i: (0, 0))],
        out_specs=tile,
        out_shape=jax.ShapeDtypeStruct((t, D_MODEL), F32),
        scratch_shapes=[pltpu.VMEM((D_MODEL, D_MODEL), BF16)],
        compiler_params=_params("arbitrary"),
        name="b_out",
    )(o, z, x, w, g)


PAGES_PER_STEP = 16


def _page_mean_kernel(*refs):
    pages, out_ref = refs[1:1 + PAGES_PER_STEP], refs[-1]
    per_block = MOBA_BLOCK // PAGE_SIZE
    for b in range(PAGES_PER_STEP // per_block):
        tot = None
        for r in range(per_block):
            page = pages[per_block * b + r][...].reshape(PAGE_SIZE, N_HEADS, HEAD_DIM)
            part = jnp.sum(page, axis=0)
            tot = part if tot is None else tot + part
        tot = tot * (1.0 / MOBA_BLOCK)
        for hd in range(N_HEADS):
            out_ref[hd, b:b + 1, :] = tot[hd:hd + 1, :]


def _page_means(cache_k2, page_table):
    n, n_pages = page_table.shape
    steps = n_pages // PAGES_PER_STEP
    nb = PAGES_PER_STEP * PAGE_SIZE // MOBA_BLOCK

    def page_spec(r):
        return pl.BlockSpec((None, PAGE_SIZE * N_HEADS, HEAD_DIM),
                            lambda s, j, pt, r=r: (pt[s * n_pages + j * PAGES_PER_STEP + r], 0, 0))

    return pl.pallas_call(
        _page_mean_kernel,
        grid_spec=pltpu.PrefetchScalarGridSpec(
            num_scalar_prefetch=1,
            grid=(n, steps),
            in_specs=[page_spec(r) for r in range(PAGES_PER_STEP)],
            out_specs=pl.BlockSpec((None, N_HEADS, nb, HEAD_DIM), lambda s, j, pt: (s, 0, j, 0))),
        out_shape=jax.ShapeDtypeStruct((n, N_HEADS, n_pages * PAGE_SIZE // MOBA_BLOCK, HEAD_DIM), F32),
        compiler_params=_params("parallel", "arbitrary"),
        name="page_means",
    )(page_table.reshape(-1), *([cache_k2] * PAGES_PER_STEP))


def _q_sample_kernel(x_ref, g_ref, w_ref, cos_ref, sin_ref, q_ref, z_ref):
    h = _rmsnorm(x_ref[...], g_ref[...]).astype(BF16)
    qz = _dot(h, w_ref[...].astype(BF16))
    z_ref[...] = qz[:, D_MODEL:]
    cos2, sin2 = cos_ref[...], sin_ref[...]
    for hd in range(N_HEADS):
        sl = slice(hd * HEAD_DIM, (hd + 1) * HEAD_DIM)
        q_ref[:, sl] = _rope(qz[:, sl], cos2, sin2)


def _q_sample(x, g, w, cos2, sin2):
    n = x.shape[0]
    return pl.pallas_call(
        _q_sample_kernel,
        out_shape=[jax.ShapeDtypeStruct((n, D_MODEL), F32)] * 2,
        compiler_params=pltpu.CompilerParams(vmem_limit_bytes=VMEM_LIMIT),
        name="q_sample",
    )(x, g, w, cos2, sin2)


def _pick_kernel(q_ref, kbar_ref, knew_ref, idx_ref):
    n_seq, _, n_past, _ = kbar_ref.shape
    lane_o = lax.broadcasted_iota(jnp.int32, (8, LANES), 1)

    def one_sequence(sq, carry):
        q_row, knew_row = q_ref[pl.ds(sq, 1), :], knew_ref[pl.ds(sq, 1), :]
        scores = None
        for hd in range(N_HEADS):
            sl = slice(hd * HEAD_DIM, (hd + 1) * HEAD_DIM)
            q8 = jnp.broadcast_to(q_row[:, sl], (8, HEAD_DIM))
            own_mean = jnp.broadcast_to(knew_row[:, sl] * (1.0 / MOBA_BLOCK), (8, HEAD_DIM))
            means = jnp.concatenate([kbar_ref[sq, hd], own_mean], axis=0)
            s = _dot_nt(q8, means, precision=lax.Precision.HIGHEST)
            head_row = lax.broadcasted_iota(jnp.int32, s.shape, 0) == hd
            scores = s if scores is None else jnp.where(head_row, s, scores)
        lane = lax.broadcasted_iota(jnp.int32, scores.shape, 1)
        _, picks = _top3(scores, lane < n_past, 1)
        out = jnp.zeros((8, LANES), jnp.int32)
        for r, idx in enumerate(picks):
            out = jnp.where(lane_o == r, idx, out)
        idx_ref[sq] = out
        return carry

    lax.fori_loop(0, n_seq, one_sequence, 0)


def _pick_blocks(q, kbar, knew):
    n = kbar.shape[0]
    return pl.pallas_call(
        _pick_kernel,
        out_shape=jax.ShapeDtypeStruct((n, 8, LANES), jnp.int32),
        compiler_params=pltpu.CompilerParams(vmem_limit_bytes=VMEM_LIMIT),
        name="pick_blocks",
    )(q, kbar, knew)


N_FETCH = MOBA_TOPK * (MOBA_BLOCK // PAGE_SIZE)


def _decode_attend(head, q, k_new, v_new, k_pages, v_pages):
    scale = HEAD_DIM ** -0.5
    q8 = jnp.broadcast_to(q, (8, HEAD_DIM)).astype(BF16)
    head_rows = pl.ds(head, PAGE_SIZE, stride=N_HEADS)
    kc = jnp.concatenate([r[head_rows, :].astype(BF16) for r in k_pages], axis=0)
    vc = jnp.concatenate([r[head_rows, :].astype(BF16) for r in v_pages], axis=0)
    s = _dot_nt(q8, kc)[0:1, :] * scale
    s_new = jnp.sum(q * k_new, axis=1, keepdims=True) * scale
    m = jnp.maximum(jnp.max(s, axis=1, keepdims=True), s_new)
    p = jnp.exp(s - m)
    p_new = jnp.exp(s_new - m)
    l = jnp.sum(p, axis=1, keepdims=True) + p_new
    pv = _dot(jnp.broadcast_to(p, (8, p.shape[1])).astype(BF16), vc)[0:1, :]
    return (pv + p_new * v_new) / l


def _rope_tables(pos):
    half = HEAD_DIM // 2
    inv = ROPE_THETA ** (-jnp.arange(half, dtype=F32) / half)
    ang = pos.astype(F32)[:, None] * inv[None, :]
    cos, sin = jnp.cos(ang), jnp.sin(ang)
    return jnp.concatenate([cos, cos], axis=1), jnp.concatenate([-sin, sin], axis=1)


def kernel(x_prompt, x_sample, state_ssm_re, state_ssm_im, cache_k, cache_v, page_table, a_norm, a_w_in, a_lam_re, a_lam_im, a_log_dt, a_b_re, a_b_im, a_c_re, a_c_im, a_d, a_w_glu, a_w_out, kv_norm, w_kv, b_norm, b_w_in, b_w_out, final_norm):
    n_seq, seq, _ = x_prompt.shape
    n_dec = x_sample.shape[0]
    n_pool = cache_k.shape[0]
    past_len = page_table.shape[1] * PAGE_SIZE
    assert x_sample.shape[1] == 1 and seq % ROW_TILE == 0 and ROW_TILE % MOBA_BLOCK == 0
    assert past_len % (PAGES_PER_STEP * PAGE_SIZE) == 0 and past_len // MOBA_BLOCK >= MOBA_TOPK
    assert seq // MOBA_BLOCK <= LANES and (seq // MOBA_BLOCK) % 8 == 0
    n_a, n_b = a_norm.shape[0], b_norm.shape[0]

    xp = x_prompt.reshape(n_seq * seq, D_MODEL)
    xs = x_sample.reshape(n_dec, D_MODEL)
    cos_p, sin_p = _rope_tables(jnp.arange(seq, dtype=jnp.int32))
    cos_s, sin_s = _rope_tables(jnp.full((n_dec,), past_len, jnp.int32))

    st_p_re, st_p_im, st_s_re, st_s_im = [], [], [], []
    for l in range(n_a):
        prep = _ssm_prep(a_lam_re[l], a_lam_im[l], a_log_dt[l], a_b_re[l], a_b_im[l], a_c_re[l], a_c_im[l],
                         seq // SSM_TBLK)
        norm, d = a_norm[l][None], a_d[l][None]
        w_in, wg, wo = a_w_in[l], a_w_glu[l], a_w_out[l]
        xp, hr, hi = _s5_layer_prompt(xp, n_seq, norm, w_in, prep, d, wg, wo)
        st_p_re.append(hr)
        st_p_im.append(hi)
        xs, hr, hi = _s5_layer_sample(xs, state_ssm_re[l], state_ssm_im[l], norm, w_in, prep, d, wg, wo)
        st_s_re.append(hr)
        st_s_im.append(hi)

    k_p, v_p, kb, vt, kbar_p = _kv_proj(xp, kv_norm[None], w_kv, cos_p, sin_p, n_seq, ROW_TILE, True)
    k_s, v_s = _kv_proj(xs, kv_norm[None], w_kv, cos_s, sin_s, 1, n_dec, False)
    nblk = seq // MOBA_BLOCK
    kbar_p = kbar_p.reshape(n_seq, nblk, N_HEADS, HEAD_DIM).transpose(0, 2, 1, 3)
    cache_k3 = cache_k.reshape(n_pool, PAGE_SIZE * N_HEADS, HEAD_DIM)
    cache_v3 = cache_v.reshape(n_pool, PAGE_SIZE * N_HEADS, HEAD_DIM)
    kbar_s = _page_means(cache_k3, page_table)

    for j in range(n_b):
        w_in, wo = b_w_in[j], b_w_out[j]
        final = j == n_b - 1
        q, bias, z = _q_proj(xp, b_norm[j][None], w_in, cos_p, sin_p, kbar_p, n_seq)
        q_s, z_s = _q_sample(xs, b_norm[j][None], w_in, cos_s, sin_s)
        picks = _pick_blocks(q_s, kbar_s, k_s)[:, :, :MOBA_TOPK].reshape(-1)
        o, o_s = _moba_layer(q, bias, kb, vt, q_s, k_s, v_s, cache_k3, cache_v3, page_table, picks)
        xp = _b_out(o.reshape(n_seq * seq, D_MODEL), z, xp, wo, final_norm[None], ROW_TILE, final)
        xs = _b_out(o_s.reshape(n_dec, D_MODEL), z_s, xs, wo, final_norm[None], n_dec, final)

    return (xp.reshape(n_seq, seq, D_MODEL), xs.reshape(n_dec, 1, D_MODEL),
            jnp.stack(st_p_re), jnp.stack(st_p_im), jnp.stack(st_s_re), jnp.stack(st_s_im),
            k_p.reshape(n_seq, seq, N_HEADS, HEAD_DIM), v_p.reshape(n_seq, seq, N_HEADS, HEAD_DIM),
            k_s.reshape(n_dec, 1, N_HEADS, HEAD_DIM), v_s.reshape(n_dec, 1, N_HEADS, HEAD_DIM))
```

```python
import functools
import math

import jax
import jax.numpy as jnp
from jax import lax
from jax.experimental import pallas as pl
from jax.experimental.pallas import tpu as pltpu

D_MODEL = 1024
SSM_GROUP = 16
SSM_GROUPS = D_MODEL // SSM_GROUP
SSM_STATE = 64
SSM_TBLK = 16
SSM_BLKW = SSM_TBLK * SSM_GROUP
HEAD_DIM = 128
N_HEADS = D_MODEL // HEAD_DIM
MOBA_BLOCK = 256
MOBA_TOPK = 3
PAGE_SIZE = 128
ROPE_THETA = 10000.0
RMS_EPS = 1e-6
NEG_INF = -1e30
LANES = 128
VMEM_LIMIT = 48 * 1024 * 1024
ROW_TILE = 512
MOBA_HEADS = 4
V_ROWS = HEAD_DIM + 16

F32 = jnp.float32
BF16 = jnp.bfloat16


def _params(*sem):
    return pltpu.CompilerParams(dimension_semantics=sem, vmem_limit_bytes=VMEM_LIMIT)


def _rmsnorm(x, g):
    r = lax.rsqrt(jnp.mean(x * x, axis=-1, keepdims=True) + RMS_EPS)
    return x * r * g


def _dot(a, b):
    return jnp.dot(a, b, preferred_element_type=F32)


def _dot_nt(a, b, precision=None):
    return lax.dot_general(a, b, (((1,), (1,)), ((), ())), precision=precision,
                           preferred_element_type=F32)


def _weight_spec(w, layer):
    if layer is None:
        return pl.BlockSpec(w.shape, lambda *_: (0,) * w.ndim, pipeline_mode=pl.Buffered(1))
    return pl.BlockSpec((None,) + w.shape[1:], lambda *_: (layer,) + (0,) * (w.ndim - 1),
                        pipeline_mode=pl.Buffered(1))


def _cast_once(*pairs):
    @pl.when(pl.program_id(0) == 0)
    def _():
        for w_ref, w_scr in pairs:
            w_scr[...] = w_ref[...].astype(BF16)


def _rope(x, cos2, sin2):
    return x * cos2 + pltpu.roll(x, HEAD_DIM // 2, 1) * sin2


def _silu(z):
    return z * jax.nn.sigmoid(z)


def _top3(s, valid, axis):
    n_lanes = s.shape[axis]
    lane = lax.broadcasted_iota(jnp.int32, s.shape, axis)
    sm = jnp.where(valid, s, NEG_INF)
    sel = jnp.zeros(s.shape, F32)
    picks = []
    for _ in range(MOBA_TOPK):
        m = jnp.max(sm, axis=axis, keepdims=True)
        idx = jnp.min(jnp.where(sm == m, lane, n_lanes), axis=axis, keepdims=True)
        hit = lane == idx
        sel = jnp.where(hit, jnp.where(valid, 1.0, sel), sel)
        sm = jnp.where(hit, NEG_INF, sm)
        picks.append(idx)
    return sel, picks


GROUPS_PER_TILE = LANES // SSM_GROUP
STEP_TILES = SSM_BLKW // LANES


def _slot_transpose(xs, slot):
    xs = list(xs)
    step = GROUPS_PER_TILE // 2
    while step:
        low = (slot & step) == 0
        for i in range(GROUPS_PER_TILE):
            if i & step == 0:
                a, b = xs[i], xs[i + step]
                xs[i] = jnp.where(low, a, pltpu.roll(b, step * SSM_GROUP, 1))
                xs[i + step] = jnp.where(low, pltpu.roll(a, LANES - step * SSM_GROUP, 1), b)
        step //= 2
    return xs


def _to_blocked(u, tiles_ref, ug_ref):
    nb = ug_ref.shape[0]
    slot = lax.broadcasted_iota(jnp.int32, (nb, LANES), 1) // SSM_GROUP
    for tile in range(D_MODEL // LANES):
        tiles_ref[tile] = u[:, tile * LANES:(tile + 1) * LANES]
    for tile in range(D_MODEL // LANES):
        for hh in range(STEP_TILES):
            steps = [tiles_ref[tile, pl.ds(hh * GROUPS_PER_TILE + rp, nb, stride=SSM_TBLK), :]
                     for rp in range(GROUPS_PER_TILE)]
            for gp, piece in enumerate(_slot_transpose(steps, slot)):
                col = (tile * GROUPS_PER_TILE + gp) * SSM_BLKW + hh * LANES
                ug_ref[:, col:col + LANES] = piece.astype(ug_ref.dtype)


def _from_blocked(yg_ref, tiles_ref):
    nb = yg_ref.shape[0]
    slot = lax.broadcasted_iota(jnp.int32, (nb, LANES), 1) // SSM_GROUP
    for tile in range(D_MODEL // LANES):
        for hh in range(STEP_TILES):
            cols = [(tile * GROUPS_PER_TILE + gp) * SSM_BLKW + hh * LANES for gp in range(GROUPS_PER_TILE)]
            for rp, piece in enumerate(_slot_transpose([yg_ref[:, c:c + LANES] for c in cols], slot)):
                tiles_ref[tile, pl.ds(hh * GROUPS_PER_TILE + rp, nb, stride=SSM_TBLK), :] = piece
    return jnp.concatenate([tiles_ref[tile] for tile in range(D_MODEL // LANES)], axis=1)


def _a_in_kernel(blocked, x_ref, g_ref, w_ref, u_ref, z_ref, *rest):
    w_scr = rest[-1]
    _cast_once((w_ref, w_scr))
    h = _rmsnorm(x_ref[...], g_ref[...]).astype(BF16)
    uz = _dot(h, w_scr[...])
    u_ref[...] = uz[:, :D_MODEL]
    z_ref[...] = uz[:, D_MODEL:]
    if blocked:
        ug_ref, tiles_ref = rest[:2]
        _to_blocked(uz[:, :D_MODEL], tiles_ref, ug_ref)


def _a_in(x, g, w, layer, tm, blocked):
    t = x.shape[0]
    tile = pl.BlockSpec((tm, D_MODEL), lambda i: (i, 0))
    out_specs, out_shape = [tile, tile], [jax.ShapeDtypeStruct((t, D_MODEL), F32)] * 2
    if blocked:
        out_specs.append(pl.BlockSpec((tm // SSM_TBLK, SSM_GROUPS * SSM_BLKW), lambda i: (i, 0)))
        out_shape.append(jax.ShapeDtypeStruct((t // SSM_TBLK, SSM_GROUPS * SSM_BLKW), BF16))
    return pl.pallas_call(
        functools.partial(_a_in_kernel, blocked),
        grid=(t // tm,),
        in_specs=[tile, pl.BlockSpec((1, D_MODEL), lambda i: (0, 0)), _weight_spec(w, layer)],
        out_specs=out_specs,
        out_shape=out_shape,
        scratch_shapes=([pltpu.VMEM((D_MODEL // LANES, tm, LANES), F32)] if blocked else [])
        + [pltpu.VMEM((D_MODEL, 2 * D_MODEL), BF16)],
        compiler_params=_params("arbitrary"),
        name="a_in",
    )(x, g, w)


def _ssm_scan_kernel(n_seq, ug_ref, m_ref, f_ref, e_ref, ap_ref, y_ref, st_ref):
    rows = ug_ref.shape[0]
    nblk = rows // n_seq
    ug = ug_ref[...]
    s = _dot(ug, f_ref[...])
    row = lax.broadcasted_iota(jnp.int32, (nblk, LANES), 0)
    hp = []
    for q in range(n_seq):
        re = s[q * nblk:(q + 1) * nblk, :LANES]
        im = s[q * nblk:(q + 1) * nblk, LANES:]
        step, k = 1, 0
        while step < nblk:
            ar = ap_ref[2 * k:2 * k + 1, :]
            ai = ap_ref[2 * k + 1:2 * k + 2, :]
            sr = jnp.where(row >= step, pltpu.roll(re, step, 0), 0.0)
            si = jnp.where(row >= step, pltpu.roll(im, step, 0), 0.0)
            re, im = re + ar * sr - ai * si, im + ar * si + ai * sr
            step, k = step * 2, k + 1
        st_ref[2 * q:2 * q + 1, :] = re[nblk - 1:nblk, :]
        st_ref[2 * q + 1:2 * q + 2, :] = im[nblk - 1:nblk, :]
        pr = jnp.where(row >= 1, pltpu.roll(re, 1, 0), 0.0)
        pi = jnp.where(row >= 1, pltpu.roll(im, 1, 0), 0.0)
        hp.append(jnp.concatenate([pr, pi], axis=1))
    hprev = jnp.concatenate(hp, axis=0).astype(BF16)
    carry = _dot(hprev, e_ref[...])
    y_ref[:, :SSM_BLKW] = _dot(ug[:, :SSM_BLKW], m_ref[0]) + carry[:, :SSM_BLKW]
    y_ref[:, SSM_BLKW:] = _dot(ug[:, SSM_BLKW:], m_ref[1]) + carry[:, SSM_BLKW:]


def _ssm_scan(ug, prep, n_seq):
    rows = ug.shape[0]
    n_pairs = SSM_GROUPS // 2
    n_pow = prep["apow"].shape[1]
    return pl.pallas_call(
        functools.partial(_ssm_scan_kernel, n_seq),
        grid=(n_pairs,),
        in_specs=[pl.BlockSpec((rows, 2 * SSM_BLKW), lambda p: (0, p)),
                  pl.BlockSpec((2, SSM_BLKW, SSM_BLKW), lambda p: (p, 0, 0)),
                  pl.BlockSpec((None, 2 * SSM_BLKW, 2 * LANES), lambda p: (p, 0, 0)),
                  pl.BlockSpec((None, 2 * LANES, 2 * SSM_BLKW), lambda p: (p, 0, 0)),
                  pl.BlockSpec((None, n_pow, LANES), lambda p: (p, 0, 0))],
        out_specs=[pl.BlockSpec((rows, 2 * SSM_BLKW), lambda p: (0, p)),
                   pl.BlockSpec((None, 2 * n_seq, LANES), lambda p: (p, 0, 0))],
        out_shape=[jax.ShapeDtypeStruct((rows, SSM_GROUPS * SSM_BLKW), F32),
                   jax.ShapeDtypeStruct((n_pairs, 2 * n_seq, LANES), F32)],
        compiler_params=_params("parallel"),
        name="ssm_scan",
    )(ug, prep["m"], prep["f"], prep["e"], prep["apow"])


def _ssm_step_kernel(u_ref, hr_ref, hi_ref, bb_ref, ar_ref, ai_ref, cs_ref, y_ref, or_ref, oi_ref):
    bu = lax.dot_general(u_ref[...], bb_ref[...], (((2,), (1,)), ((0,), (0,))),
                         preferred_element_type=F32)
    h0r, h0i = hr_ref[...], hi_ref[...]
    ar, ai = ar_ref[...], ai_ref[...]
    hr = bu[:, :, :SSM_STATE] + ar * h0r - ai * h0i
    hi = bu[:, :, SSM_STATE:] + ar * h0i + ai * h0r
    or_ref[...] = hr
    oi_ref[...] = hi
    y_ref[...] = lax.dot_general(jnp.concatenate([hr, hi], axis=2), cs_ref[...],
                                 (((2,), (1,)), ((0,), (0,))), preferred_element_type=F32)


def _ssm_step(u_g, h0r, h0i, prep):
    g, n, _ = u_g.shape
    return pl.pallas_call(
        _ssm_step_kernel,
        out_shape=[jax.ShapeDtypeStruct((g, n, SSM_GROUP), F32),
                   jax.ShapeDtypeStruct((g, n, SSM_STATE), F32),
                   jax.ShapeDtypeStruct((g, n, SSM_STATE), F32)],
        compiler_params=pltpu.CompilerParams(vmem_limit_bytes=VMEM_LIMIT),
        name="ssm_step",
    )(u_g, h0r, h0i, prep["bb"], prep["a_re"], prep["a_im"], prep["cs"])


def _a_out_kernel(blocked, ys_ref, u_ref, z_ref, x_ref, d_ref, wg_ref, wo_ref, o_ref, *scratch):
    wg_scr, wo_scr = scratch[-2:]
    _cast_once((wg_ref, wg_scr), (wo_ref, wo_scr))
    ys = _from_blocked(ys_ref, scratch[0]) if blocked else ys_ref[...]
    y = ys + d_ref[...] * u_ref[...]
    g = jax.nn.gelu(y)
    y2 = g * jax.nn.sigmoid(_dot(g.astype(BF16), wg_scr[...]))
    v = (y2 * _silu(z_ref[...])).astype(BF16)
    o_ref[...] = x_ref[...] + _dot(v, wo_scr[...])


def _a_out(ys, u, z, x, d, wg, wo, layer, tm, blocked):
    t = x.shape[0]
    tile = pl.BlockSpec((tm, D_MODEL), lambda i: (i, 0))
    ys_spec = pl.BlockSpec((tm // SSM_TBLK, SSM_GROUPS * SSM_BLKW), lambda i: (i, 0)) if blocked else tile
    return pl.pallas_call(
        functools.partial(_a_out_kernel, blocked),
        grid=(t // tm,),
        in_specs=[ys_spec, tile, tile, tile, pl.BlockSpec((1, D_MODEL), lambda i: (0, 0)),
                  _weight_spec(wg, layer), _weight_spec(wo, layer)],
        out_specs=tile,
        out_shape=jax.ShapeDtypeStruct((t, D_MODEL), F32),
        scratch_shapes=([pltpu.VMEM((D_MODEL // LANES, tm, LANES), F32)] if blocked else [])
        + [pltpu.VMEM((D_MODEL, D_MODEL), BF16)] * 2,
        compiler_params=_params("arbitrary"),
        name="a_out",
    )(ys, u, z, x, d, wg, wo)


def _ssm_prep_kernel(n_steps, lrc_ref, lic_ref, lrr_ref, lir_ref, ldt_ref, bt_re_ref, bt_im_ref, ct_re_ref, ct_im_ref,
                     m_ref, f_ref, e_ref, ap_ref, ab_ref, bb_ref, cs_ref, m32, f32s, e32):
    hi = lax.Precision.HIGHEST
    p = SSM_STATE
    m32[...] = jnp.zeros(m32.shape, F32)
    f32s[...] = jnp.zeros(f32s.shape, F32)
    e32[...] = jnp.zeros(e32.shape, F32)
    for gi in range(2):
        dt = jnp.exp(ldt_ref[gi])

        def discretise(lr, li):
            mag = jnp.exp(lr * dt)
            a_re, a_im = mag * jnp.cos(li * dt), mag * jnp.sin(li * dt)
            den = lr * lr + li * li
            f_re = ((a_re - 1.0) * lr + a_im * li) / den
            f_im = (a_im * lr - (a_re - 1.0) * li) / den
            return a_re, a_im, f_re, f_im

        ac_re, ac_im, _, _ = discretise(lrc_ref[gi], lic_ref[gi])
        lrr, lir = lrr_ref[gi], lir_ref[gi]
        ar_re, ar_im, fr_re, fr_im = discretise(lrr, lir)
        ab_ref[gi, 0:1, :] = ar_re
        ab_ref[gi, 1:2, :] = ar_im
        ct_re, ct_im = ct_re_ref[gi], ct_im_ref[gi]
        cs_ref[gi] = jnp.concatenate([ct_re, -ct_im], axis=0)
        bbt_re = fr_re * bt_re_ref[gi] - fr_im * bt_im_ref[gi]
        bbt_im = fr_re * bt_im_ref[gi] + fr_im * bt_re_ref[gi]
        bb_ref[gi, :, :p] = bbt_re
        bb_ref[gi, :, p:] = bbt_im
        for k in range(n_steps):
            n = float(SSM_TBLK * 2 ** k)
            mag = jnp.exp(lrr * dt * n)
            ap_ref[2 * k:2 * k + 1, gi * p:(gi + 1) * p] = mag * jnp.cos(lir * dt * n)
            ap_ref[2 * k + 1:2 * k + 2, gi * p:(gi + 1) * p] = mag * jnp.sin(lir * dt * n)
        pc_re, pc_im = jnp.ones_like(ac_re), jnp.zeros_like(ac_im)
        pr_re, pr_im = jnp.ones_like(ar_re), jnp.zeros_like(ar_im)
        for t in range(SSM_TBLK + 1):
            if t:
                pc_re, pc_im = pc_re * ac_re - pc_im * ac_im, pc_re * ac_im + pc_im * ac_re
                pr_re, pr_im = pr_re * ar_re - pr_im * ar_im, pr_re * ar_im + pr_im * ar_re
            ca_re = pc_re * ct_re - pc_im * ct_im
            nca_im = -(pc_re * ct_im + pc_im * ct_re)
            if t:
                cols = slice(gi * SSM_BLKW + (t - 1) * SSM_GROUP, gi * SSM_BLKW + t * SSM_GROUP)
                e32[gi * p:(gi + 1) * p, cols] = ca_re
                e32[(2 + gi) * p:(3 + gi) * p, cols] = nca_im
            if t < SSM_TBLK:
                r = SSM_TBLK - 1 - t
                rows = slice(gi * SSM_BLKW + r * SSM_GROUP, gi * SSM_BLKW + (r + 1) * SSM_GROUP)
                f32s[rows, gi * p:(gi + 1) * p] = pr_re * bbt_re - pr_im * bbt_im
                f32s[rows, (2 + gi) * p:(3 + gi) * p] = pr_re * bbt_im + pr_im * bbt_re
                kt = (jnp.dot(bbt_re, ca_re, precision=hi, preferred_element_type=F32)
                      + jnp.dot(bbt_im, nca_im, precision=hi, preferred_element_type=F32))
                for r_in in range(SSM_TBLK - t):
                    r_out = r_in + t
                    m32[gi, r_in * SSM_GROUP:(r_in + 1) * SSM_GROUP, r_out * SSM_GROUP:(r_out + 1) * SSM_GROUP] = kt
    m_ref[...] = m32[...].astype(BF16)
    f_ref[...] = f32s[...].astype(BF16)
    e_ref[...] = e32[...].astype(BF16)


def _ssm_prep(lam_re, lam_im, log_dt, b_re, b_im, c_re, c_im, nblk):
    g, p = lam_re.shape
    n_pairs = g // 2
    n_steps = max(1, math.ceil(math.log2(nblk)))
    col = lambda x: x.reshape(g, p, 1)
    row = lambda x: x.reshape(g, 1, p)
    tr = lambda x: x.transpose(0, 2, 1)

    def two(*shape):
        return pl.BlockSpec((2,) + shape, lambda i: (i,) + (0,) * len(shape))

    def one(*shape):
        return pl.BlockSpec((None,) + shape, lambda i: (i,) + (0,) * len(shape))

    m, fp, ep, ap, ab, bb, cs = pl.pallas_call(
        functools.partial(_ssm_prep_kernel, n_steps),
        grid=(n_pairs,),
        in_specs=[two(p, 1), two(p, 1), two(1, p), two(1, p), two(1, 1),
                  two(SSM_GROUP, p), two(SSM_GROUP, p), two(p, SSM_GROUP), two(p, SSM_GROUP)],
        out_specs=[two(SSM_BLKW, SSM_BLKW), one(2 * SSM_BLKW, 4 * p), one(4 * p, 2 * SSM_BLKW),
                   one(2 * n_steps, 2 * p), two(2, p), two(SSM_GROUP, 2 * p), two(2 * p, SSM_GROUP)],
        out_shape=[jax.ShapeDtypeStruct((g, SSM_BLKW, SSM_BLKW), BF16),
                   jax.ShapeDtypeStruct((n_pairs, 2 * SSM_BLKW, 4 * p), BF16),
                   jax.ShapeDtypeStruct((n_pairs, 4 * p, 2 * SSM_BLKW), BF16),
                   jax.ShapeDtypeStruct((n_pairs, 2 * n_steps, 2 * p), F32),
                   jax.ShapeDtypeStruct((g, 2, p), F32),
                   jax.ShapeDtypeStruct((g, SSM_GROUP, 2 * p), F32),
                   jax.ShapeDtypeStruct((g, 2 * p, SSM_GROUP), F32)],
        scratch_shapes=[pltpu.VMEM((2, SSM_BLKW, SSM_BLKW), F32), pltpu.VMEM((2 * SSM_BLKW, 4 * p), F32),
                        pltpu.VMEM((4 * p, 2 * SSM_BLKW), F32)],
        compiler_params=_params("parallel"),
        name="ssm_prep",
    )(col(lam_re), col(lam_im), row(lam_re), row(lam_im), log_dt.reshape(g, 1, 1),
      tr(b_re), tr(b_im), tr(c_re), tr(c_im))
    return {"m": m, "f": fp, "e": ep, "apow": ap, "bb": bb,
            "a_re": ab[:, 0:1, :], "a_im": ab[:, 1:2, :], "cs": cs}


def _s5_layer_prompt(x, n_seq, layer, norm, w_in, prep, d, wg, wo):
    u, z, ug = _a_in(x, norm, w_in, layer, ROW_TILE, True)
    yg, st = _ssm_scan(ug, prep, n_seq)
    st = st.reshape(SSM_GROUPS // 2, n_seq, 2, 2, SSM_STATE)
    st = st.transpose(2, 1, 0, 3, 4).reshape(2, n_seq, SSM_GROUPS, SSM_STATE)
    return _a_out(yg, u, z, x, d, wg, wo, layer, ROW_TILE, True), st[0], st[1]


def _s5_layer_sample(x, h0r, h0i, layer, norm, w_in, prep, d, wg, wo):
    n = x.shape[0]
    u, z = _a_in(x, norm, w_in, layer, n, False)
    u_g = u.reshape(n, SSM_GROUPS, SSM_GROUP).transpose(1, 0, 2)
    y_g, hr, hi = _ssm_step(u_g, h0r.transpose(1, 0, 2), h0i.transpose(1, 0, 2), prep)
    ys = y_g.transpose(1, 0, 2).reshape(n, D_MODEL)
    return _a_out(ys, u, z, x, d, wg, wo, layer, n, False), hr.transpose(1, 0, 2), hi.transpose(1, 0, 2)


def _kv_kernel(prompt, x_ref, g_ref, w_ref, cos_ref, sin_ref, k_ref, v_ref, *rest):
    w_scr = rest[-1]
    _cast_once((w_ref, w_scr))
    h = _rmsnorm(x_ref[...], g_ref[...]).astype(BF16)
    kv = _dot(h, w_scr[...])
    cos2, sin2 = cos_ref[...], sin_ref[...]
    tm = kv.shape[0]
    for hd in range(N_HEADS):
        sl = slice(hd * HEAD_DIM, (hd + 1) * HEAD_DIM)
        k = _rope(kv[:, sl], cos2, sin2)
        v = kv[:, D_MODEL + hd * HEAD_DIM:D_MODEL + (hd + 1) * HEAD_DIM]
        k_ref[:, sl] = k
        v_ref[:, sl] = v
        if prompt:
            kb_ref, vt_ref, mean_ref = rest[:3]
            kb_ref[hd] = k.astype(BF16)
            for b in range(tm // MOBA_BLOCK):
                rows = slice(b * MOBA_BLOCK, (b + 1) * MOBA_BLOCK)
                mean_ref[b, :, sl] = jnp.sum(k[rows], axis=0, keepdims=True) * (1.0 / MOBA_BLOCK)
                vt_ref[hd, b, :HEAD_DIM, :] = v[rows].T.astype(BF16)
                vt_ref[hd, b, HEAD_DIM:, :] = jnp.ones((V_ROWS - HEAD_DIM, MOBA_BLOCK), BF16)


def _kv_proj(x, g, w, cos2, sin2, n_seq, tm, prompt):
    t = x.shape[0]
    per_seq = t // n_seq // tm
    tile = pl.BlockSpec((tm, D_MODEL), lambda i: (i, 0))
    rope_tile = pl.BlockSpec((tm, HEAD_DIM), lambda i: (i % per_seq, 0))
    out_specs = [tile, tile]
    out_shape = [jax.ShapeDtypeStruct((t, D_MODEL), F32)] * 2
    if prompt:
        nb = tm // MOBA_BLOCK
        out_specs += [
            pl.BlockSpec((None, N_HEADS, tm, HEAD_DIM), lambda i: (i // per_seq, 0, i % per_seq, 0)),
            pl.BlockSpec((None, N_HEADS, nb, V_ROWS, MOBA_BLOCK), lambda i: (i // per_seq, 0, i % per_seq, 0, 0)),
            pl.BlockSpec((nb, 1, D_MODEL), lambda i: (i, 0, 0))]
        out_shape += [
            jax.ShapeDtypeStruct((n_seq, N_HEADS, t // n_seq, HEAD_DIM), BF16),
            jax.ShapeDtypeStruct((n_seq, N_HEADS, t // n_seq // MOBA_BLOCK, V_ROWS, MOBA_BLOCK), BF16),
            jax.ShapeDtypeStruct((t // MOBA_BLOCK, 1, D_MODEL), F32)]
    return pl.pallas_call(
        functools.partial(_kv_kernel, prompt),
        grid=(t // tm,),
        in_specs=[tile, pl.BlockSpec((1, D_MODEL), lambda i: (0, 0)), _weight_spec(w, None),
                  rope_tile, rope_tile],
        out_specs=out_specs,
        out_shape=out_shape,
        scratch_shapes=[pltpu.VMEM((D_MODEL, 2 * D_MODEL), BF16)],
        compiler_params=_params("arbitrary"),
        name="kv_proj",
    )(x, g, w, cos2, sin2)


def _q_kernel(blocks_per_seq, x_ref, g_ref, w_ref, cos_ref, sin_ref, kbar_ref, q_ref, bias_ref, z_ref, w_scr):
    _cast_once((w_ref, w_scr))
    tm = x_ref.shape[0]
    per_tile = tm // MOBA_BLOCK
    first = (pl.program_id(0) * per_tile) % blocks_per_seq
    h = _rmsnorm(x_ref[...], g_ref[...]).astype(BF16)
    qz = _dot(h, w_scr[...])
    z_ref[...] = qz[:, D_MODEL:]
    cos2, sin2 = cos_ref[...], sin_ref[...]
    blk = lax.broadcasted_iota(jnp.int32, (blocks_per_seq, MOBA_BLOCK), 0)
    for hd in range(N_HEADS):
        q = _rope(qz[:, hd * HEAD_DIM:(hd + 1) * HEAD_DIM], cos2, sin2)
        q_ref[hd] = (q * (HEAD_DIM ** -0.5)).astype(BF16)
        for b in range(per_tile):
            rows = slice(b * MOBA_BLOCK, (b + 1) * MOBA_BLOCK)
            s = _dot_nt(kbar_ref[hd], q[rows], precision=lax.Precision.HIGHEST)
            sel, _ = _top3(s, blk < first + b, 0)
            bias_ref[hd, b] = jnp.where(sel > 0.5, 0.0, NEG_INF)


def _q_proj(x, g, w, layer, cos2, sin2, kbar, n_seq):
    t = x.shape[0]
    tm = ROW_TILE
    bps = t // n_seq // MOBA_BLOCK
    per_tile = tm // MOBA_BLOCK
    tps = bps // per_tile
    tile = pl.BlockSpec((tm, D_MODEL), lambda i: (i, 0))
    rope_tile = pl.BlockSpec((tm, HEAD_DIM), lambda i: (i % tps, 0))
    return pl.pallas_call(
        functools.partial(_q_kernel, bps),
        grid=(t // tm,),
        in_specs=[tile, pl.BlockSpec((1, D_MODEL), lambda i: (0, 0)), _weight_spec(w, layer),
                  rope_tile, rope_tile,
                  pl.BlockSpec((None, N_HEADS, bps, HEAD_DIM), lambda i: (i // tps, 0, 0, 0))],
        out_specs=[pl.BlockSpec((None, N_HEADS, tm, HEAD_DIM), lambda i: (i // tps, 0, i % tps, 0)),
                   pl.BlockSpec((None, N_HEADS, per_tile, bps, MOBA_BLOCK), lambda i: (i // tps, 0, i % tps, 0, 0)),
                   tile],
        out_shape=[jax.ShapeDtypeStruct((n_seq, N_HEADS, t // n_seq, HEAD_DIM), BF16),
                   jax.ShapeDtypeStruct((n_seq, N_HEADS, bps, bps, MOBA_BLOCK), F32),
                   jax.ShapeDtypeStruct((t, D_MODEL), F32)],
        scratch_shapes=[pltpu.VMEM((D_MODEL, 2 * D_MODEL), BF16)],
        compiler_params=_params("arbitrary"),
        name="q_proj",
    )(x, g, w, cos2, sin2, kbar)


def _moba_layer_kernel(items, pk_ref, pt_ref, q_ref, bias_ref, k_ref, vt_ref, qs_ref, kn_ref, vn_ref, *rest):
    n_pages = items * N_FETCH
    k_pages, v_pages = rest[:n_pages], rest[n_pages:2 * n_pages]
    o_ref, os_ref, s_a, s_b, acc_scr = rest[2 * n_pages:]
    step = (pl.program_id(0) * pl.num_programs(1) + pl.program_id(1)) * pl.num_programs(2) + pl.program_id(2)
    head0 = (step * items) % N_HEADS
    for a in range(items):
        pages = slice(a * N_FETCH, (a + 1) * N_FETCH)
        os_ref[a] = _decode_attend(head0 + a, qs_ref[a], kn_ref[a], vn_ref[a], k_pages[pages], v_pages[pages])
    _moba_tile(q_ref, bias_ref, k_ref, vt_ref, o_ref, s_a, s_b, acc_scr)


def _moba_tile(q_ref, bias_ref, k_ref, vt_ref, o_ref, s_a, s_b, acc_scr):
    i = pl.program_id(2)
    nblk = bias_ref.shape[1]
    row = lax.broadcasted_iota(jnp.int32, (MOBA_BLOCK, LANES), 0)
    col = lax.broadcasted_iota(jnp.int32, (MOBA_BLOCK, LANES), 1)
    chains = [(h, c) for h in range(MOBA_HEADS) for c in range(MOBA_BLOCK // LANES)]
    nc = len(chains)
    trips = (i + 1) // 2
    last_trip = nblk // 2 - 1

    def scores(h, c, blk):
        off = pl.multiple_of(blk * MOBA_BLOCK, MOBA_BLOCK)
        return _dot_nt(k_ref[h, pl.ds(off, MOBA_BLOCK), :], q_ref[h, c * LANES:(c + 1) * LANES, :])

    def stage_scores(t, buf):
        mbs = []
        for n, (h, c) in enumerate(chains):
            for u in range(2):
                s = scores(h, c, 2 * t + u)
                buf[u * nc + n] = s
                mbs.append(jnp.max(s, axis=0, keepdims=True))
        return tuple(mbs)

    def consume(t, buf, ms, mbs):
        out = []
        for n, (h, c) in enumerate(chains):
            b = [bias_ref[h, pl.ds(2 * t + u, 1), :][:, c * LANES:(c + 1) * LANES] for u in range(2)]
            m_new = jnp.maximum(ms[n], jnp.maximum(mbs[2 * n] + b[0], mbs[2 * n + 1] + b[1]))
            alpha = jnp.exp(ms[n] - m_new)
            p = [jnp.exp(buf[u * nc + n] - (m_new - b[u])).astype(BF16) for u in range(2)]
            acc_scr[n] = alpha * acc_scr[n] + (_dot(vt_ref[h, 2 * t], p[0]) + _dot(vt_ref[h, 2 * t + 1], p[1]))
            out.append(m_new)
        return tuple(out)

    for n, (h, c) in enumerate(chains):
        s_b[n] = scores(h, c, i)
    mbs_a = stage_scores(0, s_a)
    ms = []
    for n, (h, c) in enumerate(chains):
        s = jnp.where(row <= col + c * LANES, s_b[n], NEG_INF)
        m = jnp.max(s, axis=0, keepdims=True)
        acc_scr[n] = _dot(vt_ref[h, i], jnp.exp(s - m).astype(BF16))
        ms.append(m)

    def body(tt, carry):
        ms, mbs_a = carry
        t0 = 2 * tt
        mbs_b = stage_scores(t0 + 1, s_b)
        ms = consume(t0, s_a, ms, mbs_a)
        mbs_a = stage_scores(jnp.minimum(t0 + 2, last_trip), s_a)
        ms = consume(t0 + 1, s_b, ms, mbs_b)
        return ms, mbs_a

    lax.fori_loop(0, (trips + 1) // 2, body, (tuple(ms), mbs_a))
    for n, (h, c) in enumerate(chains):
        acc = acc_scr[n]
        o = acc[:HEAD_DIM, :] / acc[HEAD_DIM:HEAD_DIM + 1, :]
        o_ref[c * LANES:(c + 1) * LANES, h * HEAD_DIM:(h + 1) * HEAD_DIM] = o.T.astype(o_ref.dtype)


def _moba_layer(q, bias, kb, vt, q_s, k_new, v_new, cache_k3, cache_v3, page_table, picks):
    n_seq, _, seq, _ = q.shape
    n_dec, n_pages = page_table.shape
    nblk = seq // MOBA_BLOCK
    hg = MOBA_HEADS
    n_hg = N_HEADS // hg
    n_chains = hg * (MOBA_BLOCK // LANES)
    steps = n_seq * n_hg * nblk
    items = n_dec * N_HEADS // steps
    assert items * steps == n_dec * N_HEADS and N_HEADS % items == 0
    per_block = MOBA_BLOCK // PAGE_SIZE

    def first_item(n, h, i):
        return ((n * n_hg + h) * nblk + i) * items

    def dec_index(n, h, i, pk, pt):
        e = first_item(n, h, i)
        return (e // N_HEADS, (e % N_HEADS) // items, 0, 0)

    def page_spec(a, f):
        def index(n, h, i, pk, pt):
            e = first_item(n, h, i) + a
            blk = pk[e * MOBA_TOPK + f // per_block]
            return (pt[(e // N_HEADS) * n_pages + blk * per_block + f % per_block], 0, 0)
        return pl.BlockSpec((None, PAGE_SIZE * N_HEADS, HEAD_DIM), index)

    dec_rows = pl.BlockSpec((None, items, 1, HEAD_DIM), dec_index)
    page_specs = [page_spec(a, f) for a in range(items) for f in range(N_FETCH)]
    resident = pl.Buffered(1)
    dec_shape = (n_dec, N_HEADS, 1, HEAD_DIM)
    return pl.pallas_call(
        functools.partial(_moba_layer_kernel, items),
        grid_spec=pltpu.PrefetchScalarGridSpec(
            num_scalar_prefetch=2,
            grid=(n_seq, n_hg, nblk),
            in_specs=[pl.BlockSpec((None, hg, MOBA_BLOCK, HEAD_DIM), lambda n, h, i, pk, pt: (n, h, i, 0)),
                      pl.BlockSpec((None, hg, None, nblk, MOBA_BLOCK), lambda n, h, i, pk, pt: (n, h, i, 0, 0)),
                      pl.BlockSpec((None, hg, seq, HEAD_DIM), lambda n, h, i, pk, pt: (n, h, 0, 0),
                                   pipeline_mode=resident),
                      pl.BlockSpec((None, hg, nblk, V_ROWS, MOBA_BLOCK), lambda n, h, i, pk, pt: (n, h, 0, 0, 0),
                                   pipeline_mode=resident),
                      dec_rows, dec_rows, dec_rows] + page_specs + page_specs,
            out_specs=[pl.BlockSpec((None, MOBA_BLOCK, hg * HEAD_DIM), lambda n, h, i, pk, pt: (n, i, h)),
                       dec_rows],
            scratch_shapes=[pltpu.VMEM((2 * n_chains, MOBA_BLOCK, LANES), F32),
                            pltpu.VMEM((2 * n_chains, MOBA_BLOCK, LANES), F32),
                            pltpu.VMEM((n_chains, V_ROWS, LANES), F32)]),
        out_shape=[jax.ShapeDtypeStruct((n_seq, seq, D_MODEL), BF16),
                   jax.ShapeDtypeStruct(dec_shape, F32)],
        compiler_params=_params("arbitrary", "arbitrary", "arbitrary"),
        name="moba_layer",
    )(picks, page_table.reshape(-1), q, bias, kb, vt,
      q_s.reshape(dec_shape), k_new.reshape(dec_shape), v_new.reshape(dec_shape),
      *([cache_k3] * len(page_specs)), *([cache_v3] * len(page_specs)))


def _b_out_kernel(final, o_ref, z_ref, x_ref, w_ref, g_ref, y_ref, w_scr):
    _cast_once((w_ref, w_scr))
    v = (o_ref[...].astype(F32) * _silu(z_ref[...])).astype(BF16)
    x = x_ref[...] + _dot(v, w_scr[...])
    y_ref[...] = _rmsnorm(x, g_ref[...]) if final else x


def _b_out(o, z, x, w, layer, g, tm, final):
    t = x.shape[0]
    tile = pl.BlockSpec((tm, D_MODEL), lambda i: (i, 0))
    return pl.pallas_call(
        functools.partial(_b_out_kernel, final),
        grid=(t // tm,),
        in_specs=[tile, tile, tile, _weight_spec(w, layer), pl.BlockSpec((1, D_MODEL), lambda i: (0, 0))],
        out_specs=tile,
        out_shape=jax.ShapeDtypeStruct((t, D_MODEL), F32),
        scratch_shapes=[pltpu.VMEM((D_MODEL, D_MODEL), BF16)],
        compiler_params=_params("arbitrary"),
        name="b_out",
    )(o, z, x, w, g)


PAGES_PER_STEP = 16


def _page_mean_kernel(*refs):
    pages, out_ref = refs[1:1 + PAGES_PER_STEP], refs[-1]
    per_block = MOBA_BLOCK // PAGE_SIZE
    for b in range(PAGES_PER_STEP // per_block):
        tot = None
        for r in range(per_block):
            page = pages[per_block * b + r][...].reshape(PAGE_SIZE, N_HEADS, HEAD_DIM)
            part = jnp.sum(page, axis=0)
            tot = part if tot is None else tot + part
        tot = tot * (1.0 / MOBA_BLOCK)
        for hd in range(N_HEADS):
            out_ref[hd, b:b + 1, :] = tot[hd:hd + 1, :]


def _page_means(cache_k2, page_table):
    n, n_pages = page_table.shape
    steps = n_pages // PAGES_PER_STEP
    nb = PAGES_PER_STEP * PAGE_SIZE // MOBA_BLOCK

    def page_spec(r):
        return pl.BlockSpec((None, PAGE_SIZE * N_HEADS, HEAD_DIM),
                            lambda s, j, pt, r=r: (pt[s * n_pages + j * PAGES_PER_STEP + r], 0, 0))

    return pl.pallas_call(
        _page_mean_kernel,
        grid_spec=pltpu.PrefetchScalarGridSpec(
            num_scalar_prefetch=1,
            grid=(n, steps),
            in_specs=[page_spec(r) for r in range(PAGES_PER_STEP)],
            out_specs=pl.BlockSpec((None, N_HEADS, nb, HEAD_DIM), lambda s, j, pt: (s, 0, j, 0))),
        out_shape=jax.ShapeDtypeStruct((n, N_HEADS, n_pages * PAGE_SIZE // MOBA_BLOCK, HEAD_DIM), F32),
        compiler_params=_params("parallel", "arbitrary"),
        name="page_means",
    )(page_table.reshape(-1), *([cache_k2] * PAGES_PER_STEP))


def _q_sample_kernel(x_ref, g_ref, w_ref, cos_ref, sin_ref, q_ref, z_ref):
    h = _rmsnorm(x_ref[...], g_ref[...]).astype(BF16)
    qz = _dot(h, w_ref[...].astype(BF16))
    z_ref[...] = qz[:, D_MODEL:]
    cos2, sin2 = cos_ref[...], sin_ref[...]
    for hd in range(N_HEADS):
        sl = slice(hd * HEAD_DIM, (hd + 1) * HEAD_DIM)
        q_ref[:, sl] = _rope(qz[:, sl], cos2, sin2)


def _q_sample(x, g, w, layer, cos2, sin2):
    n = x.shape[0]
    whole = lambda a: pl.BlockSpec(a.shape, lambda i: (0,) * a.ndim)
    return pl.pallas_call(
        _q_sample_kernel,
        grid=(1,),
        in_specs=[whole(x), whole(g), _weight_spec(w, layer), whole(cos2), whole(sin2)],
        out_specs=[pl.BlockSpec((n, D_MODEL), lambda i: (0, 0))] * 2,
        out_shape=[jax.ShapeDtypeStruct((n, D_MODEL), F32)] * 2,
        compiler_params=_params("arbitrary"),
        name="q_sample",
    )(x, g, w, cos2, sin2)


PICK_UNROLL = 4


def _pick_kernel(q_ref, kbar_ref, knew_ref, idx_ref):
    n_seq, _, n_past, _ = kbar_ref.shape
    lane_o = lax.broadcasted_iota(jnp.int32, (8, LANES), 1)

    def one_sequence(sq):
        q_row, knew_row = q_ref[pl.ds(sq, 1), :], knew_ref[pl.ds(sq, 1), :]
        scores = None
        for hd in range(N_HEADS):
            sl = slice(hd * HEAD_DIM, (hd + 1) * HEAD_DIM)
            q8 = jnp.broadcast_to(q_row[:, sl], (8, HEAD_DIM))
            own_mean = jnp.broadcast_to(knew_row[:, sl] * (1.0 / MOBA_BLOCK), (8, HEAD_DIM))
            means = jnp.concatenate([kbar_ref[sq, hd], own_mean], axis=0)
            s = _dot_nt(q8, means, precision=lax.Precision.HIGHEST)
            head_row = lax.broadcasted_iota(jnp.int32, s.shape, 0) == hd
            scores = s if scores is None else jnp.where(head_row, s, scores)
        lane = lax.broadcasted_iota(jnp.int32, scores.shape, 1)
        _, picks = _top3(scores, lane < n_past, 1)
        out = jnp.zeros((8, LANES), jnp.int32)
        for r, idx in enumerate(picks):
            out = jnp.where(lane_o == r, idx, out)
        idx_ref[sq] = out

    def some_sequences(i, carry):
        for j in range(PICK_UNROLL):
            one_sequence(i * PICK_UNROLL + j)
        return carry

    lax.fori_loop(0, n_seq // PICK_UNROLL, some_sequences, 0)


def _pick_blocks(q, kbar, knew):
    n = kbar.shape[0]
    assert n % PICK_UNROLL == 0
    return pl.pallas_call(
        _pick_kernel,
        out_shape=jax.ShapeDtypeStruct((n, 8, LANES), jnp.int32),
        compiler_params=pltpu.CompilerParams(vmem_limit_bytes=VMEM_LIMIT),
        name="pick_blocks",
    )(q, kbar, knew)


N_FETCH = MOBA_TOPK * (MOBA_BLOCK // PAGE_SIZE)


def _decode_attend(head, q, k_new, v_new, k_pages, v_pages):
    scale = HEAD_DIM ** -0.5
    q8 = jnp.broadcast_to(q, (8, HEAD_DIM)).astype(BF16)
    head_rows = pl.ds(head, PAGE_SIZE, stride=N_HEADS)
    kc = jnp.concatenate([r[head_rows, :].astype(BF16) for r in k_pages], axis=0)
    vc = jnp.concatenate([r[head_rows, :].astype(BF16) for r in v_pages], axis=0)
    s = _dot_nt(q8, kc)[0:1, :] * scale
    s_new = jnp.sum(q * k_new, axis=1, keepdims=True) * scale
    m = jnp.maximum(jnp.max(s, axis=1, keepdims=True), s_new)
    p = jnp.exp(s - m)
    p_new = jnp.exp(s_new - m)
    l = jnp.sum(p, axis=1, keepdims=True) + p_new
    pv = _dot(jnp.broadcast_to(p, (8, p.shape[1])).astype(BF16), vc)[0:1, :]
    return (pv + p_new * v_new) / l


def _rope_tables(pos):
    half = HEAD_DIM // 2
    inv = ROPE_THETA ** (-jnp.arange(half, dtype=F32) / half)
    ang = pos.astype(F32)[:, None] * inv[None, :]
    cos, sin = jnp.cos(ang), jnp.sin(ang)
    return jnp.concatenate([cos, cos], axis=1), jnp.concatenate([-sin, sin], axis=1)


def kernel(x_prompt, x_sample, state_ssm_re, state_ssm_im, cache_k, cache_v, page_table, a_norm, a_w_in, a_lam_re, a_lam_im, a_log_dt, a_b_re, a_b_im, a_c_re, a_c_im, a_d, a_w_glu, a_w_out, kv_norm, w_kv, b_norm, b_w_in, b_w_out, final_norm):
    n_seq, seq, _ = x_prompt.shape
    n_dec = x_sample.shape[0]
    n_pool = cache_k.shape[0]
    past_len = page_table.shape[1] * PAGE_SIZE
    assert x_sample.shape[1] == 1 and seq % ROW_TILE == 0 and ROW_TILE % MOBA_BLOCK == 0
    assert past_len % (PAGES_PER_STEP * PAGE_SIZE) == 0 and past_len // MOBA_BLOCK >= MOBA_TOPK
    assert seq // MOBA_BLOCK <= LANES and (seq // MOBA_BLOCK) % 8 == 0
    n_a, n_b = a_norm.shape[0], b_norm.shape[0]

    xp = x_prompt.reshape(n_seq * seq, D_MODEL)
    xs = x_sample.reshape(n_dec, D_MODEL)
    cos_p, sin_p = _rope_tables(jnp.arange(seq, dtype=jnp.int32))
    cos_s, sin_s = _rope_tables(jnp.full((n_dec,), past_len, jnp.int32))

    st_p_re, st_p_im, st_s_re, st_s_im = [], [], [], []
    for l in range(n_a):
        prep = _ssm_prep(a_lam_re[l], a_lam_im[l], a_log_dt[l], a_b_re[l], a_b_im[l], a_c_re[l], a_c_im[l],
                         seq // SSM_TBLK)
        norm, d = a_norm[l][None], a_d[l][None]
        xp, hr, hi = _s5_layer_prompt(xp, n_seq, l, norm, a_w_in, prep, d, a_w_glu, a_w_out)
        st_p_re.append(hr)
        st_p_im.append(hi)
        xs, hr, hi = _s5_layer_sample(xs, state_ssm_re[l], state_ssm_im[l], l, norm, a_w_in, prep, d, a_w_glu, a_w_out)
        st_s_re.append(hr)
        st_s_im.append(hi)

    k_p, v_p, kb, vt, kbar_p = _kv_proj(xp, kv_norm[None], w_kv, cos_p, sin_p, n_seq, ROW_TILE, True)
    k_s, v_s = _kv_proj(xs, kv_norm[None], w_kv, cos_s, sin_s, 1, n_dec, False)
    nblk = seq // MOBA_BLOCK
    kbar_p = kbar_p.reshape(n_seq, nblk, N_HEADS, HEAD_DIM).transpose(0, 2, 1, 3)
    cache_k3 = cache_k.reshape(n_pool, PAGE_SIZE * N_HEADS, HEAD_DIM)
    cache_v3 = cache_v.reshape(n_pool, PAGE_SIZE * N_HEADS, HEAD_DIM)
    kbar_s = _page_means(cache_k3, page_table)

    for j in range(n_b):
        final = j == n_b - 1
        q, bias, z = _q_proj(xp, b_norm[j][None], b_w_in, j, cos_p, sin_p, kbar_p, n_seq)
        q_s, z_s = _q_sample(xs, b_norm[j][None], b_w_in, j, cos_s, sin_s)
        picks = _pick_blocks(q_s, kbar_s, k_s)[:, :, :MOBA_TOPK].reshape(-1)
        o, o_s = _moba_layer(q, bias, kb, vt, q_s, k_s, v_s, cache_k3, cache_v3, page_table, picks)
        xp = _b_out(o.reshape(n_seq * seq, D_MODEL), z, xp, b_w_out, j, final_norm[None], ROW_TILE, final)
        xs = _b_out(o_s.reshape(n_dec, D_MODEL), z_s, xs, b_w_out, j, final_norm[None], n_dec, final)

    return (xp.reshape(n_seq, seq, D_MODEL), xs.reshape(n_dec, 1, D_MODEL),
            jnp.stack(st_p_re), jnp.stack(st_p_im), jnp.stack(st_s_re), jnp.stack(st_s_im),
            k_p.reshape(n_seq, seq, N_HEADS, HEAD_DIM), v_p.reshape(n_seq, seq, N_HEADS, HEAD_DIM),
            k_s.reshape(n_dec, 1, N_HEADS, HEAD_DIM), v_s.reshape(n_dec, 1, N_HEADS, HEAD_DIM))
```

```python
import functools
import math

import jax
import jax.numpy as jnp
import numpy as np
from jax import lax
from jax.experimental import pallas as pl
from jax.experimental.pallas import tpu as pltpu

D_MODEL = 1024
SSM_GROUP = 16
SSM_GROUPS = D_MODEL // SSM_GROUP
SSM_STATE = 64
SSM_TBLK = 16
SSM_BLKW = SSM_TBLK * SSM_GROUP
HEAD_DIM = 128
N_HEADS = D_MODEL // HEAD_DIM
MOBA_BLOCK = 256
MOBA_TOPK = 3
PAGE_SIZE = 128
ROPE_THETA = 10000.0
RMS_EPS = 1e-6
NEG_INF = -1e30
LANES = 128
VMEM_LIMIT = 48 * 1024 * 1024
ROW_TILE = 512
MOBA_HEADS = 4
V_ROWS = HEAD_DIM + 16

F32 = jnp.float32
BF16 = jnp.bfloat16


def _params(*sem):
    return pltpu.CompilerParams(dimension_semantics=sem, vmem_limit_bytes=VMEM_LIMIT)


def _rmsnorm(x, g):
    r = lax.rsqrt(jnp.mean(x * x, axis=-1, keepdims=True) + RMS_EPS)
    return x * r * g


def _dot(a, b):
    return jnp.dot(a, b, preferred_element_type=F32)


def _dot_nt(a, b, precision=None):
    return lax.dot_general(a, b, (((1,), (1,)), ((), ())), precision=precision,
                           preferred_element_type=F32)


def _weight_spec(w, layer):
    if layer is None:
        return pl.BlockSpec(w.shape, lambda *_: (0,) * w.ndim, pipeline_mode=pl.Buffered(1))
    return pl.BlockSpec((None,) + w.shape[1:], lambda *_: (layer,) + (0,) * (w.ndim - 1),
                        pipeline_mode=pl.Buffered(1))


def _cast_once(*pairs):
    @pl.when(pl.program_id(0) == 0)
    def _():
        for w_ref, w_scr in pairs:
            w_scr[...] = w_ref[...].astype(BF16)


def _rope(x, cos2, sin2):
    return x * cos2 + pltpu.roll(x, HEAD_DIM // 2, 1) * sin2


def _silu(z):
    return z * jax.nn.sigmoid(z)


def _top3(s, valid, axis):
    n_lanes = s.shape[axis]
    lane = lax.broadcasted_iota(jnp.int32, s.shape, axis)
    sm = jnp.where(valid, s, NEG_INF)
    sel = jnp.zeros(s.shape, F32)
    picks = []
    for _ in range(MOBA_TOPK):
        m = jnp.max(sm, axis=axis, keepdims=True)
        idx = jnp.min(jnp.where(sm == m, lane, n_lanes), axis=axis, keepdims=True)
        hit = lane == idx
        sel = jnp.where(hit, jnp.where(valid, 1.0, sel), sel)
        sm = jnp.where(hit, NEG_INF, sm)
        picks.append(idx)
    return sel, picks


GROUPS_PER_TILE = LANES // SSM_GROUP
STEP_TILES = SSM_BLKW // LANES


def _slot_transpose(xs, slot):
    xs = list(xs)
    step = GROUPS_PER_TILE // 2
    while step:
        low = (slot & step) == 0
        for i in range(GROUPS_PER_TILE):
            if i & step == 0:
                a, b = xs[i], xs[i + step]
                xs[i] = jnp.where(low, a, pltpu.roll(b, step * SSM_GROUP, 1))
                xs[i + step] = jnp.where(low, pltpu.roll(a, LANES - step * SSM_GROUP, 1), b)
        step //= 2
    return xs


def _to_blocked(u, tiles_ref, ug_ref):
    nb = ug_ref.shape[0]
    slot = lax.broadcasted_iota(jnp.int32, (nb, LANES), 1) // SSM_GROUP
    for tile in range(D_MODEL // LANES):
        tiles_ref[tile] = u[:, tile * LANES:(tile + 1) * LANES]
    for tile in range(D_MODEL // LANES):
        for hh in range(STEP_TILES):
            steps = [tiles_ref[tile, pl.ds(hh * GROUPS_PER_TILE + rp, nb, stride=SSM_TBLK), :]
                     for rp in range(GROUPS_PER_TILE)]
            for gp, piece in enumerate(_slot_transpose(steps, slot)):
                col = (tile * GROUPS_PER_TILE + gp) * SSM_BLKW + hh * LANES
                ug_ref[:, col:col + LANES] = piece.astype(ug_ref.dtype)


def _from_blocked(yg_ref, tiles_ref):
    nb = yg_ref.shape[0]
    slot = lax.broadcasted_iota(jnp.int32, (nb, LANES), 1) // SSM_GROUP
    for tile in range(D_MODEL // LANES):
        for hh in range(STEP_TILES):
            cols = [(tile * GROUPS_PER_TILE + gp) * SSM_BLKW + hh * LANES for gp in range(GROUPS_PER_TILE)]
            for rp, piece in enumerate(_slot_transpose([yg_ref[:, c:c + LANES] for c in cols], slot)):
                tiles_ref[tile, pl.ds(hh * GROUPS_PER_TILE + rp, nb, stride=SSM_TBLK), :] = piece
    return jnp.concatenate([tiles_ref[tile] for tile in range(D_MODEL // LANES)], axis=1)


def _a_in_kernel(blocked, x_ref, g_ref, w_ref, u_ref, z_ref, *rest):
    w_scr = rest[-1]
    _cast_once((w_ref, w_scr))
    h = _rmsnorm(x_ref[...], g_ref[...]).astype(BF16)
    uz = _dot(h, w_scr[...])
    u_ref[...] = uz[:, :D_MODEL]
    z_ref[...] = uz[:, D_MODEL:]
    if blocked:
        ug_ref, tiles_ref = rest[:2]
        _to_blocked(uz[:, :D_MODEL], tiles_ref, ug_ref)


def _a_in(x, g, w, layer, tm, blocked):
    t = x.shape[0]
    tile = pl.BlockSpec((tm, D_MODEL), lambda i: (i, 0))
    out_specs, out_shape = [tile, tile], [jax.ShapeDtypeStruct((t, D_MODEL), F32)] * 2
    if blocked:
        out_specs.append(pl.BlockSpec((tm // SSM_TBLK, SSM_GROUPS * SSM_BLKW), lambda i: (i, 0)))
        out_shape.append(jax.ShapeDtypeStruct((t // SSM_TBLK, SSM_GROUPS * SSM_BLKW), BF16))
    return pl.pallas_call(
        functools.partial(_a_in_kernel, blocked),
        grid=(t // tm,),
        in_specs=[tile, pl.BlockSpec((1, D_MODEL), lambda i: (0, 0)), _weight_spec(w, layer)],
        out_specs=out_specs,
        out_shape=out_shape,
        scratch_shapes=([pltpu.VMEM((D_MODEL // LANES, tm, LANES), F32)] if blocked else [])
        + [pltpu.VMEM((D_MODEL, 2 * D_MODEL), BF16)],
        compiler_params=_params("arbitrary"),
        name="a_in",
    )(x, g, w)


def _ssm_scan_kernel(n_seq, ug_ref, m_ref, f_ref, e_ref, ap_ref, y_ref, st_ref):
    rows = ug_ref.shape[0]
    nblk = rows // n_seq
    ug = ug_ref[...]
    s = _dot(ug, f_ref[...])
    row = lax.broadcasted_iota(jnp.int32, (nblk, LANES), 0)
    hp = []
    for q in range(n_seq):
        re = s[q * nblk:(q + 1) * nblk, :LANES]
        im = s[q * nblk:(q + 1) * nblk, LANES:]
        step, k = 1, 0
        while step < nblk:
            ar = ap_ref[2 * k:2 * k + 1, :]
            ai = ap_ref[2 * k + 1:2 * k + 2, :]
            sr = jnp.where(row >= step, pltpu.roll(re, step, 0), 0.0)
            si = jnp.where(row >= step, pltpu.roll(im, step, 0), 0.0)
            re, im = re + ar * sr - ai * si, im + ar * si + ai * sr
            step, k = step * 2, k + 1
        st_ref[2 * q:2 * q + 1, :] = re[nblk - 1:nblk, :]
        st_ref[2 * q + 1:2 * q + 2, :] = im[nblk - 1:nblk, :]
        pr = jnp.where(row >= 1, pltpu.roll(re, 1, 0), 0.0)
        pi = jnp.where(row >= 1, pltpu.roll(im, 1, 0), 0.0)
        hp.append(jnp.concatenate([pr, pi], axis=1))
    hprev = jnp.concatenate(hp, axis=0).astype(BF16)
    carry = _dot(hprev, e_ref[...])
    y_ref[:, :SSM_BLKW] = _dot(ug[:, :SSM_BLKW], m_ref[0]) + carry[:, :SSM_BLKW]
    y_ref[:, SSM_BLKW:] = _dot(ug[:, SSM_BLKW:], m_ref[1]) + carry[:, SSM_BLKW:]


def _ssm_scan(ug, prep, n_seq):
    rows = ug.shape[0]
    n_pairs = SSM_GROUPS // 2
    n_pow = prep["apow"].shape[1]
    return pl.pallas_call(
        functools.partial(_ssm_scan_kernel, n_seq),
        grid=(n_pairs,),
        in_specs=[pl.BlockSpec((rows, 2 * SSM_BLKW), lambda p: (0, p)),
                  pl.BlockSpec((2, SSM_BLKW, SSM_BLKW), lambda p: (p, 0, 0)),
                  pl.BlockSpec((None, 2 * SSM_BLKW, 2 * LANES), lambda p: (p, 0, 0)),
                  pl.BlockSpec((None, 2 * LANES, 2 * SSM_BLKW), lambda p: (p, 0, 0)),
                  pl.BlockSpec((None, n_pow, LANES), lambda p: (p, 0, 0))],
        out_specs=[pl.BlockSpec((rows, 2 * SSM_BLKW), lambda p: (0, p)),
                   pl.BlockSpec((None, 2 * n_seq, LANES), lambda p: (p, 0, 0))],
        out_shape=[jax.ShapeDtypeStruct((rows, SSM_GROUPS * SSM_BLKW), F32),
                   jax.ShapeDtypeStruct((n_pairs, 2 * n_seq, LANES), F32)],
        compiler_params=_params("parallel"),
        name="ssm_scan",
    )(ug, prep["m"], prep["f"], prep["e"], prep["apow"])


def _ssm_step_kernel(u_ref, hr_ref, hi_ref, bb_ref, ar_ref, ai_ref, cs_ref, y_ref, or_ref, oi_ref):
    bu = lax.dot_general(u_ref[...], bb_ref[...], (((2,), (1,)), ((0,), (0,))),
                         preferred_element_type=F32)
    h0r, h0i = hr_ref[...], hi_ref[...]
    ar, ai = ar_ref[...], ai_ref[...]
    hr = bu[:, :, :SSM_STATE] + ar * h0r - ai * h0i
    hi = bu[:, :, SSM_STATE:] + ar * h0i + ai * h0r
    or_ref[...] = hr
    oi_ref[...] = hi
    y_ref[...] = lax.dot_general(jnp.concatenate([hr, hi], axis=2), cs_ref[...],
                                 (((2,), (1,)), ((0,), (0,))), preferred_element_type=F32)


def _ssm_step(u_g, h0r, h0i, prep):
    g, n, _ = u_g.shape
    return pl.pallas_call(
        _ssm_step_kernel,
        out_shape=[jax.ShapeDtypeStruct((g, n, SSM_GROUP), F32),
                   jax.ShapeDtypeStruct((g, n, SSM_STATE), F32),
                   jax.ShapeDtypeStruct((g, n, SSM_STATE), F32)],
        compiler_params=pltpu.CompilerParams(vmem_limit_bytes=VMEM_LIMIT),
        name="ssm_step",
    )(u_g, h0r, h0i, prep["bb"], prep["a_re"], prep["a_im"], prep["cs"])


def _a_out_kernel(blocked, ys_ref, u_ref, z_ref, x_ref, d_ref, wg_ref, wo_ref, o_ref, *scratch):
    wg_scr, wo_scr = scratch[-2:]
    _cast_once((wg_ref, wg_scr), (wo_ref, wo_scr))
    ys = _from_blocked(ys_ref, scratch[0]) if blocked else ys_ref[...]
    y = ys + d_ref[...] * u_ref[...]
    g = jax.nn.gelu(y)
    y2 = g * jax.nn.sigmoid(_dot(g.astype(BF16), wg_scr[...]))
    v = (y2 * _silu(z_ref[...])).astype(BF16)
    o_ref[...] = x_ref[...] + _dot(v, wo_scr[...])


def _a_out(ys, u, z, x, d, wg, wo, layer, tm, blocked):
    t = x.shape[0]
    tile = pl.BlockSpec((tm, D_MODEL), lambda i: (i, 0))
    ys_spec = pl.BlockSpec((tm // SSM_TBLK, SSM_GROUPS * SSM_BLKW), lambda i: (i, 0)) if blocked else tile
    return pl.pallas_call(
        functools.partial(_a_out_kernel, blocked),
        grid=(t // tm,),
        in_specs=[ys_spec, tile, tile, tile, pl.BlockSpec((1, D_MODEL), lambda i: (0, 0)),
                  _weight_spec(wg, layer), _weight_spec(wo, layer)],
        out_specs=tile,
        out_shape=jax.ShapeDtypeStruct((t, D_MODEL), F32),
        scratch_shapes=([pltpu.VMEM((D_MODEL // LANES, tm, LANES), F32)] if blocked else [])
        + [pltpu.VMEM((D_MODEL, D_MODEL), BF16)] * 2,
        compiler_params=_params("arbitrary"),
        name="a_out",
    )(ys, u, z, x, d, wg, wo)


def _ssm_prep_kernel(n_steps, lrc_ref, lic_ref, lrr_ref, lir_ref, ldt_ref, bt_re_ref, bt_im_ref, ct_re_ref, ct_im_ref,
                     m_ref, f_ref, e_ref, ap_ref, ab_ref, bb_ref, cs_ref, m32, f32s, e32):
    hi = lax.Precision.HIGHEST
    p = SSM_STATE
    m32[...] = jnp.zeros(m32.shape, F32)
    f32s[...] = jnp.zeros(f32s.shape, F32)
    e32[...] = jnp.zeros(e32.shape, F32)
    for gi in range(2):
        dt = jnp.exp(ldt_ref[gi])

        def discretise(lr, li):
            mag = jnp.exp(lr * dt)
            a_re, a_im = mag * jnp.cos(li * dt), mag * jnp.sin(li * dt)
            den = lr * lr + li * li
            f_re = ((a_re - 1.0) * lr + a_im * li) / den
            f_im = (a_im * lr - (a_re - 1.0) * li) / den
            return a_re, a_im, f_re, f_im

        ac_re, ac_im, _, _ = discretise(lrc_ref[gi], lic_ref[gi])
        lrr, lir = lrr_ref[gi], lir_ref[gi]
        ar_re, ar_im, fr_re, fr_im = discretise(lrr, lir)
        ab_ref[gi, 0:1, :] = ar_re
        ab_ref[gi, 1:2, :] = ar_im
        ct_re, ct_im = ct_re_ref[gi], ct_im_ref[gi]
        cs_ref[gi] = jnp.concatenate([ct_re, -ct_im], axis=0)
        bbt_re = fr_re * bt_re_ref[gi] - fr_im * bt_im_ref[gi]
        bbt_im = fr_re * bt_im_ref[gi] + fr_im * bt_re_ref[gi]
        bb_ref[gi, :, :p] = bbt_re
        bb_ref[gi, :, p:] = bbt_im
        for k in range(n_steps):
            n = float(SSM_TBLK * 2 ** k)
            mag = jnp.exp(lrr * dt * n)
            ap_ref[2 * k:2 * k + 1, gi * p:(gi + 1) * p] = mag * jnp.cos(lir * dt * n)
            ap_ref[2 * k + 1:2 * k + 2, gi * p:(gi + 1) * p] = mag * jnp.sin(lir * dt * n)
        pc_re, pc_im = jnp.ones_like(ac_re), jnp.zeros_like(ac_im)
        pr_re, pr_im = jnp.ones_like(ar_re), jnp.zeros_like(ar_im)
        for t in range(SSM_TBLK + 1):
            if t:
                pc_re, pc_im = pc_re * ac_re - pc_im * ac_im, pc_re * ac_im + pc_im * ac_re
                pr_re, pr_im = pr_re * ar_re - pr_im * ar_im, pr_re * ar_im + pr_im * ar_re
            ca_re = pc_re * ct_re - pc_im * ct_im
            nca_im = -(pc_re * ct_im + pc_im * ct_re)
            if t:
                cols = slice(gi * SSM_BLKW + (t - 1) * SSM_GROUP, gi * SSM_BLKW + t * SSM_GROUP)
                e32[gi * p:(gi + 1) * p, cols] = ca_re
                e32[(2 + gi) * p:(3 + gi) * p, cols] = nca_im
            if t < SSM_TBLK:
                r = SSM_TBLK - 1 - t
                rows = slice(gi * SSM_BLKW + r * SSM_GROUP, gi * SSM_BLKW + (r + 1) * SSM_GROUP)
                f32s[rows, gi * p:(gi + 1) * p] = pr_re * bbt_re - pr_im * bbt_im
                f32s[rows, (2 + gi) * p:(3 + gi) * p] = pr_re * bbt_im + pr_im * bbt_re
                kt = (jnp.dot(bbt_re, ca_re, precision=hi, preferred_element_type=F32)
                      + jnp.dot(bbt_im, nca_im, precision=hi, preferred_element_type=F32))
                for r_in in range(SSM_TBLK - t):
                    r_out = r_in + t
                    m32[gi, r_in * SSM_GROUP:(r_in + 1) * SSM_GROUP, r_out * SSM_GROUP:(r_out + 1) * SSM_GROUP] = kt
    m_ref[...] = m32[...].astype(BF16)
    f_ref[...] = f32s[...].astype(BF16)
    e_ref[...] = e32[...].astype(BF16)


def _ssm_prep(lam_re, lam_im, log_dt, b_re, b_im, c_re, c_im, nblk):
    g, p = lam_re.shape
    n_pairs = g // 2
    n_steps = max(1, math.ceil(math.log2(nblk)))
    col = lambda x: x.reshape(g, p, 1)
    row = lambda x: x.reshape(g, 1, p)
    tr = lambda x: x.transpose(0, 2, 1)

    def two(*shape):
        return pl.BlockSpec((2,) + shape, lambda i: (i,) + (0,) * len(shape))

    def one(*shape):
        return pl.BlockSpec((None,) + shape, lambda i: (i,) + (0,) * len(shape))

    m, fp, ep, ap, ab, bb, cs = pl.pallas_call(
        functools.partial(_ssm_prep_kernel, n_steps),
        grid=(n_pairs,),
        in_specs=[two(p, 1), two(p, 1), two(1, p), two(1, p), two(1, 1),
                  two(SSM_GROUP, p), two(SSM_GROUP, p), two(p, SSM_GROUP), two(p, SSM_GROUP)],
        out_specs=[two(SSM_BLKW, SSM_BLKW), one(2 * SSM_BLKW, 4 * p), one(4 * p, 2 * SSM_BLKW),
                   one(2 * n_steps, 2 * p), two(2, p), two(SSM_GROUP, 2 * p), two(2 * p, SSM_GROUP)],
        out_shape=[jax.ShapeDtypeStruct((g, SSM_BLKW, SSM_BLKW), BF16),
                   jax.ShapeDtypeStruct((n_pairs, 2 * SSM_BLKW, 4 * p), BF16),
                   jax.ShapeDtypeStruct((n_pairs, 4 * p, 2 * SSM_BLKW), BF16),
                   jax.ShapeDtypeStruct((n_pairs, 2 * n_steps, 2 * p), F32),
                   jax.ShapeDtypeStruct((g, 2, p), F32),
                   jax.ShapeDtypeStruct((g, SSM_GROUP, 2 * p), F32),
                   jax.ShapeDtypeStruct((g, 2 * p, SSM_GROUP), F32)],
        scratch_shapes=[pltpu.VMEM((2, SSM_BLKW, SSM_BLKW), F32), pltpu.VMEM((2 * SSM_BLKW, 4 * p), F32),
                        pltpu.VMEM((4 * p, 2 * SSM_BLKW), F32)],
        compiler_params=_params("parallel"),
        name="ssm_prep",
    )(col(lam_re), col(lam_im), row(lam_re), row(lam_im), log_dt.reshape(g, 1, 1),
      tr(b_re), tr(b_im), tr(c_re), tr(c_im))
    return {"m": m, "f": fp, "e": ep, "apow": ap, "bb": bb,
            "a_re": ab[:, 0:1, :], "a_im": ab[:, 1:2, :], "cs": cs}


def _s5_layer_prompt(x, n_seq, layer, norm, w_in, prep, d, wg, wo):
    u, z, ug = _a_in(x, norm, w_in, layer, ROW_TILE, True)
    yg, st = _ssm_scan(ug, prep, n_seq)
    st = st.reshape(SSM_GROUPS // 2, n_seq, 2, 2, SSM_STATE)
    st = st.transpose(2, 1, 0, 3, 4).reshape(2, n_seq, SSM_GROUPS, SSM_STATE)
    return _a_out(yg, u, z, x, d, wg, wo, layer, ROW_TILE, True), st[0], st[1]


def _s5_layer_sample(x, h0r, h0i, layer, norm, w_in, prep, d, wg, wo):
    n = x.shape[0]
    u, z = _a_in(x, norm, w_in, layer, n, False)
    u_g = u.reshape(n, SSM_GROUPS, SSM_GROUP).transpose(1, 0, 2)
    y_g, hr, hi = _ssm_step(u_g, h0r.transpose(1, 0, 2), h0i.transpose(1, 0, 2), prep)
    ys = y_g.transpose(1, 0, 2).reshape(n, D_MODEL)
    return _a_out(ys, u, z, x, d, wg, wo, layer, n, False), hr.transpose(1, 0, 2), hi.transpose(1, 0, 2)


def _kv_kernel(prompt, x_ref, g_ref, w_ref, cos_ref, sin_ref, k_ref, v_ref, *rest):
    w_scr = rest[-1]
    _cast_once((w_ref, w_scr))
    h = _rmsnorm(x_ref[...], g_ref[...]).astype(BF16)
    kv = _dot(h, w_scr[...])
    cos2, sin2 = cos_ref[...], sin_ref[...]
    tm = kv.shape[0]
    for hd in range(N_HEADS):
        sl = slice(hd * HEAD_DIM, (hd + 1) * HEAD_DIM)
        k = _rope(kv[:, sl], cos2, sin2)
        v = kv[:, D_MODEL + hd * HEAD_DIM:D_MODEL + (hd + 1) * HEAD_DIM]
        k_ref[:, sl] = k
        v_ref[:, sl] = v
        if prompt:
            kb_ref, vt_ref, mean_ref = rest[:3]
            kb_ref[hd] = k.astype(BF16)
            for b in range(tm // MOBA_BLOCK):
                rows = slice(b * MOBA_BLOCK, (b + 1) * MOBA_BLOCK)
                mean_ref[b, :, sl] = jnp.sum(k[rows], axis=0, keepdims=True) * (1.0 / MOBA_BLOCK)
                vt_ref[hd, b, :HEAD_DIM, :] = v[rows].T.astype(BF16)
                vt_ref[hd, b, HEAD_DIM:, :] = jnp.ones((V_ROWS - HEAD_DIM, MOBA_BLOCK), BF16)


def _kv_proj(x, g, w, cos2, sin2, n_seq, tm, prompt):
    t = x.shape[0]
    per_seq = t // n_seq // tm
    tile = pl.BlockSpec((tm, D_MODEL), lambda i: (i, 0))
    rope_tile = pl.BlockSpec((tm, HEAD_DIM), lambda i: (i % per_seq, 0))
    out_specs = [tile, tile]
    out_shape = [jax.ShapeDtypeStruct((t, D_MODEL), F32)] * 2
    if prompt:
        nb = tm // MOBA_BLOCK
        out_specs += [
            pl.BlockSpec((None, N_HEADS, tm, HEAD_DIM), lambda i: (i // per_seq, 0, i % per_seq, 0)),
            pl.BlockSpec((None, N_HEADS, nb, V_ROWS, MOBA_BLOCK), lambda i: (i // per_seq, 0, i % per_seq, 0, 0)),
            pl.BlockSpec((nb, 1, D_MODEL), lambda i: (i, 0, 0))]
        out_shape += [
            jax.ShapeDtypeStruct((n_seq, N_HEADS, t // n_seq, HEAD_DIM), BF16),
            jax.ShapeDtypeStruct((n_seq, N_HEADS, t // n_seq // MOBA_BLOCK, V_ROWS, MOBA_BLOCK), BF16),
            jax.ShapeDtypeStruct((t // MOBA_BLOCK, 1, D_MODEL), F32)]
    return pl.pallas_call(
        functools.partial(_kv_kernel, prompt),
        grid=(t // tm,),
        in_specs=[tile, pl.BlockSpec((1, D_MODEL), lambda i: (0, 0)), _weight_spec(w, None),
                  rope_tile, rope_tile],
        out_specs=out_specs,
        out_shape=out_shape,
        scratch_shapes=[pltpu.VMEM((D_MODEL, 2 * D_MODEL), BF16)],
        compiler_params=_params("arbitrary"),
        name="kv_proj",
    )(x, g, w, cos2, sin2)


def _q_kernel(blocks_per_seq, x_ref, g_ref, w_ref, cos_ref, sin_ref, kbar_ref, q_ref, bias_ref, z_ref, w_scr):
    _cast_once((w_ref, w_scr))
    tm = x_ref.shape[0]
    per_tile = tm // MOBA_BLOCK
    first = (pl.program_id(0) * per_tile) % blocks_per_seq
    h = _rmsnorm(x_ref[...], g_ref[...]).astype(BF16)
    qz = _dot(h, w_scr[...])
    z_ref[...] = qz[:, D_MODEL:]
    cos2, sin2 = cos_ref[...], sin_ref[...]
    blk = lax.broadcasted_iota(jnp.int32, (blocks_per_seq, MOBA_BLOCK), 0)
    for hd in range(N_HEADS):
        q = _rope(qz[:, hd * HEAD_DIM:(hd + 1) * HEAD_DIM], cos2, sin2)
        q_ref[hd] = (q * (HEAD_DIM ** -0.5)).astype(BF16)
        for b in range(per_tile):
            rows = slice(b * MOBA_BLOCK, (b + 1) * MOBA_BLOCK)
            s = _dot_nt(kbar_ref[hd], q[rows], precision=lax.Precision.HIGHEST)
            sel, _ = _top3(s, blk < first + b, 0)
            bias_ref[hd, b] = jnp.where(sel > 0.5, 0.0, NEG_INF)


def _q_proj(x, g, w, layer, cos2, sin2, kbar, n_seq):
    t = x.shape[0]
    tm = ROW_TILE
    bps = t // n_seq // MOBA_BLOCK
    per_tile = tm // MOBA_BLOCK
    tps = bps // per_tile
    tile = pl.BlockSpec((tm, D_MODEL), lambda i: (i, 0))
    rope_tile = pl.BlockSpec((tm, HEAD_DIM), lambda i: (i % tps, 0))
    return pl.pallas_call(
        functools.partial(_q_kernel, bps),
        grid=(t // tm,),
        in_specs=[tile, pl.BlockSpec((1, D_MODEL), lambda i: (0, 0)), _weight_spec(w, layer),
                  rope_tile, rope_tile,
                  pl.BlockSpec((None, N_HEADS, bps, HEAD_DIM), lambda i: (i // tps, 0, 0, 0))],
        out_specs=[pl.BlockSpec((None, N_HEADS, tm, HEAD_DIM), lambda i: (i // tps, 0, i % tps, 0)),
                   pl.BlockSpec((None, N_HEADS, per_tile, bps, MOBA_BLOCK), lambda i: (i // tps, 0, i % tps, 0, 0)),
                   tile],
        out_shape=[jax.ShapeDtypeStruct((n_seq, N_HEADS, t // n_seq, HEAD_DIM), BF16),
                   jax.ShapeDtypeStruct((n_seq, N_HEADS, bps, bps, MOBA_BLOCK), F32),
                   jax.ShapeDtypeStruct((t, D_MODEL), F32)],
        scratch_shapes=[pltpu.VMEM((D_MODEL, 2 * D_MODEL), BF16)],
        compiler_params=_params("arbitrary"),
        name="q_proj",
    )(x, g, w, cos2, sin2, kbar)


def _moba_layer_kernel(items, pk_ref, pt_ref, q_ref, bias_ref, k_ref, vt_ref, qs_ref, kn_ref, vn_ref, *rest):
    n_pages = items * N_FETCH
    k_pages, v_pages = rest[:n_pages], rest[n_pages:2 * n_pages]
    o_ref, os_ref, s_a, s_b, acc_scr = rest[2 * n_pages:]
    step = (pl.program_id(0) * pl.num_programs(1) + pl.program_id(1)) * pl.num_programs(2) + pl.program_id(2)
    head0 = (step * items) % N_HEADS
    for a in range(items):
        pages = slice(a * N_FETCH, (a + 1) * N_FETCH)
        os_ref[a] = _decode_attend(head0 + a, qs_ref[a], kn_ref[a], vn_ref[a], k_pages[pages], v_pages[pages])
    _moba_tile(q_ref, bias_ref, k_ref, vt_ref, o_ref, s_a, s_b, acc_scr)


def _moba_tile(q_ref, bias_ref, k_ref, vt_ref, o_ref, s_a, s_b, acc_scr):
    i = pl.program_id(2)
    nblk = bias_ref.shape[1]
    row = lax.broadcasted_iota(jnp.int32, (MOBA_BLOCK, LANES), 0)
    col = lax.broadcasted_iota(jnp.int32, (MOBA_BLOCK, LANES), 1)
    chains = [(h, c) for h in range(MOBA_HEADS) for c in range(MOBA_BLOCK // LANES)]
    nc = len(chains)
    trips = (i + 1) // 2
    last_trip = nblk // 2 - 1

    def scores(h, c, blk):
        off = pl.multiple_of(blk * MOBA_BLOCK, MOBA_BLOCK)
        return _dot_nt(k_ref[h, pl.ds(off, MOBA_BLOCK), :], q_ref[h, c * LANES:(c + 1) * LANES, :])

    def stage_scores(t, buf):
        mbs = []
        for n, (h, c) in enumerate(chains):
            for u in range(2):
                s = scores(h, c, 2 * t + u)
                buf[u * nc + n] = s
                mbs.append(jnp.max(s, axis=0, keepdims=True))
        return tuple(mbs)

    def consume(t, buf, ms, mbs):
        out = []
        for n, (h, c) in enumerate(chains):
            b = [bias_ref[h, pl.ds(2 * t + u, 1), :][:, c * LANES:(c + 1) * LANES] for u in range(2)]
            m_new = jnp.maximum(ms[n], jnp.maximum(mbs[2 * n] + b[0], mbs[2 * n + 1] + b[1]))
            alpha = jnp.exp(ms[n] - m_new)
            p = [jnp.exp(buf[u * nc + n] - (m_new - b[u])).astype(BF16) for u in range(2)]
            acc_scr[n] = alpha * acc_scr[n] + (_dot(vt_ref[h, 2 * t], p[0]) + _dot(vt_ref[h, 2 * t + 1], p[1]))
            out.append(m_new)
        return tuple(out)

    for n, (h, c) in enumerate(chains):
        s_b[n] = scores(h, c, i)
    mbs_a = stage_scores(0, s_a)
    ms = []
    for n, (h, c) in enumerate(chains):
        s = jnp.where(row <= col + c * LANES, s_b[n], NEG_INF)
        m = jnp.max(s, axis=0, keepdims=True)
        acc_scr[n] = _dot(vt_ref[h, i], jnp.exp(s - m).astype(BF16))
        ms.append(m)

    def body(tt, carry):
        ms, mbs_a = carry
        t0 = 2 * tt
        mbs_b = stage_scores(t0 + 1, s_b)
        ms = consume(t0, s_a, ms, mbs_a)
        mbs_a = stage_scores(jnp.minimum(t0 + 2, last_trip), s_a)
        ms = consume(t0 + 1, s_b, ms, mbs_b)
        return ms, mbs_a

    lax.fori_loop(0, (trips + 1) // 2, body, (tuple(ms), mbs_a))
    for n, (h, c) in enumerate(chains):
        acc = acc_scr[n]
        o = acc[:HEAD_DIM, :] / acc[HEAD_DIM:HEAD_DIM + 1, :]
        o_ref[c * LANES:(c + 1) * LANES, h * HEAD_DIM:(h + 1) * HEAD_DIM] = o.T.astype(o_ref.dtype)


def _moba_layer(q, bias, kb, vt, q_s, k_new, v_new, cache_k3, cache_v3, page_table, picks):
    n_seq, _, seq, _ = q.shape
    n_dec, n_pages = page_table.shape
    nblk = seq // MOBA_BLOCK
    hg = MOBA_HEADS
    n_hg = N_HEADS // hg
    n_chains = hg * (MOBA_BLOCK // LANES)
    steps = n_seq * n_hg * nblk
    items = n_dec * N_HEADS // steps
    assert items * steps == n_dec * N_HEADS and N_HEADS % items == 0
    per_block = MOBA_BLOCK // PAGE_SIZE

    def first_item(n, h, i):
        return ((n * n_hg + h) * nblk + i) * items

    def dec_index(n, h, i, pk, pt):
        e = first_item(n, h, i)
        return (e // N_HEADS, (e % N_HEADS) // items, 0, 0)

    def page_spec(a, f):
        def index(n, h, i, pk, pt):
            e = first_item(n, h, i) + a
            blk = pk[e * MOBA_TOPK + f // per_block]
            return (pt[(e // N_HEADS) * n_pages + blk * per_block + f % per_block], 0, 0)
        return pl.BlockSpec((None, PAGE_SIZE * N_HEADS, HEAD_DIM), index)

    dec_rows = pl.BlockSpec((None, items, 1, HEAD_DIM), dec_index)
    page_specs = [page_spec(a, f) for a in range(items) for f in range(N_FETCH)]
    resident = pl.Buffered(1)
    dec_shape = (n_dec, N_HEADS, 1, HEAD_DIM)
    return pl.pallas_call(
        functools.partial(_moba_layer_kernel, items),
        grid_spec=pltpu.PrefetchScalarGridSpec(
            num_scalar_prefetch=2,
            grid=(n_seq, n_hg, nblk),
            in_specs=[pl.BlockSpec((None, hg, MOBA_BLOCK, HEAD_DIM), lambda n, h, i, pk, pt: (n, h, i, 0)),
                      pl.BlockSpec((None, hg, None, nblk, MOBA_BLOCK), lambda n, h, i, pk, pt: (n, h, i, 0, 0)),
                      pl.BlockSpec((None, hg, seq, HEAD_DIM), lambda n, h, i, pk, pt: (n, h, 0, 0),
                                   pipeline_mode=resident),
                      pl.BlockSpec((None, hg, nblk, V_ROWS, MOBA_BLOCK), lambda n, h, i, pk, pt: (n, h, 0, 0, 0),
                                   pipeline_mode=resident),
                      dec_rows, dec_rows, dec_rows] + page_specs + page_specs,
            out_specs=[pl.BlockSpec((None, MOBA_BLOCK, hg * HEAD_DIM), lambda n, h, i, pk, pt: (n, i, h)),
                       dec_rows],
            scratch_shapes=[pltpu.VMEM((2 * n_chains, MOBA_BLOCK, LANES), F32),
                            pltpu.VMEM((2 * n_chains, MOBA_BLOCK, LANES), F32),
                            pltpu.VMEM((n_chains, V_ROWS, LANES), F32)]),
        out_shape=[jax.ShapeDtypeStruct((n_seq, seq, D_MODEL), BF16),
                   jax.ShapeDtypeStruct(dec_shape, F32)],
        compiler_params=_params("arbitrary", "arbitrary", "arbitrary"),
        name="moba_layer",
    )(picks, page_table.reshape(-1), q, bias, kb, vt,
      q_s.reshape(dec_shape), k_new.reshape(dec_shape), v_new.reshape(dec_shape),
      *([cache_k3] * len(page_specs)), *([cache_v3] * len(page_specs)))


def _b_out_kernel(final, o_ref, z_ref, x_ref, w_ref, g_ref, y_ref, w_scr):
    _cast_once((w_ref, w_scr))
    v = (o_ref[...].astype(F32) * _silu(z_ref[...])).astype(BF16)
    x = x_ref[...] + _dot(v, w_scr[...])
    y_ref[...] = _rmsnorm(x, g_ref[...]) if final else x


def _b_out(o, z, x, w, layer, g, tm, final):
    t = x.shape[0]
    tile = pl.BlockSpec((tm, D_MODEL), lambda i: (i, 0))
    return pl.pallas_call(
        functools.partial(_b_out_kernel, final),
        grid=(t // tm,),
        in_specs=[tile, tile, tile, _weight_spec(w, layer), pl.BlockSpec((1, D_MODEL), lambda i: (0, 0))],
        out_specs=tile,
        out_shape=jax.ShapeDtypeStruct((t, D_MODEL), F32),
        scratch_shapes=[pltpu.VMEM((D_MODEL, D_MODEL), BF16)],
        compiler_params=_params("arbitrary"),
        name="b_out",
    )(o, z, x, w, g)


PAGES_PER_STEP = 32


def _page_mean_kernel(*refs):
    pages, out_ref = refs[1:1 + PAGES_PER_STEP], refs[-1]
    per_block = MOBA_BLOCK // PAGE_SIZE
    for b in range(PAGES_PER_STEP // per_block):
        tot = None
        for r in range(per_block):
            page = pages[per_block * b + r][...].reshape(PAGE_SIZE, N_HEADS, HEAD_DIM)
            part = jnp.sum(page, axis=0)
            tot = part if tot is None else tot + part
        tot = tot * (1.0 / MOBA_BLOCK)
        for hd in range(N_HEADS):
            out_ref[hd, b:b + 1, :] = tot[hd:hd + 1, :]


def _page_means(cache_k2, page_table):
    n, n_pages = page_table.shape
    steps = n_pages // PAGES_PER_STEP
    nb = PAGES_PER_STEP * PAGE_SIZE // MOBA_BLOCK

    def page_spec(r):
        return pl.BlockSpec((None, PAGE_SIZE * N_HEADS, HEAD_DIM),
                            lambda s, j, pt, r=r: (pt[s * n_pages + j * PAGES_PER_STEP + r], 0, 0))

    return pl.pallas_call(
        _page_mean_kernel,
        grid_spec=pltpu.PrefetchScalarGridSpec(
            num_scalar_prefetch=1,
            grid=(n, steps),
            in_specs=[page_spec(r) for r in range(PAGES_PER_STEP)],
            out_specs=pl.BlockSpec((None, N_HEADS, nb, HEAD_DIM), lambda s, j, pt: (s, 0, j, 0))),
        out_shape=jax.ShapeDtypeStruct((n, N_HEADS, n_pages * PAGE_SIZE // MOBA_BLOCK, HEAD_DIM), F32),
        compiler_params=_params("parallel", "arbitrary"),
        name="page_means",
    )(page_table.reshape(-1), *([cache_k2] * PAGES_PER_STEP))


def _q_sample_kernel(x_ref, g_ref, w_ref, cos_ref, sin_ref, q_ref, z_ref):
    h = _rmsnorm(x_ref[...], g_ref[...]).astype(BF16)
    qz = _dot(h, w_ref[...].astype(BF16))
    z_ref[...] = qz[:, D_MODEL:]
    cos2, sin2 = cos_ref[...], sin_ref[...]
    for hd in range(N_HEADS):
        sl = slice(hd * HEAD_DIM, (hd + 1) * HEAD_DIM)
        q_ref[:, sl] = _rope(qz[:, sl], cos2, sin2)


def _q_sample(x, g, w, layer, cos2, sin2):
    n = x.shape[0]
    whole = lambda a: pl.BlockSpec(a.shape, lambda i: (0,) * a.ndim)
    return pl.pallas_call(
        _q_sample_kernel,
        grid=(1,),
        in_specs=[whole(x), whole(g), _weight_spec(w, layer), whole(cos2), whole(sin2)],
        out_specs=[pl.BlockSpec((n, D_MODEL), lambda i: (0, 0))] * 2,
        out_shape=[jax.ShapeDtypeStruct((n, D_MODEL), F32)] * 2,
        compiler_params=_params("arbitrary"),
        name="q_sample",
    )(x, g, w, cos2, sin2)


PICK_UNROLL = 4


def _pick_kernel(q_ref, kbar_ref, knew_ref, idx_ref):
    n_seq, _, n_past, _ = kbar_ref.shape
    lane_o = lax.broadcasted_iota(jnp.int32, (8, LANES), 1)

    def one_sequence(sq):
        q_row, knew_row = q_ref[pl.ds(sq, 1), :], knew_ref[pl.ds(sq, 1), :]
        scores = None
        for hd in range(N_HEADS):
            sl = slice(hd * HEAD_DIM, (hd + 1) * HEAD_DIM)
            q8 = jnp.broadcast_to(q_row[:, sl], (8, HEAD_DIM))
            own_mean = jnp.broadcast_to(knew_row[:, sl] * (1.0 / MOBA_BLOCK), (8, HEAD_DIM))
            means = jnp.concatenate([kbar_ref[sq, hd], own_mean], axis=0)
            s = _dot_nt(q8, means, precision=lax.Precision.HIGHEST)
            head_row = lax.broadcasted_iota(jnp.int32, s.shape, 0) == hd
            scores = s if scores is None else jnp.where(head_row, s, scores)
        lane = lax.broadcasted_iota(jnp.int32, scores.shape, 1)
        _, picks = _top3(scores, lane < n_past, 1)
        out = jnp.zeros((8, LANES), jnp.int32)
        for r, idx in enumerate(picks):
            out = jnp.where(lane_o == r, idx, out)
        idx_ref[sq] = out

    def some_sequences(i, carry):
        for j in range(PICK_UNROLL):
            one_sequence(i * PICK_UNROLL + j)
        return carry

    lax.fori_loop(0, n_seq // PICK_UNROLL, some_sequences, 0)


def _pick_blocks(q, kbar, knew):
    n = kbar.shape[0]
    assert n % PICK_UNROLL == 0
    return pl.pallas_call(
        _pick_kernel,
        out_shape=jax.ShapeDtypeStruct((n, 8, LANES), jnp.int32),
        compiler_params=pltpu.CompilerParams(vmem_limit_bytes=VMEM_LIMIT),
        name="pick_blocks",
    )(q, kbar, knew)


N_FETCH = MOBA_TOPK * (MOBA_BLOCK // PAGE_SIZE)


def _decode_attend(head, q, k_new, v_new, k_pages, v_pages):
    scale = HEAD_DIM ** -0.5
    q8 = jnp.broadcast_to(q, (8, HEAD_DIM)).astype(BF16)
    head_rows = pl.ds(head, PAGE_SIZE, stride=N_HEADS)
    kc = jnp.concatenate([r[head_rows, :].astype(BF16) for r in k_pages], axis=0)
    vc = jnp.concatenate([r[head_rows, :].astype(BF16) for r in v_pages], axis=0)
    s = _dot_nt(q8, kc)[0:1, :] * scale
    s_new = jnp.sum(q * k_new, axis=1, keepdims=True) * scale
    m = jnp.maximum(jnp.max(s, axis=1, keepdims=True), s_new)
    p = jnp.exp(s - m)
    p_new = jnp.exp(s_new - m)
    l = jnp.sum(p, axis=1, keepdims=True) + p_new
    pv = _dot(jnp.broadcast_to(p, (8, p.shape[1])).astype(BF16), vc)[0:1, :]
    return (pv + p_new * v_new) / l


def _rope_tables(pos):
    half = HEAD_DIM // 2
    inv = ROPE_THETA ** (-np.arange(half, dtype=np.float64) / half)
    ang = np.asarray(pos, np.float64)[:, None] * inv[None, :]
    cos, sin = np.cos(ang), np.sin(ang)
    return (jnp.asarray(np.concatenate([cos, cos], axis=1), F32),
            jnp.asarray(np.concatenate([-sin, sin], axis=1), F32))


def kernel(x_prompt, x_sample, state_ssm_re, state_ssm_im, cache_k, cache_v, page_table, a_norm, a_w_in, a_lam_re, a_lam_im, a_log_dt, a_b_re, a_b_im, a_c_re, a_c_im, a_d, a_w_glu, a_w_out, kv_norm, w_kv, b_norm, b_w_in, b_w_out, final_norm):
    n_seq, seq, _ = x_prompt.shape
    n_dec = x_sample.shape[0]
    n_pool = cache_k.shape[0]
    past_len = page_table.shape[1] * PAGE_SIZE
    assert x_sample.shape[1] == 1 and seq % ROW_TILE == 0 and ROW_TILE % MOBA_BLOCK == 0
    assert past_len % (PAGES_PER_STEP * PAGE_SIZE) == 0 and past_len // MOBA_BLOCK >= MOBA_TOPK
    assert seq // MOBA_BLOCK <= LANES and (seq // MOBA_BLOCK) % 8 == 0
    n_a, n_b = a_norm.shape[0], b_norm.shape[0]

    xp = x_prompt.reshape(n_seq * seq, D_MODEL)
    xs = x_sample.reshape(n_dec, D_MODEL)
    cos_p, sin_p = _rope_tables(np.arange(seq))
    cos_s, sin_s = _rope_tables(np.full((n_dec,), past_len))

    st_p_re, st_p_im, st_s_re, st_s_im = [], [], [], []
    for l in range(n_a):
        prep = _ssm_prep(a_lam_re[l], a_lam_im[l], a_log_dt[l], a_b_re[l], a_b_im[l], a_c_re[l], a_c_im[l],
                         seq // SSM_TBLK)
        norm, d = a_norm[l][None], a_d[l][None]
        xp, hr, hi = _s5_layer_prompt(xp, n_seq, l, norm, a_w_in, prep, d, a_w_glu, a_w_out)
        st_p_re.append(hr)
        st_p_im.append(hi)
        xs, hr, hi = _s5_layer_sample(xs, state_ssm_re[l], state_ssm_im[l], l, norm, a_w_in, prep, d, a_w_glu, a_w_out)
        st_s_re.append(hr)
        st_s_im.append(hi)

    k_p, v_p, kb, vt, kbar_p = _kv_proj(xp, kv_norm[None], w_kv, cos_p, sin_p, n_seq, ROW_TILE, True)
    k_s, v_s = _kv_proj(xs, kv_norm[None], w_kv, cos_s, sin_s, 1, n_dec, False)
    nblk = seq // MOBA_BLOCK
    kbar_p = kbar_p.reshape(n_seq, nblk, N_HEADS, HEAD_DIM).transpose(0, 2, 1, 3)
    cache_k3 = cache_k.reshape(n_pool, PAGE_SIZE * N_HEADS, HEAD_DIM)
    cache_v3 = cache_v.reshape(n_pool, PAGE_SIZE * N_HEADS, HEAD_DIM)
    kbar_s = _page_means(cache_k3, page_table)

    for j in range(n_b):
        final = j == n_b - 1
        q, bias, z = _q_proj(xp, b_norm[j][None], b_w_in, j, cos_p, sin_p, kbar_p, n_seq)
        q_s, z_s = _q_sample(xs, b_norm[j][None], b_w_in, j, cos_s, sin_s)
        picks = _pick_blocks(q_s, kbar_s, k_s)[:, :, :MOBA_TOPK].reshape(-1)
        o, o_s = _moba_layer(q, bias, kb, vt, q_s, k_s, v_s, cache_k3, cache_v3, page_table, picks)
        xp = _b_out(o.reshape(n_seq * seq, D_MODEL), z, xp, b_w_out, j, final_norm[None], ROW_TILE, final)
        xs = _b_out(o_s.reshape(n_dec, D_MODEL), z_s, xs, b_w_out, j, final_norm[None], n_dec, final)

    return (xp.reshape(n_seq, seq, D_MODEL), xs.reshape(n_dec, 1, D_MODEL),
            jnp.stack(st_p_re), jnp.stack(st_p_im), jnp.stack(st_s_re), jnp.stack(st_s_im),
            k_p.reshape(n_seq, seq, N_HEADS, HEAD_DIM), v_p.reshape(n_seq, seq, N_HEADS, HEAD_DIM),
            k_s.reshape(n_dec, 1, N_HEADS, HEAD_DIM), v_s.reshape(n_dec, 1, N_HEADS, HEAD_DIM))
```

```python
import functools
import math

import jax
import jax.numpy as jnp
import numpy as np
from jax import lax
from jax.experimental import pallas as pl
from jax.experimental.pallas import tpu as pltpu

D_MODEL = 1024
SSM_GROUP = 16
SSM_GROUPS = D_MODEL // SSM_GROUP
SSM_STATE = 64
SSM_TBLK = 16
SSM_BLKW = SSM_TBLK * SSM_GROUP
HEAD_DIM = 128
N_HEADS = D_MODEL // HEAD_DIM
MOBA_BLOCK = 256
MOBA_TOPK = 3
PAGE_SIZE = 128
ROPE_THETA = 10000.0
RMS_EPS = 1e-6
NEG_INF = -1e30
LANES = 128
VMEM_LIMIT = 48 * 1024 * 1024
ROW_TILE = 512
MOBA_HEADS = 4
V_ROWS = HEAD_DIM + 16

F32 = jnp.float32
BF16 = jnp.bfloat16


def _params(*sem):
    return pltpu.CompilerParams(dimension_semantics=sem, vmem_limit_bytes=VMEM_LIMIT)


def _rmsnorm(x, g):
    r = lax.rsqrt(jnp.mean(x * x, axis=-1, keepdims=True) + RMS_EPS)
    return x * r * g


def _dot(a, b):
    return jnp.dot(a, b, preferred_element_type=F32)


def _dot_nt(a, b, precision=None):
    return lax.dot_general(a, b, (((1,), (1,)), ((), ())), precision=precision,
                           preferred_element_type=F32)


def _udiv(x, n):
    if n & (n - 1) == 0:
        return lax.shift_right_logical(x, jnp.int32(n.bit_length() - 1))
    return x // n


def _umod(x, n):
    return x & (n - 1) if n & (n - 1) == 0 else x % n


def _weight_spec(w, layer):
    if layer is None:
        return pl.BlockSpec(w.shape, lambda *_: (0,) * w.ndim, pipeline_mode=pl.Buffered(1))
    return pl.BlockSpec((None,) + w.shape[1:], lambda *_: (layer,) + (0,) * (w.ndim - 1),
                        pipeline_mode=pl.Buffered(1))


def _cast_once(*pairs):
    @pl.when(pl.program_id(0) == 0)
    def _():
        for w_ref, w_scr in pairs:
            w_scr[...] = w_ref[...].astype(BF16)


def _rope(x, cos2, sin2):
    return x * cos2 + pltpu.roll(x, HEAD_DIM // 2, 1) * sin2


def _silu(z):
    return z * jax.nn.sigmoid(z)


def _top3(s, valid, axis):
    n_lanes = s.shape[axis]
    lane = lax.broadcasted_iota(jnp.int32, s.shape, axis)
    sm = jnp.where(valid, s, NEG_INF)
    sel = jnp.zeros(s.shape, F32)
    picks = []
    for _ in range(MOBA_TOPK):
        m = jnp.max(sm, axis=axis, keepdims=True)
        idx = jnp.min(jnp.where(sm == m, lane, n_lanes), axis=axis, keepdims=True)
        hit = lane == idx
        sel = jnp.where(hit, jnp.where(valid, 1.0, sel), sel)
        sm = jnp.where(hit, NEG_INF, sm)
        picks.append(idx)
    return sel, picks


GROUPS_PER_TILE = LANES // SSM_GROUP
STEP_TILES = SSM_BLKW // LANES


def _slot_transpose(xs, slot):
    xs = list(xs)
    step = GROUPS_PER_TILE // 2
    while step:
        low = (slot & step) == 0
        for i in range(GROUPS_PER_TILE):
            if i & step == 0:
                a, b = xs[i], xs[i + step]
                xs[i] = jnp.where(low, a, pltpu.roll(b, step * SSM_GROUP, 1))
                xs[i + step] = jnp.where(low, pltpu.roll(a, LANES - step * SSM_GROUP, 1), b)
        step //= 2
    return xs


def _to_blocked(u, tiles_ref, ug_ref):
    nb = ug_ref.shape[0]
    slot = lax.broadcasted_iota(jnp.int32, (nb, LANES), 1) // SSM_GROUP
    for tile in range(D_MODEL // LANES):
        tiles_ref[tile] = u[:, tile * LANES:(tile + 1) * LANES]
    for tile in range(D_MODEL // LANES):
        for hh in range(STEP_TILES):
            steps = [tiles_ref[tile, pl.ds(hh * GROUPS_PER_TILE + rp, nb, stride=SSM_TBLK), :]
                     for rp in range(GROUPS_PER_TILE)]
            for gp, piece in enumerate(_slot_transpose(steps, slot)):
                col = (tile * GROUPS_PER_TILE + gp) * SSM_BLKW + hh * LANES
                ug_ref[:, col:col + LANES] = piece.astype(ug_ref.dtype)


def _from_blocked(yg_ref, tiles_ref):
    nb = yg_ref.shape[0]
    slot = lax.broadcasted_iota(jnp.int32, (nb, LANES), 1) // SSM_GROUP
    for tile in range(D_MODEL // LANES):
        for hh in range(STEP_TILES):
            cols = [(tile * GROUPS_PER_TILE + gp) * SSM_BLKW + hh * LANES for gp in range(GROUPS_PER_TILE)]
            for rp, piece in enumerate(_slot_transpose([yg_ref[:, c:c + LANES] for c in cols], slot)):
                tiles_ref[tile, pl.ds(hh * GROUPS_PER_TILE + rp, nb, stride=SSM_TBLK), :] = piece
    return jnp.concatenate([tiles_ref[tile] for tile in range(D_MODEL // LANES)], axis=1)


def _a_in_kernel(blocked, x_ref, g_ref, w_ref, u_ref, z_ref, *rest):
    w_scr = rest[-1]
    _cast_once((w_ref, w_scr))
    h = _rmsnorm(x_ref[...], g_ref[...]).astype(BF16)
    uz = _dot(h, w_scr[...])
    u_ref[...] = uz[:, :D_MODEL]
    z_ref[...] = uz[:, D_MODEL:]
    if blocked:
        ug_ref, tiles_ref = rest[:2]
        _to_blocked(uz[:, :D_MODEL], tiles_ref, ug_ref)


def _a_in(x, g, w, layer, tm, blocked):
    t = x.shape[0]
    tile = pl.BlockSpec((tm, D_MODEL), lambda i: (i, 0))
    out_specs, out_shape = [tile, tile], [jax.ShapeDtypeStruct((t, D_MODEL), F32)] * 2
    if blocked:
        out_specs.append(pl.BlockSpec((tm // SSM_TBLK, SSM_GROUPS * SSM_BLKW), lambda i: (i, 0)))
        out_shape.append(jax.ShapeDtypeStruct((t // SSM_TBLK, SSM_GROUPS * SSM_BLKW), BF16))
    return pl.pallas_call(
        functools.partial(_a_in_kernel, blocked),
        grid=(t // tm,),
        in_specs=[tile, pl.BlockSpec((1, D_MODEL), lambda i: (0, 0)), _weight_spec(w, layer)],
        out_specs=out_specs,
        out_shape=out_shape,
        scratch_shapes=([pltpu.VMEM((D_MODEL // LANES, tm, LANES), F32)] if blocked else [])
        + [pltpu.VMEM((D_MODEL, 2 * D_MODEL), BF16)],
        compiler_params=_params("arbitrary"),
        name="a_in",
    )(x, g, w)


def _ssm_scan_kernel(n_seq, ug_ref, m_ref, f_ref, e_ref, ap_ref, y_ref, st_ref):
    rows = ug_ref.shape[0]
    nblk = rows // n_seq
    ug = ug_ref[...]
    s = _dot(ug, f_ref[...])
    row = lax.broadcasted_iota(jnp.int32, (nblk, LANES), 0)
    hp = []
    for q in range(n_seq):
        re = s[q * nblk:(q + 1) * nblk, :LANES]
        im = s[q * nblk:(q + 1) * nblk, LANES:]
        step, k = 1, 0
        while step < nblk:
            ar = ap_ref[2 * k:2 * k + 1, :]
            ai = ap_ref[2 * k + 1:2 * k + 2, :]
            sr = jnp.where(row >= step, pltpu.roll(re, step, 0), 0.0)
            si = jnp.where(row >= step, pltpu.roll(im, step, 0), 0.0)
            re, im = re + ar * sr - ai * si, im + ar * si + ai * sr
            step, k = step * 2, k + 1
        st_ref[2 * q:2 * q + 1, :] = re[nblk - 1:nblk, :]
        st_ref[2 * q + 1:2 * q + 2, :] = im[nblk - 1:nblk, :]
        pr = jnp.where(row >= 1, pltpu.roll(re, 1, 0), 0.0)
        pi = jnp.where(row >= 1, pltpu.roll(im, 1, 0), 0.0)
        hp.append(jnp.concatenate([pr, pi], axis=1))
    hprev = jnp.concatenate(hp, axis=0).astype(BF16)
    carry = _dot(hprev, e_ref[...])
    y_ref[:, :SSM_BLKW] = _dot(ug[:, :SSM_BLKW], m_ref[0]) + carry[:, :SSM_BLKW]
    y_ref[:, SSM_BLKW:] = _dot(ug[:, SSM_BLKW:], m_ref[1]) + carry[:, SSM_BLKW:]


def _ssm_scan(ug, prep, n_seq):
    rows = ug.shape[0]
    n_pairs = SSM_GROUPS // 2
    n_pow = prep["apow"].shape[1]
    return pl.pallas_call(
        functools.partial(_ssm_scan_kernel, n_seq),
        grid=(n_pairs,),
        in_specs=[pl.BlockSpec((rows, 2 * SSM_BLKW), lambda p: (0, p)),
                  pl.BlockSpec((2, SSM_BLKW, SSM_BLKW), lambda p: (p, 0, 0)),
                  pl.BlockSpec((None, 2 * SSM_BLKW, 2 * LANES), lambda p: (p, 0, 0)),
                  pl.BlockSpec((None, 2 * LANES, 2 * SSM_BLKW), lambda p: (p, 0, 0)),
                  pl.BlockSpec((None, n_pow, LANES), lambda p: (p, 0, 0))],
        out_specs=[pl.BlockSpec((rows, 2 * SSM_BLKW), lambda p: (0, p)),
                   pl.BlockSpec((None, 2 * n_seq, LANES), lambda p: (p, 0, 0))],
        out_shape=[jax.ShapeDtypeStruct((rows, SSM_GROUPS * SSM_BLKW), F32),
                   jax.ShapeDtypeStruct((n_pairs, 2 * n_seq, LANES), F32)],
        compiler_params=_params("parallel"),
        name="ssm_scan",
    )(ug, prep["m"], prep["f"], prep["e"], prep["apow"])


def _ssm_step_kernel(u_ref, hr_ref, hi_ref, bb_ref, ar_ref, ai_ref, cs_ref, y_ref, or_ref, oi_ref):
    bu = lax.dot_general(u_ref[...], bb_ref[...], (((2,), (1,)), ((0,), (0,))),
                         preferred_element_type=F32)
    h0r, h0i = hr_ref[...], hi_ref[...]
    ar, ai = ar_ref[...], ai_ref[...]
    hr = bu[:, :, :SSM_STATE] + ar * h0r - ai * h0i
    hi = bu[:, :, SSM_STATE:] + ar * h0i + ai * h0r
    or_ref[...] = hr
    oi_ref[...] = hi
    y_ref[...] = lax.dot_general(jnp.concatenate([hr, hi], axis=2), cs_ref[...],
                                 (((2,), (1,)), ((0,), (0,))), preferred_element_type=F32)


def _ssm_step(u_g, h0r, h0i, prep):
    g, n, _ = u_g.shape
    return pl.pallas_call(
        _ssm_step_kernel,
        out_shape=[jax.ShapeDtypeStruct((g, n, SSM_GROUP), F32),
                   jax.ShapeDtypeStruct((g, n, SSM_STATE), F32),
                   jax.ShapeDtypeStruct((g, n, SSM_STATE), F32)],
        compiler_params=pltpu.CompilerParams(vmem_limit_bytes=VMEM_LIMIT),
        name="ssm_step",
    )(u_g, h0r, h0i, prep["bb"], prep["a_re"], prep["a_im"], prep["cs"])


def _a_out_kernel(blocked, ys_ref, u_ref, z_ref, x_ref, d_ref, wg_ref, wo_ref, o_ref, *scratch):
    wg_scr, wo_scr = scratch[-2:]
    _cast_once((wg_ref, wg_scr), (wo_ref, wo_scr))
    ys = _from_blocked(ys_ref, scratch[0]) if blocked else ys_ref[...]
    y = ys + d_ref[...] * u_ref[...]
    g = jax.nn.gelu(y)
    y2 = g * jax.nn.sigmoid(_dot(g.astype(BF16), wg_scr[...]))
    v = (y2 * _silu(z_ref[...])).astype(BF16)
    o_ref[...] = x_ref[...] + _dot(v, wo_scr[...])


def _a_out(ys, u, z, x, d, wg, wo, layer, tm, blocked):
    t = x.shape[0]
    tile = pl.BlockSpec((tm, D_MODEL), lambda i: (i, 0))
    ys_spec = pl.BlockSpec((tm // SSM_TBLK, SSM_GROUPS * SSM_BLKW), lambda i: (i, 0)) if blocked else tile
    return pl.pallas_call(
        functools.partial(_a_out_kernel, blocked),
        grid=(t // tm,),
        in_specs=[ys_spec, tile, tile, tile, pl.BlockSpec((1, D_MODEL), lambda i: (0, 0)),
                  _weight_spec(wg, layer), _weight_spec(wo, layer)],
        out_specs=tile,
        out_shape=jax.ShapeDtypeStruct((t, D_MODEL), F32),
        scratch_shapes=([pltpu.VMEM((D_MODEL // LANES, tm, LANES), F32)] if blocked else [])
        + [pltpu.VMEM((D_MODEL, D_MODEL), BF16)] * 2,
        compiler_params=_params("arbitrary"),
        name="a_out",
    )(ys, u, z, x, d, wg, wo)


def _ssm_prep_kernel(n_steps, lrc_ref, lic_ref, lrr_ref, lir_ref, ldt_ref, bt_re_ref, bt_im_ref, ct_re_ref, ct_im_ref,
                     m_ref, f_ref, e_ref, ap_ref, ab_ref, bb_ref, cs_ref, m32, f32s, e32):
    hi = lax.Precision.HIGHEST
    p = SSM_STATE
    m32[...] = jnp.zeros(m32.shape, F32)
    f32s[...] = jnp.zeros(f32s.shape, F32)
    e32[...] = jnp.zeros(e32.shape, F32)
    for gi in range(2):
        dt = jnp.exp(ldt_ref[gi])

        def discretise(lr, li):
            mag = jnp.exp(lr * dt)
            a_re, a_im = mag * jnp.cos(li * dt), mag * jnp.sin(li * dt)
            den = lr * lr + li * li
            f_re = ((a_re - 1.0) * lr + a_im * li) / den
            f_im = (a_im * lr - (a_re - 1.0) * li) / den
            return a_re, a_im, f_re, f_im

        ac_re, ac_im, _, _ = discretise(lrc_ref[gi], lic_ref[gi])
        lrr, lir = lrr_ref[gi], lir_ref[gi]
        ar_re, ar_im, fr_re, fr_im = discretise(lrr, lir)
        ab_ref[gi, 0:1, :] = ar_re
        ab_ref[gi, 1:2, :] = ar_im
        ct_re, ct_im = ct_re_ref[gi], ct_im_ref[gi]
        cs_ref[gi] = jnp.concatenate([ct_re, -ct_im], axis=0)
        bbt_re = fr_re * bt_re_ref[gi] - fr_im * bt_im_ref[gi]
        bbt_im = fr_re * bt_im_ref[gi] + fr_im * bt_re_ref[gi]
        bb_ref[gi, :, :p] = bbt_re
        bb_ref[gi, :, p:] = bbt_im
        for k in range(n_steps):
            n = float(SSM_TBLK * 2 ** k)
            mag = jnp.exp(lrr * dt * n)
            ap_ref[2 * k:2 * k + 1, gi * p:(gi + 1) * p] = mag * jnp.cos(lir * dt * n)
            ap_ref[2 * k + 1:2 * k + 2, gi * p:(gi + 1) * p] = mag * jnp.sin(lir * dt * n)
        pc_re, pc_im = jnp.ones_like(ac_re), jnp.zeros_like(ac_im)
        pr_re, pr_im = jnp.ones_like(ar_re), jnp.zeros_like(ar_im)
        for t in range(SSM_TBLK + 1):
            if t:
                pc_re, pc_im = pc_re * ac_re - pc_im * ac_im, pc_re * ac_im + pc_im * ac_re
                pr_re, pr_im = pr_re * ar_re - pr_im * ar_im, pr_re * ar_im + pr_im * ar_re
            ca_re = pc_re * ct_re - pc_im * ct_im
            nca_im = -(pc_re * ct_im + pc_im * ct_re)
            if t:
                cols = slice(gi * SSM_BLKW + (t - 1) * SSM_GROUP, gi * SSM_BLKW + t * SSM_GROUP)
                e32[gi * p:(gi + 1) * p, cols] = ca_re
                e32[(2 + gi) * p:(3 + gi) * p, cols] = nca_im
            if t < SSM_TBLK:
                r = SSM_TBLK - 1 - t
                rows = slice(gi * SSM_BLKW + r * SSM_GROUP, gi * SSM_BLKW + (r + 1) * SSM_GROUP)
                f32s[rows, gi * p:(gi + 1) * p] = pr_re * bbt_re - pr_im * bbt_im
                f32s[rows, (2 + gi) * p:(3 + gi) * p] = pr_re * bbt_im + pr_im * bbt_re
                kt = (jnp.dot(bbt_re, ca_re, precision=hi, preferred_element_type=F32)
                      + jnp.dot(bbt_im, nca_im, precision=hi, preferred_element_type=F32))
                for r_in in range(SSM_TBLK - t):
                    r_out = r_in + t
                    m32[gi, r_in * SSM_GROUP:(r_in + 1) * SSM_GROUP, r_out * SSM_GROUP:(r_out + 1) * SSM_GROUP] = kt
    m_ref[...] = m32[...].astype(BF16)
    f_ref[...] = f32s[...].astype(BF16)
    e_ref[...] = e32[...].astype(BF16)


def _ssm_prep(lam_re, lam_im, log_dt, b_re, b_im, c_re, c_im, nblk):
    g, p = lam_re.shape
    n_pairs = g // 2
    n_steps = max(1, math.ceil(math.log2(nblk)))
    col = lambda x: x.reshape(g, p, 1)
    row = lambda x: x.reshape(g, 1, p)
    tr = lambda x: x.transpose(0, 2, 1)

    def two(*shape):
        return pl.BlockSpec((2,) + shape, lambda i: (i,) + (0,) * len(shape))

    def one(*shape):
        return pl.BlockSpec((None,) + shape, lambda i: (i,) + (0,) * len(shape))

    m, fp, ep, ap, ab, bb, cs = pl.pallas_call(
        functools.partial(_ssm_prep_kernel, n_steps),
        grid=(n_pairs,),
        in_specs=[two(p, 1), two(p, 1), two(1, p), two(1, p), two(1, 1),
                  two(SSM_GROUP, p), two(SSM_GROUP, p), two(p, SSM_GROUP), two(p, SSM_GROUP)],
        out_specs=[two(SSM_BLKW, SSM_BLKW), one(2 * SSM_BLKW, 4 * p), one(4 * p, 2 * SSM_BLKW),
                   one(2 * n_steps, 2 * p), two(2, p), two(SSM_GROUP, 2 * p), two(2 * p, SSM_GROUP)],
        out_shape=[jax.ShapeDtypeStruct((g, SSM_BLKW, SSM_BLKW), BF16),
                   jax.ShapeDtypeStruct((n_pairs, 2 * SSM_BLKW, 4 * p), BF16),
                   jax.ShapeDtypeStruct((n_pairs, 4 * p, 2 * SSM_BLKW), BF16),
                   jax.ShapeDtypeStruct((n_pairs, 2 * n_steps, 2 * p), F32),
                   jax.ShapeDtypeStruct((g, 2, p), F32),
                   jax.ShapeDtypeStruct((g, SSM_GROUP, 2 * p), F32),
                   jax.ShapeDtypeStruct((g, 2 * p, SSM_GROUP), F32)],
        scratch_shapes=[pltpu.VMEM((2, SSM_BLKW, SSM_BLKW), F32), pltpu.VMEM((2 * SSM_BLKW, 4 * p), F32),
                        pltpu.VMEM((4 * p, 2 * SSM_BLKW), F32)],
        compiler_params=_params("parallel"),
        name="ssm_prep",
    )(col(lam_re), col(lam_im), row(lam_re), row(lam_im), log_dt.reshape(g, 1, 1),
      tr(b_re), tr(b_im), tr(c_re), tr(c_im))
    return {"m": m, "f": fp, "e": ep, "apow": ap, "bb": bb,
            "a_re": ab[:, 0:1, :], "a_im": ab[:, 1:2, :], "cs": cs}


def _s5_layer_prompt(x, n_seq, layer, norm, w_in, prep, d, wg, wo):
    u, z, ug = _a_in(x, norm, w_in, layer, ROW_TILE, True)
    yg, st = _ssm_scan(ug, prep, n_seq)
    st = st.reshape(SSM_GROUPS // 2, n_seq, 2, 2, SSM_STATE)
    st = st.transpose(2, 1, 0, 3, 4).reshape(2, n_seq, SSM_GROUPS, SSM_STATE)
    return _a_out(yg, u, z, x, d, wg, wo, layer, ROW_TILE, True), st[0], st[1]


def _s5_layer_sample(x, h0r, h0i, layer, norm, w_in, prep, d, wg, wo):
    n = x.shape[0]
    u, z = _a_in(x, norm, w_in, layer, n, False)
    u_g = u.reshape(n, SSM_GROUPS, SSM_GROUP).transpose(1, 0, 2)
    y_g, hr, hi = _ssm_step(u_g, h0r.transpose(1, 0, 2), h0i.transpose(1, 0, 2), prep)
    ys = y_g.transpose(1, 0, 2).reshape(n, D_MODEL)
    return _a_out(ys, u, z, x, d, wg, wo, layer, n, False), hr.transpose(1, 0, 2), hi.transpose(1, 0, 2)


def _kv_kernel(prompt, x_ref, g_ref, w_ref, cos_ref, sin_ref, k_ref, v_ref, *rest):
    w_scr = rest[-1]
    _cast_once((w_ref, w_scr))
    h = _rmsnorm(x_ref[...], g_ref[...]).astype(BF16)
    kv = _dot(h, w_scr[...])
    cos2, sin2 = cos_ref[...], sin_ref[...]
    tm = kv.shape[0]
    for hd in range(N_HEADS):
        sl = slice(hd * HEAD_DIM, (hd + 1) * HEAD_DIM)
        k = _rope(kv[:, sl], cos2, sin2)
        v = kv[:, D_MODEL + hd * HEAD_DIM:D_MODEL + (hd + 1) * HEAD_DIM]
        k_ref[:, sl] = k
        v_ref[:, sl] = v
        if prompt:
            kb_ref, vt_ref, mean_ref = rest[:3]
            kb_ref[hd] = k.astype(BF16)
            for b in range(tm // MOBA_BLOCK):
                rows = slice(b * MOBA_BLOCK, (b + 1) * MOBA_BLOCK)
                mean_ref[b, :, sl] = jnp.sum(k[rows], axis=0, keepdims=True) * (1.0 / MOBA_BLOCK)
                vt_ref[hd, b, :HEAD_DIM, :] = v[rows].T.astype(BF16)
                vt_ref[hd, b, HEAD_DIM:, :] = jnp.ones((V_ROWS - HEAD_DIM, MOBA_BLOCK), BF16)


def _kv_proj(x, g, w, cos2, sin2, n_seq, tm, prompt):
    t = x.shape[0]
    per_seq = t // n_seq // tm
    tile = pl.BlockSpec((tm, D_MODEL), lambda i: (i, 0))
    rope_tile = pl.BlockSpec((tm, HEAD_DIM), lambda i: (i % per_seq, 0))
    out_specs = [tile, tile]
    out_shape = [jax.ShapeDtypeStruct((t, D_MODEL), F32)] * 2
    if prompt:
        nb = tm // MOBA_BLOCK
        out_specs += [
            pl.BlockSpec((None, N_HEADS, tm, HEAD_DIM), lambda i: (i // per_seq, 0, i % per_seq, 0)),
            pl.BlockSpec((None, N_HEADS, nb, V_ROWS, MOBA_BLOCK), lambda i: (i // per_seq, 0, i % per_seq, 0, 0)),
            pl.BlockSpec((nb, 1, D_MODEL), lambda i: (i, 0, 0))]
        out_shape += [
            jax.ShapeDtypeStruct((n_seq, N_HEADS, t // n_seq, HEAD_DIM), BF16),
            jax.ShapeDtypeStruct((n_seq, N_HEADS, t // n_seq // MOBA_BLOCK, V_ROWS, MOBA_BLOCK), BF16),
            jax.ShapeDtypeStruct((t // MOBA_BLOCK, 1, D_MODEL), F32)]
    return pl.pallas_call(
        functools.partial(_kv_kernel, prompt),
        grid=(t // tm,),
        in_specs=[tile, pl.BlockSpec((1, D_MODEL), lambda i: (0, 0)), _weight_spec(w, None),
                  rope_tile, rope_tile],
        out_specs=out_specs,
        out_shape=out_shape,
        scratch_shapes=[pltpu.VMEM((D_MODEL, 2 * D_MODEL), BF16)],
        compiler_params=_params("arbitrary"),
        name="kv_proj",
    )(x, g, w, cos2, sin2)


def _q_kernel(blocks_per_seq, x_ref, g_ref, w_ref, cos_ref, sin_ref, kbar_ref, q_ref, bias_ref, z_ref, w_scr):
    _cast_once((w_ref, w_scr))
    tm = x_ref.shape[0]
    per_tile = tm // MOBA_BLOCK
    first = (pl.program_id(0) * per_tile) % blocks_per_seq
    h = _rmsnorm(x_ref[...], g_ref[...]).astype(BF16)
    qz = _dot(h, w_scr[...])
    z_ref[...] = qz[:, D_MODEL:]
    cos2, sin2 = cos_ref[...], sin_ref[...]
    blk = lax.broadcasted_iota(jnp.int32, (blocks_per_seq, MOBA_BLOCK), 0)
    for hd in range(N_HEADS):
        q = _rope(qz[:, hd * HEAD_DIM:(hd + 1) * HEAD_DIM], cos2, sin2)
        q_ref[hd] = (q * (HEAD_DIM ** -0.5)).astype(BF16)
        for b in range(per_tile):
            rows = slice(b * MOBA_BLOCK, (b + 1) * MOBA_BLOCK)
            s = _dot_nt(kbar_ref[hd], q[rows], precision=lax.Precision.HIGHEST)
            sel, _ = _top3(s, blk < first + b, 0)
            bias_ref[hd, b] = jnp.where(sel > 0.5, 0.0, NEG_INF)


def _q_proj(x, g, w, layer, cos2, sin2, kbar, n_seq):
    t = x.shape[0]
    tm = ROW_TILE
    bps = t // n_seq // MOBA_BLOCK
    per_tile = tm // MOBA_BLOCK
    tps = bps // per_tile
    tile = pl.BlockSpec((tm, D_MODEL), lambda i: (i, 0))
    rope_tile = pl.BlockSpec((tm, HEAD_DIM), lambda i: (i % tps, 0))
    return pl.pallas_call(
        functools.partial(_q_kernel, bps),
        grid=(t // tm,),
        in_specs=[tile, pl.BlockSpec((1, D_MODEL), lambda i: (0, 0)), _weight_spec(w, layer),
                  rope_tile, rope_tile,
                  pl.BlockSpec((None, N_HEADS, bps, HEAD_DIM), lambda i: (i // tps, 0, 0, 0))],
        out_specs=[pl.BlockSpec((None, N_HEADS, tm, HEAD_DIM), lambda i: (i // tps, 0, i % tps, 0)),
                   pl.BlockSpec((None, N_HEADS, per_tile, bps, MOBA_BLOCK), lambda i: (i // tps, 0, i % tps, 0, 0)),
                   tile],
        out_shape=[jax.ShapeDtypeStruct((n_seq, N_HEADS, t // n_seq, HEAD_DIM), BF16),
                   jax.ShapeDtypeStruct((n_seq, N_HEADS, bps, bps, MOBA_BLOCK), F32),
                   jax.ShapeDtypeStruct((t, D_MODEL), F32)],
        scratch_shapes=[pltpu.VMEM((D_MODEL, 2 * D_MODEL), BF16)],
        compiler_params=_params("arbitrary"),
        name="q_proj",
    )(x, g, w, cos2, sin2, kbar)


def _moba_layer_kernel(items, pk_ref, pt_ref, q_ref, bias_ref, k_ref, vt_ref, qs_ref, kn_ref, vn_ref, *rest):
    n_pages = items * N_FETCH
    k_pages, v_pages = rest[:n_pages], rest[n_pages:2 * n_pages]
    o_ref, os_ref, s_a, s_b, acc_scr = rest[2 * n_pages:]
    step = (pl.program_id(0) * pl.num_programs(1) + pl.program_id(1)) * pl.num_programs(2) + pl.program_id(2)
    head0 = _umod(step * items, N_HEADS)
    for a in range(items):
        pages = slice(a * N_FETCH, (a + 1) * N_FETCH)
        os_ref[a] = _decode_attend(head0 + a, qs_ref[a], kn_ref[a], vn_ref[a], k_pages[pages], v_pages[pages])
    _moba_tile(q_ref, bias_ref, k_ref, vt_ref, o_ref, s_a, s_b, acc_scr)


def _moba_tile(q_ref, bias_ref, k_ref, vt_ref, o_ref, s_a, s_b, acc_scr):
    i = pl.program_id(2)
    nblk = bias_ref.shape[1]
    row = lax.broadcasted_iota(jnp.int32, (MOBA_BLOCK, LANES), 0)
    col = lax.broadcasted_iota(jnp.int32, (MOBA_BLOCK, LANES), 1)
    chains = [(h, c) for h in range(MOBA_HEADS) for c in range(MOBA_BLOCK // LANES)]
    nc = len(chains)
    trips = (i + 1) // 2
    last_trip = nblk // 2 - 1

    def scores(h, c, blk):
        off = pl.multiple_of(blk * MOBA_BLOCK, MOBA_BLOCK)
        return _dot_nt(k_ref[h, pl.ds(off, MOBA_BLOCK), :], q_ref[h, c * LANES:(c + 1) * LANES, :])

    def stage_scores(t, buf):
        mbs = []
        for n, (h, c) in enumerate(chains):
            for u in range(2):
                s = scores(h, c, 2 * t + u)
                buf[u * nc + n] = s
                mbs.append(jnp.max(s, axis=0, keepdims=True))
        return tuple(mbs)

    def consume(t, buf, ms, mbs):
        out = []
        for n, (h, c) in enumerate(chains):
            b = [bias_ref[h, pl.ds(2 * t + u, 1), :][:, c * LANES:(c + 1) * LANES] for u in range(2)]
            m_new = jnp.maximum(ms[n], jnp.maximum(mbs[2 * n] + b[0], mbs[2 * n + 1] + b[1]))
            alpha = jnp.exp(ms[n] - m_new)
            p = [jnp.exp(buf[u * nc + n] - (m_new - b[u])).astype(BF16) for u in range(2)]
            acc_scr[n] = alpha * acc_scr[n] + (_dot(vt_ref[h, 2 * t], p[0]) + _dot(vt_ref[h, 2 * t + 1], p[1]))
            out.append(m_new)
        return tuple(out)

    for n, (h, c) in enumerate(chains):
        s_b[n] = scores(h, c, i)
    mbs_a = stage_scores(0, s_a)
    ms = []
    for n, (h, c) in enumerate(chains):
        s = jnp.where(row <= col + c * LANES, s_b[n], NEG_INF)
        m = jnp.max(s, axis=0, keepdims=True)
        acc_scr[n] = _dot(vt_ref[h, i], jnp.exp(s - m).astype(BF16))
        ms.append(m)

    def body(tt, carry):
        ms, mbs_a = carry
        t0 = 2 * tt
        mbs_b = stage_scores(t0 + 1, s_b)
        ms = consume(t0, s_a, ms, mbs_a)
        mbs_a = stage_scores(jnp.minimum(t0 + 2, last_trip), s_a)
        ms = consume(t0 + 1, s_b, ms, mbs_b)
        return ms, mbs_a

    lax.fori_loop(0, (trips + 1) // 2, body, (tuple(ms), mbs_a))
    for n, (h, c) in enumerate(chains):
        acc = acc_scr[n]
        o = acc[:HEAD_DIM, :] / acc[HEAD_DIM:HEAD_DIM + 1, :]
        o_ref[c * LANES:(c + 1) * LANES, h * HEAD_DIM:(h + 1) * HEAD_DIM] = o.T.astype(o_ref.dtype)


def _moba_layer(q, bias, kb, vt, q_s, k_new, v_new, cache_k3, cache_v3, page_table, picks):
    n_seq, _, seq, _ = q.shape
    n_dec, n_pages = page_table.shape
    nblk = seq // MOBA_BLOCK
    hg = MOBA_HEADS
    n_hg = N_HEADS // hg
    n_chains = hg * (MOBA_BLOCK // LANES)
    steps = n_seq * n_hg * nblk
    items = n_dec * N_HEADS // steps
    assert items * steps == n_dec * N_HEADS and N_HEADS % items == 0
    per_block = MOBA_BLOCK // PAGE_SIZE

    def first_item(n, h, i):
        return ((n * n_hg + h) * nblk + i) * items

    def dec_index(n, h, i, pk, pt):
        e = first_item(n, h, i)
        return (_udiv(e, N_HEADS), _udiv(_umod(e, N_HEADS), items), 0, 0)

    def page_spec(a, f):
        def index(n, h, i, pk, pt):
            e = first_item(n, h, i) + a
            blk = pk[e * MOBA_TOPK + f // per_block]
            return (pt[_udiv(e, N_HEADS) * n_pages + blk * per_block + f % per_block], 0, 0)
        return pl.BlockSpec((None, PAGE_SIZE * N_HEADS, HEAD_DIM), index)

    dec_rows = pl.BlockSpec((None, items, 1, HEAD_DIM), dec_index)
    page_specs = [page_spec(a, f) for a in range(items) for f in range(N_FETCH)]
    resident = pl.Buffered(1)
    dec_shape = (n_dec, N_HEADS, 1, HEAD_DIM)
    return pl.pallas_call(
        functools.partial(_moba_layer_kernel, items),
        grid_spec=pltpu.PrefetchScalarGridSpec(
            num_scalar_prefetch=2,
            grid=(n_seq, n_hg, nblk),
            in_specs=[pl.BlockSpec((None, hg, MOBA_BLOCK, HEAD_DIM), lambda n, h, i, pk, pt: (n, h, i, 0)),
                      pl.BlockSpec((None, hg, None, nblk, MOBA_BLOCK), lambda n, h, i, pk, pt: (n, h, i, 0, 0)),
                      pl.BlockSpec((None, hg, seq, HEAD_DIM), lambda n, h, i, pk, pt: (n, h, 0, 0),
                                   pipeline_mode=resident),
                      pl.BlockSpec((None, hg, nblk, V_ROWS, MOBA_BLOCK), lambda n, h, i, pk, pt: (n, h, 0, 0, 0),
                                   pipeline_mode=resident),
                      dec_rows, dec_rows, dec_rows] + page_specs + page_specs,
            out_specs=[pl.BlockSpec((None, MOBA_BLOCK, hg * HEAD_DIM), lambda n, h, i, pk, pt: (n, i, h)),
                       dec_rows],
            scratch_shapes=[pltpu.VMEM((2 * n_chains, MOBA_BLOCK, LANES), F32),
                            pltpu.VMEM((2 * n_chains, MOBA_BLOCK, LANES), F32),
                            pltpu.VMEM((n_chains, V_ROWS, LANES), F32)]),
        out_shape=[jax.ShapeDtypeStruct((n_seq, seq, D_MODEL), BF16),
                   jax.ShapeDtypeStruct(dec_shape, F32)],
        compiler_params=_params("arbitrary", "arbitrary", "arbitrary"),
        name="moba_layer",
    )(picks, page_table.reshape(-1), q, bias, kb, vt,
      q_s.reshape(dec_shape), k_new.reshape(dec_shape), v_new.reshape(dec_shape),
      *([cache_k3] * len(page_specs)), *([cache_v3] * len(page_specs)))


def _b_out_kernel(final, o_ref, z_ref, x_ref, w_ref, g_ref, y_ref, w_scr):
    _cast_once((w_ref, w_scr))
    v = (o_ref[...].astype(F32) * _silu(z_ref[...])).astype(BF16)
    x = x_ref[...] + _dot(v, w_scr[...])
    y_ref[...] = _rmsnorm(x, g_ref[...]) if final else x


def _b_out(o, z, x, w, layer, g, tm, final):
    t = x.shape[0]
    tile = pl.BlockSpec((tm, D_MODEL), lambda i: (i, 0))
    return pl.pallas_call(
        functools.partial(_b_out_kernel, final),
        grid=(t // tm,),
        in_specs=[tile, tile, tile, _weight_spec(w, layer), pl.BlockSpec((1, D_MODEL), lambda i: (0, 0))],
        out_specs=tile,
        out_shape=jax.ShapeDtypeStruct((t, D_MODEL), F32),
        scratch_shapes=[pltpu.VMEM((D_MODEL, D_MODEL), BF16)],
        compiler_params=_params("arbitrary"),
        name="b_out",
    )(o, z, x, w, g)


PAGES_PER_STEP = 32


def _page_mean_kernel(*refs):
    pages, out_ref = refs[1:1 + PAGES_PER_STEP], refs[-1]
    per_block = MOBA_BLOCK // PAGE_SIZE
    for b in range(PAGES_PER_STEP // per_block):
        tot = None
        for r in range(per_block):
            page = pages[per_block * b + r][...].reshape(PAGE_SIZE, N_HEADS, HEAD_DIM)
            part = jnp.sum(page, axis=0)
            tot = part if tot is None else tot + part
        tot = tot * (1.0 / MOBA_BLOCK)
        for hd in range(N_HEADS):
            out_ref[hd, b:b + 1, :] = tot[hd:hd + 1, :]


def _page_means(cache_k2, page_table):
    n, n_pages = page_table.shape
    steps = n_pages // PAGES_PER_STEP
    nb = PAGES_PER_STEP * PAGE_SIZE // MOBA_BLOCK

    def page_spec(r):
        return pl.BlockSpec((None, PAGE_SIZE * N_HEADS, HEAD_DIM),
                            lambda s, j, pt, r=r: (pt[s * n_pages + j * PAGES_PER_STEP + r], 0, 0))

    return pl.pallas_call(
        _page_mean_kernel,
        grid_spec=pltpu.PrefetchScalarGridSpec(
            num_scalar_prefetch=1,
            grid=(n, steps),
            in_specs=[page_spec(r) for r in range(PAGES_PER_STEP)],
            out_specs=pl.BlockSpec((None, N_HEADS, nb, HEAD_DIM), lambda s, j, pt: (s, 0, j, 0))),
        out_shape=jax.ShapeDtypeStruct((n, N_HEADS, n_pages * PAGE_SIZE // MOBA_BLOCK, HEAD_DIM), F32),
        compiler_params=_params("parallel", "arbitrary"),
        name="page_means",
    )(page_table.reshape(-1), *([cache_k2] * PAGES_PER_STEP))


def _q_sample_kernel(x_ref, g_ref, w_ref, cos_ref, sin_ref, q_ref, z_ref):
    h = _rmsnorm(x_ref[...], g_ref[...]).astype(BF16)
    qz = _dot(h, w_ref[...].astype(BF16))
    z_ref[...] = qz[:, D_MODEL:]
    cos2, sin2 = cos_ref[...], sin_ref[...]
    for hd in range(N_HEADS):
        sl = slice(hd * HEAD_DIM, (hd + 1) * HEAD_DIM)
        q_ref[:, sl] = _rope(qz[:, sl], cos2, sin2)


def _q_sample(x, g, w, layer, cos2, sin2):
    n = x.shape[0]
    whole = lambda a: pl.BlockSpec(a.shape, lambda i: (0,) * a.ndim)
    return pl.pallas_call(
        _q_sample_kernel,
        grid=(1,),
        in_specs=[whole(x), whole(g), _weight_spec(w, layer), whole(cos2), whole(sin2)],
        out_specs=[pl.BlockSpec((n, D_MODEL), lambda i: (0, 0))] * 2,
        out_shape=[jax.ShapeDtypeStruct((n, D_MODEL), F32)] * 2,
        compiler_params=_params("arbitrary"),
        name="q_sample",
    )(x, g, w, cos2, sin2)


PICK_UNROLL = 4


def _pick_kernel(q_ref, kbar_ref, knew_ref, idx_ref):
    n_seq, _, n_past, _ = kbar_ref.shape
    lane_o = lax.broadcasted_iota(jnp.int32, (8, LANES), 1)

    def one_sequence(sq):
        q_row, knew_row = q_ref[pl.ds(sq, 1), :], knew_ref[pl.ds(sq, 1), :]
        scores = None
        for hd in range(N_HEADS):
            sl = slice(hd * HEAD_DIM, (hd + 1) * HEAD_DIM)
            q8 = jnp.broadcast_to(q_row[:, sl], (8, HEAD_DIM))
            own_mean = jnp.broadcast_to(knew_row[:, sl] * (1.0 / MOBA_BLOCK), (8, HEAD_DIM))
            means = jnp.concatenate([kbar_ref[sq, hd], own_mean], axis=0)
            s = _dot_nt(q8, means, precision=lax.Precision.HIGHEST)
            head_row = lax.broadcasted_iota(jnp.int32, s.shape, 0) == hd
            scores = s if scores is None else jnp.where(head_row, s, scores)
        lane = lax.broadcasted_iota(jnp.int32, scores.shape, 1)
        _, picks = _top3(scores, lane < n_past, 1)
        out = jnp.zeros((8, LANES), jnp.int32)
        for r, idx in enumerate(picks):
            out = jnp.where(lane_o == r, idx, out)
        idx_ref[sq] = out

    def some_sequences(i, carry):
        for j in range(PICK_UNROLL):
            one_sequence(i * PICK_UNROLL + j)
        return carry

    lax.fori_loop(0, n_seq // PICK_UNROLL, some_sequences, 0)


def _pick_blocks(q, kbar, knew):
    n = kbar.shape[0]
    assert n % PICK_UNROLL == 0
    return pl.pallas_call(
        _pick_kernel,
        out_shape=jax.ShapeDtypeStruct((n, 8, LANES), jnp.int32),
        compiler_params=pltpu.CompilerParams(vmem_limit_bytes=VMEM_LIMIT),
        name="pick_blocks",
    )(q, kbar, knew)


N_FETCH = MOBA_TOPK * (MOBA_BLOCK // PAGE_SIZE)


def _decode_attend(head, q, k_new, v_new, k_pages, v_pages):
    scale = HEAD_DIM ** -0.5
    q8 = jnp.broadcast_to(q, (8, HEAD_DIM)).astype(BF16)
    head_rows = pl.ds(head, PAGE_SIZE, stride=N_HEADS)
    kc = jnp.concatenate([r[head_rows, :].astype(BF16) for r in k_pages], axis=0)
    vc = jnp.concatenate([r[head_rows, :].astype(BF16) for r in v_pages], axis=0)
    s = _dot_nt(q8, kc)[0:1, :] * scale
    s_new = jnp.sum(q * k_new, axis=1, keepdims=True) * scale
    m = jnp.maximum(jnp.max(s, axis=1, keepdims=True), s_new)
    p = jnp.exp(s - m)
    p_new = jnp.exp(s_new - m)
    l = jnp.sum(p, axis=1, keepdims=True) + p_new
    pv = _dot(jnp.broadcast_to(p, (8, p.shape[1])).astype(BF16), vc)[0:1, :]
    return (pv + p_new * v_new) / l


def _rope_tables(pos):
    half = HEAD_DIM // 2
    inv = ROPE_THETA ** (-np.arange(half, dtype=np.float64) / half)
    ang = np.asarray(pos, np.float64)[:, None] * inv[None, :]
    cos, sin = np.cos(ang), np.sin(ang)
    return (jnp.asarray(np.concatenate([cos, cos], axis=1), F32),
            jnp.asarray(np.concatenate([-sin, sin], axis=1), F32))


def kernel(x_prompt, x_sample, state_ssm_re, state_ssm_im, cache_k, cache_v, page_table, a_norm, a_w_in, a_lam_re, a_lam_im, a_log_dt, a_b_re, a_b_im, a_c_re, a_c_im, a_d, a_w_glu, a_w_out, kv_norm, w_kv, b_norm, b_w_in, b_w_out, final_norm):
    n_seq, seq, _ = x_prompt.shape
    n_dec = x_sample.shape[0]
    n_pool = cache_k.shape[0]
    past_len = page_table.shape[1] * PAGE_SIZE
    assert x_sample.shape[1] == 1 and seq % ROW_TILE == 0 and ROW_TILE % MOBA_BLOCK == 0
    assert past_len % (PAGES_PER_STEP * PAGE_SIZE) == 0 and past_len // MOBA_BLOCK >= MOBA_TOPK
    assert seq // MOBA_BLOCK <= LANES and (seq // MOBA_BLOCK) % 8 == 0
    n_a, n_b = a_norm.shape[0], b_norm.shape[0]

    xp = x_prompt.reshape(n_seq * seq, D_MODEL)
    xs = x_sample.reshape(n_dec, D_MODEL)
    cos_p, sin_p = _rope_tables(np.arange(seq))
    cos_s, sin_s = _rope_tables(np.full((n_dec,), past_len))

    st_p_re, st_p_im, st_s_re, st_s_im = [], [], [], []
    for l in range(n_a):
        prep = _ssm_prep(a_lam_re[l], a_lam_im[l], a_log_dt[l], a_b_re[l], a_b_im[l], a_c_re[l], a_c_im[l],
                         seq // SSM_TBLK)
        norm, d = a_norm[l][None], a_d[l][None]
        xp, hr, hi = _s5_layer_prompt(xp, n_seq, l, norm, a_w_in, prep, d, a_w_glu, a_w_out)
        st_p_re.append(hr)
        st_p_im.append(hi)
        xs, hr, hi = _s5_layer_sample(xs, state_ssm_re[l], state_ssm_im[l], l, norm, a_w_in, prep, d, a_w_glu, a_w_out)
        st_s_re.append(hr)
        st_s_im.append(hi)

    k_p, v_p, kb, vt, kbar_p = _kv_proj(xp, kv_norm[None], w_kv, cos_p, sin_p, n_seq, ROW_TILE, True)
    k_s, v_s = _kv_proj(xs, kv_norm[None], w_kv, cos_s, sin_s, 1, n_dec, False)
    nblk = seq // MOBA_BLOCK
    kbar_p = kbar_p.reshape(n_seq, nblk, N_HEADS, HEAD_DIM).transpose(0, 2, 1, 3)
    cache_k3 = cache_k.reshape(n_pool, PAGE_SIZE * N_HEADS, HEAD_DIM)
    cache_v3 = cache_v.reshape(n_pool, PAGE_SIZE * N_HEADS, HEAD_DIM)
    kbar_s = _page_means(cache_k3, page_table)

    for j in range(n_b):
        final = j == n_b - 1
        q, bias, z = _q_proj(xp, b_norm[j][None], b_w_in, j, cos_p, sin_p, kbar_p, n_seq)
        q_s, z_s = _q_sample(xs, b_norm[j][None], b_w_in, j, cos_s, sin_s)
        picks = _pick_blocks(q_s, kbar_s, k_s)[:, :, :MOBA_TOPK].reshape(-1)
        o, o_s = _moba_layer(q, bias, kb, vt, q_s, k_s, v_s, cache_k3, cache_v3, page_table, picks)
        xp = _b_out(o.reshape(n_seq * seq, D_MODEL), z, xp, b_w_out, j, final_norm[None], ROW_TILE, final)
        xs = _b_out(o_s.reshape(n_dec, D_MODEL), z_s, xs, b_w_out, j, final_norm[None], n_dec, final)

    return (xp.reshape(n_seq, seq, D_MODEL), xs.reshape(n_dec, 1, D_MODEL),
            jnp.stack(st_p_re), jnp.stack(st_p_im), jnp.stack(st_s_re), jnp.stack(st_s_im),
            k_p.reshape(n_seq, seq, N_HEADS, HEAD_DIM), v_p.reshape(n_seq, seq, N_HEADS, HEAD_DIM),
            k_s.reshape(n_dec, 1, N_HEADS, HEAD_DIM), v_s.reshape(n_dec, 1, N_HEADS, HEAD_DIM))
```
